```python
import math
import jax, jax.numpy as jnp
from jax import lax
import numpy as np

D_MODEL = 1024
BATCH = 32
SEQ = 256
DEPTH = 2
DEC_BATCH = 4
DEC_SEQ = 2048
PAST_LEN = 256

GRID_W = 64
H_R = 4
DK_R = 64
DV_R = 128
H_M = 8
P_M = 64
N_M = 64
G_M = 2
D_CONV = 5
H_C = 4
E_C = 128
DV_C = 128
BR_W = 512
N_BRANCH = 3
RET_QK_W = H_R * DK_R
CONV_CH = BR_W + 2 * G_M * N_M
SSM_DT_W = 2 * H_M
IN_SPLITS = (RET_QK_W, RET_QK_W, BR_W, BR_W, BR_W, CONV_CH, SSM_DT_W, BR_W, 2 * BR_W, BR_W, BR_W, N_BRANCH * D_MODEL)
N_IN = sum(IN_SPLITS)
CHUNK_SCALAR = 128
CHUNK_VECTOR = 16
N_EXPERTS = 64
TOP_K = 6
N_GROUPS = 8
TOPK_GROUPS = 4
D_EXPERT = 256
D_SHARED = 256
ROUTED_SCALE = 2.5
MOE_BLOCK = 128
EPS = 1e-6
ROPE_BASE = 10000.0

kernel_name = 'bidir_retention_ssd_hgrn2_moe_dit_step'

F32 = jnp.float32


def rms_norm(x, g):
    xf = x.astype(F32)
    y = xf * lax.rsqrt(jnp.mean(xf * xf, axis=-1, keepdims=True) + EPS)
    return (y * g).astype(x.dtype)


def head_norm(x, g, n_heads, center):
    b, l, w = x.shape
    xf = x.astype(F32).reshape(b, l, n_heads, w // n_heads)
    if center:
        xf = xf - jnp.mean(xf, axis=-1, keepdims=True)
    xf = xf * lax.rsqrt(jnp.mean(xf * xf, axis=-1, keepdims=True) + EPS)
    return (xf.reshape(b, l, w) * g).astype(x.dtype)


def grid_rope(l, dtype):
    rows = l // GRID_W
    row = jnp.repeat(jnp.arange(rows), GRID_W).astype(F32)
    col = (jnp.arange(rows * GRID_W) % GRID_W).astype(F32)
    n_freq = DK_R // 4
    freqs = ROPE_BASE ** (-jnp.arange(n_freq, dtype=F32) / n_freq)
    ang = jnp.concatenate([row[:, None] * freqs, col[:, None] * freqs], axis=-1)
    return jnp.cos(ang).astype(dtype), jnp.sin(ang).astype(dtype)


def apply_rope(x, cos, sin):
    half = DK_R // 2
    x1, x2 = x[..., :half], x[..., half:]
    cs, sn = cos[None, :, None, :], sin[None, :, None, :]
    return jnp.concatenate([x1 * cs - x2 * sn, x1 * sn + x2 * cs], axis=-1)


def chunked_recurrence(q, k, v, log_a, s0, chunk):
    b, h, l, dk = q.shape
    dv = v.shape[-1]
    da = log_a.shape[-1]
    n = l // chunk
    qc = q.astype(F32).reshape(b, h, n, chunk, dk)
    kc = k.astype(F32).reshape(b, h, n, chunk, dk)
    vc = v.astype(F32).reshape(b, h, n, chunk, dv)
    cum = jnp.cumsum(log_a.astype(F32).reshape(b, h, n, chunk, da), axis=3)
    total = cum[:, :, :, -1:, :]
    causal = jnp.tril(jnp.ones((chunk, chunk), dtype=bool))[:, :, None]
    diff = cum[:, :, :, :, None, :] - cum[:, :, :, None, :, :]
    decay = jnp.where(causal, jnp.exp(jnp.where(causal, diff, 0.0)), 0.0)
    if da == 1:
        scores = jnp.einsum('bhnid,bhnjd->bhnij', qc, kc) * decay[..., 0]
    else:
        scores = jnp.einsum('bhnid,bhnjd,bhnijd->bhnij', qc, kc, decay)
    intra = jnp.einsum('bhnij,bhnjv->bhniv', scores, vc)
    kv = jnp.einsum('bhnjd,bhnjv->bhndv', kc * jnp.exp(total - cum), vc)
    a_chunk = jnp.exp(total[:, :, :, 0, :])

    def step(s, inp):
        kv_n, a_n = inp
        return a_n[..., None] * s + kv_n, s

    s_last, s_prev = lax.scan(step, s0.astype(F32), (jnp.moveaxis(kv, 2, 0), jnp.moveaxis(a_chunk, 2, 0)))
    s_prev = jnp.moveaxis(s_prev, 0, 2)
    inter = jnp.einsum('bhnid,bhndv->bhniv', qc * jnp.exp(cum), s_prev)
    o = (intra + inter).reshape(b, h, l, dv)
    return o.astype(v.dtype), s_last


def bidir(q, k_f, k_b, v, la_f, la_b, s0_f, s0_b, chunk):
    o_f, s_f = chunked_recurrence(q, k_f, v, la_f, s0_f, chunk)
    flip = lambda t: jnp.flip(t, axis=2)
    o_b, s_b = chunked_recurrence(flip(q), flip(k_b), flip(v), flip(la_b), s0_b, chunk)
    return o_f + flip(o_b), s_f, s_b


def centred_dwconv(x, w, bias):
    ch, width = w.shape
    out = lax.conv_general_dilated(x, jnp.transpose(w)[:, None, :], (1,), [(width // 2, width // 2)],
                                   dimension_numbers=('NWC', 'WIO', 'NWC'), feature_group_count=ch)
    return out + bias


def retention_branch(q, k, v, g, decay_logit, gn_g, s0, rope):
    b, l, _ = q.shape
    q = q.reshape(b, l, H_R, DK_R)
    k = k.reshape(b, l, H_R, DK_R)
    if rope is not None:
        q = apply_rope(q, rope[0], rope[1])
        k = apply_rope(k, rope[0], rope[1])
    k = k * (DK_R ** -0.5)
    v = v.reshape(b, l, H_R, DV_R)
    q, k, v = (t.transpose(0, 2, 1, 3) for t in (q, k, v))
    log_gamma = jax.nn.log_sigmoid(decay_logit.astype(F32))
    la = jnp.broadcast_to(log_gamma[:, None, :, None, None], (2, b, H_R, l, 1))
    o, s_f, s_b = bidir(q, k, k, v, la[0], la[1], s0[:, 0], s0[:, 1], CHUNK_SCALAR)
    o = o.transpose(0, 2, 1, 3).reshape(b, l, BR_W)
    o = head_norm(o, gn_g, H_R, True) * jax.nn.silu(g)
    return o, jnp.stack([s_f, s_b], axis=1)


def ssd_branch(z, xbc, dt_raw, conv_w, conv_b, a_log, dt_bias, d_skip, norm_g, s0):
    b, l, _ = z.shape
    xbc = jax.nn.silu(centred_dwconv(xbc, conv_w, conv_b))
    xs, bm, cm = jnp.split(xbc, [BR_W, BR_W + G_M * N_M], axis=-1)
    xh = xs.reshape(b, l, H_M, P_M).transpose(0, 2, 1, 3)
    rep = H_M // G_M
    to_heads = lambda t: jnp.repeat(t.reshape(b, l, G_M, N_M), rep, axis=2).transpose(0, 2, 1, 3)
    bh, ch = to_heads(bm), to_heads(cm)
    dt = jax.nn.softplus(dt_raw.astype(F32).reshape(b, l, 2, H_M) + dt_bias.astype(F32))
    dt = dt.transpose(2, 0, 3, 1)
    log_a = dt * (-jnp.exp(a_log.astype(F32)))[:, None, :, None]
    y, s_f, s_b = bidir(ch, bh * dt[0][..., None], bh * dt[1][..., None], xh,
                        log_a[0][..., None], log_a[1][..., None], s0[:, 0], s0[:, 1], CHUNK_SCALAR)
    y = y + d_skip[None, :, None, None] * xh
    y = y.transpose(0, 2, 1, 3).reshape(b, l, BR_W)
    return rms_norm(y * jax.nn.silu(z), norm_g), jnp.stack([s_f, s_b], axis=1)


def hgrn2_branch(q, f_raw, i_in, g, lb, norm_g, s0):
    b, l, _ = q.shape
    heads = lambda t: t.reshape(b, l, H_C, t.shape[-1] // H_C).transpose(0, 2, 1, 3)
    fr = f_raw.astype(F32).reshape(b, l, 2, BR_W)
    lb = lb.astype(F32)
    log_f = jnp.logaddexp(jnp.log(lb), jnp.log1p(-lb) + jax.nn.log_sigmoid(fr))
    key = (1.0 - lb) * jax.nn.sigmoid(-fr)
    o, s_f, s_b = bidir(heads(q), heads(key[:, :, 0]), heads(key[:, :, 1]), heads(i_in),
                        heads(log_f[:, :, 0]), heads(log_f[:, :, 1]), s0[:, 0], s0[:, 1], CHUNK_VECTOR)
    o = o.transpose(0, 2, 1, 3).reshape(b, l, BR_W)
    o = head_norm(o, norm_g, H_C, False) * jax.nn.silu(g)
    return o, jnp.stack([s_f, s_b], axis=1)


def token_mixer(h, p, s_ret, s_ssm, s_hg, rope):
    b, l, _ = h.shape
    cuts = [int(c) for c in np.cumsum(IN_SPLITS)[:-1]]
    (rq, rk, rv, rg, sz, sxbc, sdt, hq, hf, hi, hg, gl) = jnp.split(h @ p['w_in'], cuts, axis=-1)
    o_ret, st_ret = retention_branch(rq, rk, rv, rg, p['ret_decay_logit'], p['ret_gn'], s_ret, rope)
    o_ssm, st_ssm = ssd_branch(sz, sxbc, sdt, p['ssm_conv_w'], p['ssm_conv_b'], p['ssm_a_log'],
                               p['ssm_dt_bias'], p['ssm_d'], p['ssm_norm'], s_ssm)
    o_hg, st_hg = hgrn2_branch(hq, hf, hi, hg, p['hgrn_lb'], p['hgrn_norm'], s_hg)
    branches = jnp.stack([o_ret, o_ssm, o_hg], axis=2)
    y = jnp.einsum('blkc,kcd->blkd', branches, p['w_branch'])
    gates = jax.nn.sigmoid(gl.reshape(b, l, N_BRANCH, D_MODEL))
    merged = jnp.sum(gates * y, axis=2)
    return merged @ p['w_out'], (st_ret, st_ssm, st_hg)


def swiglu(x, wg, wu, wd):
    return (jax.nn.silu(x @ wg) * (x @ wu)) @ wd


def routed_experts(x, idx, w, w_gate, w_up, w_down):
    t, d = x.shape
    p = t * TOP_K
    e_flat = idx.reshape(-1)
    tok_flat = jnp.repeat(jnp.arange(t, dtype=jnp.int32), TOP_K)
    w_flat = w.reshape(-1)
    order = jnp.argsort(e_flat)
    e_s, tok_s, w_s = e_flat[order], tok_flat[order], w_flat[order]
    counts = jnp.bincount(e_flat, length=N_EXPERTS)
    starts = jnp.cumsum(counts) - counts
    padded = (counts + MOE_BLOCK - 1) // MOE_BLOCK * MOE_BLOCK
    pends = jnp.cumsum(padded)
    pstarts = pends - padded
    dest = pstarts[e_s] + (jnp.arange(p) - starts[e_s])
    n_blocks = -(-p // MOE_BLOCK) + N_EXPERTS
    cap = n_blocks * MOE_BLOCK
    buf_tok = jnp.full((cap,), t, dtype=jnp.int32).at[dest].set(tok_s)
    buf_w = jnp.zeros((cap,), dtype=w.dtype).at[dest].set(w_s)
    block_expert = jnp.minimum(jnp.searchsorted(pends, jnp.arange(n_blocks) * MOE_BLOCK, side='right'),
                               N_EXPERTS - 1)
    x_pad = jnp.concatenate([x, jnp.zeros((1, d), x.dtype)], axis=0)

    def run_block(args):
        tok, e = args
        xb = x_pad[tok]
        return (jax.nn.silu(xb @ w_gate[e]) * (xb @ w_up[e])) @ w_down[e]

    yb = lax.map(run_block, (buf_tok.reshape(n_blocks, MOE_BLOCK), block_expert)).reshape(cap, d)
    yb = yb * buf_w[:, None].astype(yb.dtype)
    return jax.ops.segment_sum(yb, buf_tok, num_segments=t + 1)[:t]


def moe_ffn(h, p):
    b, l, d = h.shape
    x = h.reshape(b * l, d)
    t = x.shape[0]
    scores = jax.nn.sigmoid((x @ p['router_w']).astype(F32))
    biased = scores + p['router_b'].astype(F32)
    group_score = jnp.sum(lax.top_k(biased.reshape(t, N_GROUPS, N_EXPERTS // N_GROUPS), 2)[0], axis=-1)
    _, top_groups = lax.top_k(group_score, TOPK_GROUPS)
    group_mask = jnp.sum(jax.nn.one_hot(top_groups, N_GROUPS, dtype=F32), axis=1) > 0
    expert_mask = jnp.repeat(group_mask, N_EXPERTS // N_GROUPS, axis=1)
    _, idx = lax.top_k(jnp.where(expert_mask, biased, -jnp.inf), TOP_K)
    w = jnp.take_along_axis(scores, idx, axis=1)
    w = ROUTED_SCALE * w / jnp.sum(w, axis=-1, keepdims=True)
    routed = routed_experts(x, idx, w, p['exp_w_gate'], p['exp_w_up'], p['exp_w_down'])
    shared = swiglu(x, p['sh_w_gate'], p['sh_w_up'], p['sh_w_down'])
    return (routed + shared).reshape(b, l, d)


def trunk_layer(x, mod, p, s_ret, s_ssm, s_hg, rope):
    shift1, scale1, gate1, shift2, scale2, gate2 = (mod[:, None, j] for j in range(6))
    h = rms_norm(x, p['norm1']) * (1 + scale1) + shift1
    mix, states = token_mixer(h, p, s_ret, s_ssm, s_hg, rope)
    x = x + gate1 * mix
    h = rms_norm(x, p['norm2']) * (1 + scale2) + shift2
    x = x + gate2 * moe_ffn(h, p)
    return x, states


def setup_inputs(seed: int = 0) -> dict:
    key = jax.random.key(seed)
    ks = iter(jax.random.split(key, 48))
    nrm = lambda shape, scale: scale * jax.random.normal(next(ks), shape, F32)
    x_prompt = nrm((BATCH, SEQ, D_MODEL), 1.0)
    x_sample = nrm((DEC_BATCH, DEC_SEQ, D_MODEL), 1.0)
    state_ret = nrm((DEC_BATCH, DEPTH, 2, H_R, DK_R, DV_R), 1.0)
    state_ssm = nrm((DEC_BATCH, DEPTH, 2, H_M, N_M, P_M), 0.5)
    state_hgrn = nrm((DEC_BATCH, DEPTH, 2, H_C, E_C, DV_C), 0.5)
    c = nrm((DEC_BATCH, D_MODEL), 1.0)
    c_ctx = nrm((D_MODEL,), 1.0)
    ada_w = nrm((DEPTH, D_MODEL, 6 * D_MODEL), 0.5 * D_MODEL ** -0.5)
    ada_b = nrm((DEPTH, 6 * D_MODEL), 0.02)
    norm1 = 1.0 + nrm((DEPTH, D_MODEL), 0.02)
    norm2 = 1.0 + nrm((DEPTH, D_MODEL), 0.02)
    w_in = nrm((DEPTH, D_MODEL, N_IN), D_MODEL ** -0.5)
    hidx = jnp.arange(H_R, dtype=F32)
    ret_decay_logit = jnp.log(2.0 ** (5.0 + hidx) - 1.0)[None, None, :] + nrm((DEPTH, 2, H_R), 0.1)
    ret_gn = 1.0 + nrm((DEPTH, BR_W), 0.02)
    ssm_conv_w = nrm((DEPTH, CONV_CH, D_CONV), D_CONV ** -0.5)
    ssm_conv_b = nrm((DEPTH, CONV_CH), 0.02)
    ssm_a_log = jnp.log(jax.random.uniform(next(ks), (DEPTH, 2, H_M), F32, 1.0, 16.0))
    dt0 = jnp.exp(jax.random.uniform(next(ks), (DEPTH, 2, H_M), F32, math.log(1e-3), math.log(1e-1)))
    ssm_dt_bias = dt0 + jnp.log(-jnp.expm1(-dt0))
    ssm_d = 1.0 + nrm((DEPTH, H_M), 0.1)
    ssm_norm = 1.0 + nrm((DEPTH, BR_W), 0.02)
    hgrn_lb_logits = nrm((2, DEPTH, BR_W), 0.5)
    hgrn_norm = 1.0 + nrm((DEPTH, BR_W), 0.02)
    w_branch = nrm((DEPTH, N_BRANCH, BR_W, D_MODEL), BR_W ** -0.5)
    w_out = nrm((DEPTH, D_MODEL, D_MODEL), D_MODEL ** -0.5)
    router_w = nrm((DEPTH, D_MODEL, N_EXPERTS), D_MODEL ** -0.5)
    router_b = nrm((DEPTH, N_EXPERTS), 0.01)
    exp_w_gate = nrm((DEPTH, N_EXPERTS, D_MODEL, D_EXPERT), D_MODEL ** -0.5)
    exp_w_up = nrm((DEPTH, N_EXPERTS, D_MODEL, D_EXPERT), D_MODEL ** -0.5)
    exp_w_down = nrm((DEPTH, N_EXPERTS, D_EXPERT, D_MODEL), D_EXPERT ** -0.5)
    sh_w_gate = nrm((DEPTH, D_MODEL, D_SHARED), D_MODEL ** -0.5)
    sh_w_up = nrm((DEPTH, D_MODEL, D_SHARED), D_MODEL ** -0.5)
    sh_w_down = nrm((DEPTH, D_SHARED, D_MODEL), D_SHARED ** -0.5)
    final_norm = 1.0 + nrm((D_MODEL,), 0.02)
    return {'x_prompt': x_prompt, 'x_sample': x_sample, 'state_ret': state_ret, 'state_ssm': state_ssm,
            'state_hgrn': state_hgrn, 'c': c, 'c_ctx': c_ctx, 'ada_w': ada_w, 'ada_b': ada_b,
            'norm1': norm1, 'norm2': norm2, 'w_in': w_in, 'ret_decay_logit': ret_decay_logit, 'ret_gn': ret_gn,
            'ssm_conv_w': ssm_conv_w, 'ssm_conv_b': ssm_conv_b, 'ssm_a_log': ssm_a_log,
            'ssm_dt_bias': ssm_dt_bias, 'ssm_d': ssm_d, 'ssm_norm': ssm_norm,
            'hgrn_lb_logits': hgrn_lb_logits, 'hgrn_norm': hgrn_norm, 'w_branch': w_branch, 'w_out': w_out,
            'router_w': router_w, 'router_b': router_b, 'exp_w_gate': exp_w_gate, 'exp_w_up': exp_w_up,
            'exp_w_down': exp_w_down, 'sh_w_gate': sh_w_gate, 'sh_w_up': sh_w_up, 'sh_w_down': sh_w_down,
            'final_norm': final_norm}


def reference(x_prompt, x_sample, state_ret, state_ssm, state_hgrn, c, c_ctx, ada_w, ada_b, norm1, norm2,
              w_in, ret_decay_logit, ret_gn, ssm_conv_w, ssm_conv_b, ssm_a_log, ssm_dt_bias, ssm_d, ssm_norm,
              hgrn_lb_logits, hgrn_norm, w_branch, w_out, router_w, router_b, exp_w_gate, exp_w_up,
              exp_w_down, sh_w_gate, sh_w_up, sh_w_down, final_norm):
    lb_all = jnp.cumsum(jax.nn.softmax(hgrn_lb_logits.astype(F32), axis=1), axis=1)
    lb_all = lb_all - lb_all[:, :1]
    rope = grid_rope(x_sample.shape[1], x_sample.dtype)
    bp, bd = x_prompt.shape[0], x_sample.shape[0]
    xc, xs = x_prompt, x_sample
    new_ret, new_ssm, new_hg = [], [], []
    for i in range(DEPTH):
        p = {'norm1': norm1[i], 'norm2': norm2[i], 'w_in': w_in[i], 'ret_decay_logit': ret_decay_logit[i],
             'ret_gn': ret_gn[i], 'ssm_conv_w': ssm_conv_w[i], 'ssm_conv_b': ssm_conv_b[i],
             'ssm_a_log': ssm_a_log[i], 'ssm_dt_bias': ssm_dt_bias[i], 'ssm_d': ssm_d[i],
             'ssm_norm': ssm_norm[i], 'hgrn_lb': lb_all[:, i], 'hgrn_norm': hgrn_norm[i],
             'w_branch': w_branch[i], 'w_out': w_out[i], 'router_w': router_w[i], 'router_b': router_b[i],
             'exp_w_gate': exp_w_gate[i], 'exp_w_up': exp_w_up[i], 'exp_w_down': exp_w_down[i],
             'sh_w_gate': sh_w_gate[i], 'sh_w_up': sh_w_up[i], 'sh_w_down': sh_w_down[i]}
        mod_ctx = (jax.nn.silu(c_ctx)[None, :] @ ada_w[i] + ada_b[i]).reshape(1, 6, D_MODEL)
        mod_lat = (jax.nn.silu(c) @ ada_w[i] + ada_b[i]).reshape(bd, 6, D_MODEL)
        z_ret = jnp.zeros((bp, 2, H_R, DK_R, DV_R), F32)
        z_ssm = jnp.zeros((bp, 2, H_M, N_M, P_M), F32)
        z_hg = jnp.zeros((bp, 2, H_C, E_C, DV_C), F32)
        xc, (sr, ss, sh) = trunk_layer(xc, mod_ctx, p, z_ret, z_ssm, z_hg, None)
        new_ret.append(sr)
        new_ssm.append(ss)
        new_hg.append(sh)
        xs, _ = trunk_layer(xs, mod_lat, p, state_ret[:, i], state_ssm[:, i], state_hgrn[:, i], rope)
    y_prompt = rms_norm(xc, final_norm)
    y_sample = rms_norm(xs, final_norm)
    new_state_ret = jnp.stack(new_ret, axis=1).astype(x_prompt.dtype)
    new_state_ssm = jnp.stack(new_ssm, axis=1).astype(x_prompt.dtype)
    new_state_hgrn = jnp.stack(new_hg, axis=1).astype(x_prompt.dtype)
    return (y_prompt, y_sample, new_state_ret, new_state_ssm, new_state_hgrn)
```

```python
import functools

import numpy as np
import jax
import jax.numpy as jnp
from jax import lax
from jax.experimental import pallas as pl
from jax.experimental.pallas import tpu as pltpu

F32 = jnp.float32
BF16 = jnp.bfloat16
HIGHEST = lax.Precision.HIGHEST

D_MODEL = 1024
DEPTH = 2
GRID_W = 64
H_R, DK_R, DV_R = 4, 64, 128
H_M, P_M, N_M, G_M, D_CONV = 8, 64, 64, 2, 5
H_C, E_C, DV_C = 4, 128, 128
BR_W = 512
N_BRANCH = 3
RET_QK_W = H_R * DK_R
CONV_CH = BR_W + 2 * G_M * N_M
SSM_DT_W = 2 * H_M
IN_SPLITS = (RET_QK_W, RET_QK_W, BR_W, BR_W, BR_W, CONV_CH, SSM_DT_W, BR_W, 2 * BR_W, BR_W, BR_W,
             N_BRANCH * D_MODEL)
N_EXPERTS, TOP_K, N_GROUPS, TOPK_GROUPS = 64, 6, 8, 4
D_EXPERT = 256
D_SHARED = 256
ROUTED_SCALE = 2.5
EPS = 1e-6
ROPE_BASE = 10000.0

LANES = 128
SUBLANES = 8
SLAB = D_MODEL // LANES
CHUNK = 128
ROW_TILE = 256
MOE_ROWS = 256
VMEM_LIMIT = 56 * 1024 * 1024
NEG_BIG = -1e30

_CP = functools.partial(pltpu.CompilerParams, vmem_limit_bytes=VMEM_LIMIT)


def _sigmoid(x):
    return 1.0 / (1.0 + jnp.exp(-x))


def _silu(x):
    return x * _sigmoid(x)


def _softplus(x):
    return jnp.maximum(x, 0.0) + jnp.log1p(jnp.exp(-jnp.abs(x)))


def _log_sigmoid(x):
    return jnp.minimum(x, 0.0) - jnp.log1p(jnp.exp(-jnp.abs(x)))


def _dot(a, b):
    return jnp.dot(a, b, preferred_element_type=F32)


def _dot_nt(a, b):
    return lax.dot_general(a, b, (((1,), (1,)), ((), ())), preferred_element_type=F32)


def _dot_tn(a, b):
    return lax.dot_general(a, b, (((0,), (0,)), ((), ())), preferred_element_type=F32)


def _tri_dot(tri, x):
    hi = x.astype(BF16)
    r1 = x - hi.astype(F32)
    mid = r1.astype(BF16)
    lo = (r1 - mid.astype(F32)).astype(BF16)
    return _dot(tri, hi) + _dot(tri, mid) + _dot(tri, lo)


def _iota2(shape, dim):
    return lax.broadcasted_iota(jnp.int32, shape, dim)


def _rows(c):
    return pl.ds(pl.multiple_of(c * CHUNK, CHUNK), CHUNK)


def _const_spec(shape):
    nd = len(shape)
    return pl.BlockSpec(shape, lambda *_: (0,) * nd)


def _ada_kernel(c_ref, w_ref, b_ref, o_ref):
    a = _silu(c_ref[...])
    o_ref[0] = jnp.dot(a, w_ref[0], preferred_element_type=F32, precision=HIGHEST) + b_ref[0]


def _ada(c_all, ada_w, ada_b):
    depth, d, n = ada_w.shape
    tn = 1536
    return pl.pallas_call(
        _ada_kernel,
        grid=(depth, n // tn),
        in_specs=[pl.BlockSpec((SUBLANES, d), lambda i, j: (0, 0)),
                  pl.BlockSpec((1, d, tn), lambda i, j: (i, 0, j)),
                  pl.BlockSpec((1, 1, tn), lambda i, j: (i, 0, j))],
        out_specs=pl.BlockSpec((1, SUBLANES, tn), lambda i, j: (i, 0, j)),
        out_shape=jax.ShapeDtypeStruct((depth, SUBLANES, n), F32),
        compiler_params=_CP(dimension_semantics=("arbitrary", "arbitrary")),
        name="ada_mod",
    )(c_all, ada_w, ada_b.reshape(depth, 1, n))


def _mod_index(n_ctx_tiles, tiles_per_seq):
    def index(r):
        return (jnp.where(r < n_ctx_tiles, 0, 1 + (r - n_ctx_tiles) // tiles_per_seq), 0, 0)
    return index


def _modulated_norm(x, g, shift, scale):
    y = x * lax.rsqrt(jnp.mean(x * x, axis=-1, keepdims=True) + EPS) * g
    return y * (1.0 + scale) + shift


def _inproj_kernel(x_ref, mod_ref, n_ref, w0, w1, w2, w3, w4, o0, o1, o2, o3, o4):
    h = _modulated_norm(x_ref[...], n_ref[...], mod_ref[0, 0:1, :], mod_ref[0, 1:2, :]).astype(BF16)
    for w, o in ((w0, o0), (w1, o1), (w2, o2), (w3, o3), (w4, o4)):
        o[...] = _dot(h, w[...]).astype(o.dtype)


def _inproj(x_all, mod, norm_g, weights, out_dtypes, mod_index):
    t, d = x_all.shape
    w_specs = [pl.BlockSpec(w.shape, lambda r: (0, 0), pipeline_mode=pl.Buffered(1)) for w in weights]
    return pl.pallas_call(
        _inproj_kernel,
        grid=(t // ROW_TILE,),
        in_specs=[pl.BlockSpec((ROW_TILE, d), lambda r: (r, 0)),
                  pl.BlockSpec((1, 6, d), mod_index),
                  pl.BlockSpec((1, d), lambda r: (0, 0))] + w_specs,
        out_specs=[pl.BlockSpec((ROW_TILE, w.shape[1]), lambda r: (r, 0)) for w in weights],
        out_shape=[jax.ShapeDtypeStruct((t, w.shape[1]), dt) for w, dt in zip(weights, out_dtypes)],
        compiler_params=_CP(dimension_semantics=("arbitrary",)),
        name="norm_inproj",
    )(x_all, mod, norm_g.reshape(1, d), *weights)


def _ret_kernel(use_rope, n_chunks, in_ref, cos_ref, sin_ref, lgl_ref, lgv_ref, gn_ref, s0_ref,
                o_ref, sfin_ref, qk_s, oacc, st, dm_s):
    qw = 2 * LANES
    ii = _iota2((CHUNK, CHUNK), 0)
    jj = _iota2((CHUNK, CHUNK), 1)
    dist = jnp.abs(ii - jj).astype(F32)
    for h in range(H_R):
        dm_s[h] = (jnp.where(ii >= jj, jnp.exp(dist * lgv_ref[0, h:h + 1, :]), 0.0)
                   + jnp.where(jj >= ii, jnp.exp(dist * lgv_ref[1, h:h + 1, :]), 0.0))
    st[...] = s0_ref[0]

    lane_head = (_iota2((1, qw), 1) % LANES) // (DK_R // 2)
    rr = _iota2((CHUNK, qw), 0).astype(F32)
    lg_f = lgl_ref[0:1, :]
    lg_b = lgl_ref[1:2, :]

    def rope(x, cs, sn):
        x1, x2 = x[:, :LANES], x[:, LANES:]
        return jnp.concatenate([x1 * cs - x2 * sn, x1 * sn + x2 * cs], axis=1)

    def fwd(c, carry):
        rows = _rows(c)
        q = in_ref[rows, 0:qw].astype(F32)
        k = in_ref[rows, qw:2 * qw].astype(F32)
        if use_rope:
            cs, sn = cos_ref[rows, :], sin_ref[rows, :]
            q, k = rope(q, cs, sn), rope(k, cs, sn)
        qk_s[rows, 0:qw] = q
        qk_s[rows, qw:2 * qw] = k
        kb = k.astype(BF16)
        q_dec = (q * jnp.exp((rr + 1.0) * lg_f)).astype(BF16)
        k_dec = k * jnp.exp((CHUNK - 1.0 - rr) * lg_f)
        for h in range(H_R):
            hs = slice(h * DV_R, (h + 1) * DV_R)
            mh = lane_head == h
            vh = in_ref[rows, 2 * qw + h * DV_R:2 * qw + (h + 1) * DV_R]
            s = _dot_nt(jnp.where(mh, q, 0.0).astype(BF16), kb)
            intra = _dot((s * dm_s[h]).astype(BF16), vh)
            sf = st[0, h]
            oacc[rows, hs] = intra + _dot(q_dec, sf.astype(BF16))
            st[0, h] = (sf * jnp.exp(CHUNK * lgv_ref[0, h:h + 1, :])
                        + _dot_tn(jnp.where(mh, k_dec, 0.0).astype(BF16), vh))
        return carry

    lax.fori_loop(0, n_chunks, fwd, 0)

    def bwd(t, carry):
        rows = _rows(n_chunks - 1 - t)
        q = qk_s[rows, 0:qw]
        k = qk_s[rows, qw:2 * qw]
        q_dec = (q * jnp.exp((CHUNK - rr) * lg_b)).astype(BF16)
        k_dec = k * jnp.exp(rr * lg_b)
        for h in range(H_R):
            hs = slice(h * DV_R, (h + 1) * DV_R)
            mh = lane_head == h
            vh = in_ref[rows, 2 * qw + h * DV_R:2 * qw + (h + 1) * DV_R]
            sb = st[1, h]
            oacc[rows, hs] = oacc[rows, hs] + _dot(q_dec, sb.astype(BF16))
            st[1, h] = (sb * jnp.exp(CHUNK * lgv_ref[1, h:h + 1, :])
                        + _dot_tn(jnp.where(mh, k_dec, 0.0).astype(BF16), vh))
        return carry

    lax.fori_loop(0, n_chunks, bwd, 0)
    sfin_ref[0] = st[...]

    def fin(c, carry):
        rows = _rows(c)
        for h in range(H_R):
            hs = slice(h * DV_R, (h + 1) * DV_R)
            o = oacc[rows, hs]
            oc = o - jnp.mean(o, axis=-1, keepdims=True)
            y = oc * lax.rsqrt(jnp.mean(oc * oc, axis=-1, keepdims=True) + EPS) * gn_ref[:, hs]
            g = in_ref[rows, 4 * qw + h * DV_R:4 * qw + (h + 1) * DV_R].astype(F32)
            o_ref[rows, hs] = (y * _silu(g)).astype(o_ref.dtype)
        return carry

    lax.fori_loop(0, n_chunks, fin, 0)


def _retention(ret_in, row0, b, l, cos4, sin4, lgl, lgv, gn, s0):
    use_rope = cos4 is not None
    if not use_rope:
        cos4 = jnp.zeros((SUBLANES, LANES), F32)
        sin4 = cos4
        trig_spec = pl.BlockSpec((SUBLANES, LANES), lambda i: (0, 0))
    else:
        trig_spec = pl.BlockSpec((l, LANES), lambda i: (0, 0))
    blk0 = row0 // l
    n_chunks = l // CHUNK
    sshape = (1, 2, H_R, 2 * LANES, DV_R)
    return pl.pallas_call(
        functools.partial(_ret_kernel, use_rope, n_chunks),
        grid=(b,),
        in_specs=[pl.BlockSpec((l, ret_in.shape[1]), lambda i: (blk0 + i, 0)),
                  trig_spec, trig_spec,
                  _const_spec(lgl.shape), _const_spec(lgv.shape), _const_spec(gn.shape),
                  pl.BlockSpec(sshape, lambda i: (i, 0, 0, 0, 0))],
        out_specs=[pl.BlockSpec((l, BR_W), lambda i: (i, 0)),
                   pl.BlockSpec(sshape, lambda i: (i, 0, 0, 0, 0))],
        out_shape=[jax.ShapeDtypeStruct((b * l, BR_W), BF16),
                   jax.ShapeDtypeStruct((b,) + sshape[1:], F32)],
        scratch_shapes=[pltpu.VMEM((l, 4 * LANES), F32),
                        pltpu.VMEM((l, BR_W), F32),
                        pltpu.VMEM(sshape[1:], F32),
                        pltpu.VMEM((H_R, CHUNK, CHUNK), F32)],
        compiler_params=_CP(dimension_semantics=("arbitrary",)),
        name="retention",
    )(ret_in, cos4, sin4, lgl, lgv, gn, s0)


def _ssd_kernel(n_chunks, in_ref, dt_ref, cw_ref, cb_ref, dtb_ref, na_ref, dsk_ref, ng_ref, s0_ref,
                o_ref, sfin_ref, pad_s, xc_s, y_s, dt_s, cumb_s, st):
    l = in_ref.shape[0]
    cw = 2 * BR_W
    halo = SUBLANES
    pad_s[0:halo, :] = jnp.zeros((halo, cw), F32)
    pad_s[l + halo:l + 2 * halo, :] = jnp.zeros((halo, cw), F32)

    def fill(c, carry):
        rows = _rows(c)
        dst = pl.ds(pl.multiple_of(c * CHUNK + halo, SUBLANES), CHUNK)
        pad_s[dst, :] = in_ref[rows, BR_W:BR_W + cw].astype(F32)
        return carry

    lax.fori_loop(0, n_chunks, fill, 0)
    st[...] = s0_ref[0]

    ii = _iota2((CHUNK, CHUNK), 0)
    jj = _iota2((CHUNK, CHUNK), 1)
    tril = jnp.where(ii >= jj, 1.0, 0.0).astype(BF16)
    triu = jnp.where(jj >= ii, 1.0, 0.0).astype(BF16)
    lane_lo = jj < P_M
    row_lo = ii < N_M
    blockdiag = lane_lo == row_lo
    half = D_CONV // 2

    def pair_cols(vals, h0, h1):
        return jnp.where(lane_lo, vals[:, h0:h0 + 1], vals[:, h1:h1 + 1])

    def fwd(c, carry):
        rows = _rows(c)
        win = pad_s[pl.ds(pl.multiple_of(c * CHUNK, CHUNK), CHUNK + 2 * halo), :]
        acc = win[halo - half:halo - half + CHUNK, :] * cw_ref[0:1, :] + cb_ref[...]
        for w in range(1, D_CONV):
            acc = acc + win[halo - half + w:halo - half + w + CHUNK, :] * cw_ref[w:w + 1, :]
        xc = _silu(acc)
        xc_s[rows, :] = xc.astype(xc_s.dtype)
        x = xc[:, 0:BR_W]
        dt = _softplus(dt_ref[rows, :] + dtb_ref[...])
        la = dt * na_ref[...]
        cumf = _tri_dot(tril, la)
        cumb = _tri_dot(triu, la)
        dt_s[rows, :] = dt
        cumb_s[rows, :] = cumb
        cumf_t, cumb_t, dt_t = cumf.T, cumb.T, dt.T
        totf = cumf[CHUNK - 1:CHUNK, :]
        etotf = jnp.exp(totf)
        kdec = dt * jnp.exp(totf - cumf)
        qdec = jnp.exp(cumf)
        for g in range(G_M):
            b2 = xc[:, BR_W + g * LANES:BR_W + (g + 1) * LANES]
            c2 = xc[:, BR_W + (G_M + g) * LANES:BR_W + (G_M + g + 1) * LANES]
            cbm = _dot_nt(jnp.where(lane_lo, c2, 0.0).astype(BF16), b2.astype(BF16))
            for pp in range(2):
                p = 2 * g + pp
                h0, h1 = 2 * p, 2 * p + 1
                ps = slice(p * LANES, (p + 1) * LANES)
                ms = []
                for h in (h0, h1):
                    hb = H_M + h
                    mf = jnp.where(ii >= jj, jnp.exp(cumf[:, h:h + 1] - cumf_t[h:h + 1, :]), 0.0) * dt_t[h:h + 1, :]
                    mb = (jnp.where(jj >= ii, jnp.exp(cumb[:, hb:hb + 1] - cumb_t[hb:hb + 1, :]), 0.0)
                          * dt_t[hb:hb + 1, :])
                    ms.append((cbm * (mf + mb)).astype(BF16))
                xp = x[:, ps]
                xbd = jnp.concatenate([jnp.where(lane_lo, xp, 0.0), jnp.where(lane_lo, 0.0, xp)], axis=0)
                intra = _dot(jnp.concatenate(ms, axis=1), xbd.astype(BF16))
                sf = st[0, p]
                inter = _dot((c2 * pair_cols(qdec, h0, h1)).astype(BF16), sf.astype(BF16))
                y_s[rows, ps] = intra + inter
                kv = _dot_tn((b2 * pair_cols(kdec, h0, h1)).astype(BF16), xp.astype(BF16))
                arow = jnp.where(row_lo, etotf[:, h0:h0 + 1], etotf[:, h1:h1 + 1])
                st[0, p] = sf * arow + jnp.where(blockdiag, kv, 0.0)
        return carry

    lax.fori_loop(0, n_chunks, fwd, 0)

    def bwd(t, carry):
        rows = _rows(n_chunks - 1 - t)
        xc = xc_s[rows, :].astype(F32)
        x = xc[:, 0:BR_W]
        dt = dt_s[rows, :]
        cumb = cumb_s[rows, :]
        totb = cumb[0:1, :]
        etotb = jnp.exp(totb)
        kdec = dt * jnp.exp(totb - cumb)
        qdec = jnp.exp(cumb)
        for g in range(G_M):
            b2 = xc[:, BR_W + g * LANES:BR_W + (g + 1) * LANES]
            c2 = xc[:, BR_W + (G_M + g) * LANES:BR_W + (G_M + g + 1) * LANES]
            for pp in range(2):
                p = 2 * g + pp
                h0, h1 = H_M + 2 * p, H_M + 2 * p + 1
                ps = slice(p * LANES, (p + 1) * LANES)
                xp = x[:, ps]
                sb = st[1, p]
                y_s[rows, ps] = y_s[rows, ps] + _dot((c2 * pair_cols(qdec, h0, h1)).astype(BF16), sb.astype(BF16))
                kv = _dot_tn((b2 * pair_cols(kdec, h0, h1)).astype(BF16), xp.astype(BF16))
                arow = jnp.where(row_lo, etotb[:, h0:h0 + 1], etotb[:, h1:h1 + 1])
                st[1, p] = sb * arow + jnp.where(blockdiag, kv, 0.0)
        return carry

    lax.fori_loop(0, n_chunks, bwd, 0)
    sfin_ref[0] = st[...]

    def fin(c, carry):
        rows = _rows(c)
        x = xc_s[rows, 0:BR_W].astype(F32)
        z = in_ref[rows, 0:BR_W].astype(F32)
        u = (y_s[rows, :] + x * dsk_ref[...]) * _silu(z)
        y = u * lax.rsqrt(jnp.mean(u * u, axis=-1, keepdims=True) + EPS) * ng_ref[...]
        o_ref[rows, :] = y.astype(o_ref.dtype)
        return carry

    lax.fori_loop(0, n_chunks, fin, 0)


def _ssd(ssm_in, prec, row0, b, l, cw, cb, dtb, na, dsk, ng, s0):
    blk0 = row0 // l
    n_chunks = l // CHUNK
    dt_col = (prec.shape[1] - LANES) // LANES
    sshape = (1, 2, H_M // 2, LANES, LANES)
    return pl.pallas_call(
        functools.partial(_ssd_kernel, n_chunks),
        grid=(b,),
        in_specs=[pl.BlockSpec((l, ssm_in.shape[1]), lambda i: (blk0 + i, 0)),
                  pl.BlockSpec((l, LANES), lambda i: (blk0 + i, dt_col)),
                  _const_spec(cw.shape), _const_spec(cb.shape), _const_spec(dtb.shape),
                  _const_spec(na.shape), _const_spec(dsk.shape), _const_spec(ng.shape),
                  pl.BlockSpec(sshape, lambda i: (i, 0, 0, 0, 0))],
        out_specs=[pl.BlockSpec((l, BR_W), lambda i: (i, 0)),
                   pl.BlockSpec(sshape, lambda i: (i, 0, 0, 0, 0))],
        out_shape=[jax.ShapeDtypeStruct((b * l, BR_W), BF16),
                   jax.ShapeDtypeStruct((b,) + sshape[1:], F32)],
        scratch_shapes=[pltpu.VMEM((l + 2 * SUBLANES, 2 * BR_W), F32),
                        pltpu.VMEM((l, 2 * BR_W), BF16),
                        pltpu.VMEM((l, BR_W), F32),
                        pltpu.VMEM((l, LANES), F32),
                        pltpu.VMEM((l, LANES), F32),
                        pltpu.VMEM(sshape[1:], F32)],
        compiler_params=_CP(dimension_semantics=("arbitrary",)),
        name="ssd",
    )(ssm_in, prec, cw, cb, dtb, na, dsk, ng, s0)


_HG_LEVELS = (8, 16, 32, 64)


def _bcast_group_row(x, j):
    x3 = x.reshape(CHUNK // SUBLANES, SUBLANES, LANES)
    r = jnp.broadcast_to(x3[:, j:j + 1, :], x3.shape)
    return r.reshape(CHUNK, LANES)


def _bcast_block_row(x, size, j):
    pieces = [jnp.broadcast_to(x[b * size + j:b * size + j + 1, :], (size, LANES)) for b in range(CHUNK // size)]
    return pieces[0] if len(pieces) == 1 else jnp.concatenate(pieces, axis=0)


def _hgrn_kernel(n_chunks, in_ref, f_ref, llb_ref, oml_ref, l1m_ref, ng_ref, s0_ref,
                 o_ref, sfin_ref, oacc, st):
    ii = _iota2((CHUNK, CHUNK), 0)
    jj = _iota2((CHUNK, CHUNK), 1)
    tril = jnp.where(ii >= jj, 1.0, 0.0).astype(BF16)
    triu = jnp.where(jj >= ii, 1.0, 0.0).astype(BF16)
    sub = ii % SUBLANES
    same_group = (ii // SUBLANES) == (jj // SUBLANES)
    sel_r = _iota2((SUBLANES * LANES, CHUNK), 0) // LANES
    sel_c = _iota2((SUBLANES * LANES, CHUNK), 1) % SUBLANES
    group_sel = jnp.where(sel_r == sel_c, 1.0, 0.0).astype(BF16)
    st[...] = s0_ref[0]

    def gates(rows, d):
        fr = f_ref[rows, d * BR_W:(d + 1) * BR_W]
        ds_ = slice(d * BR_W, (d + 1) * BR_W)
        a = llb_ref[:, ds_]
        bterm = l1m_ref[:, ds_] + _log_sigmoid(fr)
        logf = jnp.maximum(a, bterm) + jnp.log1p(jnp.exp(-jnp.abs(a - bterm)))
        key = oml_ref[:, ds_] * _sigmoid(-fr)
        return logf, key

    def fwd(c, carry):
        rows = _rows(c)
        logf_f, key_f = gates(rows, 0)
        logf_b, key_b = gates(rows, 1)
        cumf_all = _tri_dot(tril, logf_f)
        cumb_all = _tri_dot(triu, logf_b)
        for h in range(H_C):
            hs = slice(h * LANES, (h + 1) * LANES)
            q = in_ref[rows, hs].astype(F32)
            v = in_ref[rows, BR_W + h * LANES:BR_W + (h + 1) * LANES]
            cumf, cumb = cumf_all[:, hs], cumb_all[:, hs]
            kf, kb = key_f[:, hs], key_b[:, hs]
            sc = jnp.zeros((CHUNK, CHUNK), F32)
            for m in _HG_LEVELS:
                upper = (ii % (2 * m)) >= m
                same_block = (ii // (2 * m)) == (jj // (2 * m))
                ref_f = _bcast_block_row(cumf, 2 * m, m - 1)
                ref_b = _bcast_block_row(cumb, 2 * m, m)
                e_f = jnp.exp(jnp.where(upper, cumf - ref_f, ref_f - cumf))
                e_b = jnp.exp(jnp.where(upper, ref_b - cumb, cumb - ref_b))
                qcat = jnp.concatenate([jnp.where(upper, q * e_f, 0.0), jnp.where(upper, 0.0, q * e_b)], axis=1)
                kcat = jnp.concatenate([jnp.where(upper, 0.0, kf * e_f), jnp.where(upper, kb * e_b, 0.0)], axis=1)
                sc = sc + jnp.where(same_block, _dot_nt(qcat.astype(BF16), kcat.astype(BF16)), 0.0)
            prods = []
            for j in range(SUBLANES):
                arg_f = jnp.where(sub >= j, cumf - _bcast_group_row(cumf, j), NEG_BIG)
                arg_b = jnp.where(sub <= j, cumb - _bcast_group_row(cumb, j), NEG_BIG)
                pj = q * (_bcast_group_row(kf, j) * jnp.exp(arg_f) + _bcast_group_row(kb, j) * jnp.exp(arg_b))
                prods.append(pj.astype(BF16))
            diag = _dot(jnp.concatenate(prods, axis=1), group_sel)
            sc = sc + jnp.where(same_group, diag, 0.0)
            intra = _dot(sc.astype(BF16), v)
            stf = st[0, h]
            totf = cumf[CHUNK - 1:CHUNK, :]
            inter = _dot_nt((q * jnp.exp(cumf)).astype(BF16), stf.astype(BF16))
            oacc[rows, hs] = intra + inter
            st[0, h] = stf * jnp.exp(totf) + _dot_tn(v, (kf * jnp.exp(totf - cumf)).astype(BF16))
        return carry

    lax.fori_loop(0, n_chunks, fwd, 0)

    def bwd(t, carry):
        rows = _rows(n_chunks - 1 - t)
        logf_b, key_b = gates(rows, 1)
        cumb_all = _tri_dot(triu, logf_b)
        for h in range(H_C):
            hs = slice(h * LANES, (h + 1) * LANES)
            q = in_ref[rows, hs].astype(F32)
            v = in_ref[rows, BR_W + h * LANES:BR_W + (h + 1) * LANES]
            cumb, kb = cumb_all[:, hs], key_b[:, hs]
            stb = st[1, h]
            totb = cumb[0:1, :]
            oacc[rows, hs] = oacc[rows, hs] + _dot_nt((q * jnp.exp(cumb)).astype(BF16), stb.astype(BF16))
            st[1, h] = stb * jnp.exp(totb) + _dot_tn(v, (kb * jnp.exp(totb - cumb)).astype(BF16))
        return carry

    lax.fori_loop(0, n_chunks, bwd, 0)
    sfin_ref[0] = st[...]

    def fin(c, carry):
        rows = _rows(c)
        for h in range(H_C):
            hs = slice(h * LANES, (h + 1) * LANES)
            o = oacc[rows, hs]
            y = o * lax.rsqrt(jnp.mean(o * o, axis=-1, keepdims=True) + EPS) * ng_ref[:, hs]
            g = in_ref[rows, 2 * BR_W + h * LANES:2 * BR_W + (h + 1) * LANES].astype(F32)
            o_ref[rows, hs] = (y * _silu(g)).astype(o_ref.dtype)
        return carry

    lax.fori_loop(0, n_chunks, fin, 0)


def _hgrn(hg_in, prec, row0, b, l, llb, oml, l1m, ng, s0):
    blk0 = row0 // l
    n_chunks = l // CHUNK
    sshape = (1, 2, H_C, DV_C, E_C)
    return pl.pallas_call(
        functools.partial(_hgrn_kernel, n_chunks),
        grid=(b,),
        in_specs=[pl.BlockSpec((l, hg_in.shape[1]), lambda i: (blk0 + i, 0)),
                  pl.BlockSpec((l, 2 * BR_W), lambda i: (blk0 + i, 0)),
                  _const_spec(llb.shape), _const_spec(oml.shape), _const_spec(l1m.shape),
                  _const_spec(ng.shape),
                  pl.BlockSpec(sshape, lambda i: (i, 0, 0, 0, 0))],
        out_specs=[pl.BlockSpec((l, BR_W), lambda i: (i, 0)),
                   pl.BlockSpec(sshape, lambda i: (i, 0, 0, 0, 0))],
        out_shape=[jax.ShapeDtypeStruct((b * l, BR_W), BF16),
                   jax.ShapeDtypeStruct((b,) + sshape[1:], F32)],
        scratch_shapes=[pltpu.VMEM((l, BR_W), F32),
                        pltpu.VMEM(sshape[1:], F32)],
        compiler_params=_CP(dimension_semantics=("arbitrary",)),
        name="hgrn2",
    )(hg_in, prec, llb, oml, l1m, ng, s0)


def _merge_kernel(x_ref, mod_ref, n2_ref, o0, o1, o2, gl_ref, wb_ref, wo_ref, xo_ref, h2_ref, hs_ref):
    d = x_ref.shape[1]
    merged = jnp.zeros(x_ref.shape, F32)
    for k, o in enumerate((o0, o1, o2)):
        gate = _sigmoid(gl_ref[:, k * d:(k + 1) * d].astype(F32))
        merged = merged + gate * _dot(o[...], wb_ref[k])
    mix = _dot(merged.astype(BF16), wo_ref[...])
    xn = x_ref[...] + mod_ref[0, 2:3, :] * mix
    xo_ref[...] = xn
    h2 = _modulated_norm(xn, n2_ref[...], mod_ref[0, 3:4, :], mod_ref[0, 4:5, :])
    h2_ref[...] = h2.astype(h2_ref.dtype)
    rows = x_ref.shape[0]
    for s in range(SLAB):
        hs_ref[pl.ds(s, rows, stride=SLAB), :] = h2[:, s * LANES:(s + 1) * LANES]


def _merge(x_all, mod, norm_g, o_ret, o_ssm, o_hg, gl, wb, wo, mod_index):
    t, d = x_all.shape
    row = lambda r: (r, 0)
    return pl.pallas_call(
        _merge_kernel,
        grid=(t // ROW_TILE,),
        in_specs=[pl.BlockSpec((ROW_TILE, d), row),
                  pl.BlockSpec((1, 6, d), mod_index),
                  pl.BlockSpec((1, d), lambda r: (0, 0)),
                  pl.BlockSpec((ROW_TILE, BR_W), row),
                  pl.BlockSpec((ROW_TILE, BR_W), row),
                  pl.BlockSpec((ROW_TILE, BR_W), row),
                  pl.BlockSpec((ROW_TILE, N_BRANCH * d), row),
                  pl.BlockSpec(wb.shape, lambda r: (0, 0, 0), pipeline_mode=pl.Buffered(1)),
                  pl.BlockSpec(wo.shape, lambda r: (0, 0), pipeline_mode=pl.Buffered(1))],
        out_specs=[pl.BlockSpec((ROW_TILE, d), row),
                   pl.BlockSpec((ROW_TILE, d), row),
                   pl.BlockSpec((ROW_TILE * SLAB, LANES), row)],
        out_shape=[jax.ShapeDtypeStruct((t, d), F32),
                   jax.ShapeDtypeStruct((t, d), BF16),
                   jax.ShapeDtypeStruct((t * SLAB, LANES), F32)],
        compiler_params=_CP(dimension_semantics=("arbitrary",)),
        name="merge_outproj",
    )(x_all, mod, norm_g.reshape(1, d), o_ret, o_ssm, o_hg, gl, wb, wo)


def _router_kernel(x_ref, mod_ref, n2_ref, rwt_ref, rb_ref, idx_ref, w_ref):
    tm = x_ref.shape[0]
    h2 = _modulated_norm(x_ref[...], n2_ref[...], mod_ref[0, 3:4, :], mod_ref[0, 4:5, :])
    logits = lax.dot_general(rwt_ref[...], h2, (((1,), (1,)), ((), ())),
                             preferred_element_type=F32, precision=HIGHEST)
    scores = _sigmoid(logits)
    biased = scores + rb_ref[...]
    gsz = N_EXPERTS // N_GROUPS
    neg_inf = -jnp.inf

    b3 = biased.reshape(N_GROUPS, gsz, tm)
    e_in_g = lax.broadcasted_iota(jnp.int32, (N_GROUPS, gsz, tm), 1)
    m1 = jnp.max(b3, axis=1, keepdims=True)
    first = jnp.min(jnp.where(b3 == m1, e_in_g, gsz), axis=1, keepdims=True)
    m2 = jnp.max(jnp.where(e_in_g == first, neg_inf, b3), axis=1, keepdims=True)
    gscore = m1 + m2

    g_iota = lax.broadcasted_iota(jnp.int32, (N_GROUPS, 1, tm), 0)
    chosen = jnp.zeros((N_GROUPS, 1, tm), jnp.int32)
    for _ in range(TOPK_GROUPS):
        m = jnp.max(gscore, axis=0, keepdims=True)
        first = jnp.min(jnp.where(gscore == m, g_iota, N_GROUPS), axis=0, keepdims=True)
        hit = g_iota == first
        chosen = jnp.where(hit, 1, chosen)
        gscore = jnp.where(hit, neg_inf, gscore)
    emask = jnp.broadcast_to(chosen, (N_GROUPS, gsz, tm)).reshape(N_EXPERTS, tm)

    cand = jnp.where(emask > 0, biased, neg_inf)
    e_iota = _iota2((N_EXPERTS, tm), 0)
    idxs, ws = [], []
    for _ in range(TOP_K):
        m = jnp.max(cand, axis=0, keepdims=True)
        first = jnp.min(jnp.where(cand == m, e_iota, N_EXPERTS), axis=0, keepdims=True)
        hit = e_iota == first
        idxs.append(first)
        ws.append(jnp.sum(jnp.where(hit, scores, 0.0), axis=0, keepdims=True))
        cand = jnp.where(hit, neg_inf, cand)
    wsum = ws[0]
    for w in ws[1:]:
        wsum = wsum + w
    pad = SUBLANES - TOP_K
    idx_ref[...] = jnp.concatenate(idxs + [jnp.zeros((pad, tm), jnp.int32)], axis=0)
    w_ref[...] = jnp.concatenate([ROUTED_SCALE * w / wsum for w in ws] + [jnp.zeros((pad, tm), F32)], axis=0)


def _router(x_new, mod, norm_g, rwt, rb, mod_index):
    t, d = x_new.shape
    return pl.pallas_call(
        _router_kernel,
        grid=(t // ROW_TILE,),
        in_specs=[pl.BlockSpec((ROW_TILE, d), lambda r: (r, 0)),
                  pl.BlockSpec((1, 6, d), mod_index),
                  pl.BlockSpec((1, d), lambda r: (0, 0)),
                  pl.BlockSpec(rwt.shape, lambda r: (0, 0)),
                  pl.BlockSpec(rb.shape, lambda r: (0, 0))],
        out_specs=[pl.BlockSpec((SUBLANES, ROW_TILE), lambda r: (0, r)),
                   pl.BlockSpec((SUBLANES, ROW_TILE), lambda r: (0, r))],
        out_shape=[jax.ShapeDtypeStruct((SUBLANES, t), jnp.int32),
                   jax.ShapeDtypeStruct((SUBLANES, t), F32)],
        compiler_params=_CP(dimension_semantics=("arbitrary",)),
        name="router",
    )(x_new, mod, norm_g.reshape(1, d), rwt, rb)


def _expert_kernel(be_ref, nv_ref, tok_ref, tokn_ref, dst_ref, hs_hbm, wgu_ref, wd_ref, y_hbm,
                   xg, yb, gsem, ssem):
    b = pl.program_id(0)
    nv = nv_ref[0]
    slot = b % 2
    slab_rows = MOE_ROWS * SLAB

    def slab(r):
        return pl.ds(pl.multiple_of(r * SLAB, SLAB), SLAB)

    def gather(ids_ref, s):
        def issue(r, carry):
            pltpu.make_async_copy(hs_hbm.at[slab(ids_ref[0, 0, r])], xg.at[s, slab(r)], gsem.at[s]).start()
            return carry
        lax.fori_loop(0, MOE_ROWS, issue, 0, unroll=8)

    def scatter_copy_all(s):
        return pltpu.make_async_copy(yb.at[s], y_hbm.at[pl.ds(0, slab_rows)], ssem.at[s])

    @pl.when(b == 0)
    def _():
        gather(tok_ref, 0)
        spare0 = y_hbm.shape[0] - 2 * slab_rows
        yb[...] = jnp.zeros(yb.shape, yb.dtype)
        for s in range(2):
            init = pltpu.make_async_copy(yb.at[s], y_hbm.at[pl.ds(spare0 + s * slab_rows, slab_rows)], ssem.at[s])
            init.start()
            init.wait()

    @pl.when(b + 1 < nv)
    def _():
        gather(tokn_ref, 1 - slot)

    @pl.when(b < nv)
    def _():
        pltpu.make_async_copy(hs_hbm.at[pl.ds(0, slab_rows)], xg.at[slot], gsem.at[slot]).wait()
        lhs = jnp.concatenate([xg[slot, pl.ds(k, MOE_ROWS, stride=SLAB), :] for k in range(SLAB)], axis=1)
        gu = _dot(lhs.astype(BF16), wgu_ref[0])
        act = _silu(gu[:, :D_EXPERT]) * gu[:, D_EXPERT:]
        y = _dot(act.astype(BF16), wd_ref[0])

        @pl.when(b >= 2)
        def _():
            scatter_copy_all(slot).wait()

        for k in range(SLAB):
            yb[slot, pl.ds(k, MOE_ROWS, stride=SLAB), :] = y[:, k * LANES:(k + 1) * LANES]

        def issue(r, carry):
            pltpu.make_async_copy(yb.at[slot, slab(r)], y_hbm.at[slab(dst_ref[0, 0, r])], ssem.at[slot]).start()
            return carry
        lax.fori_loop(0, MOE_ROWS, issue, 0, unroll=8)

        @pl.when(b == nv - 1)
        def _():
            scatter_copy_all(slot).wait()

            @pl.when(nv >= 2)
            def _():
                scatter_copy_all(1 - slot).wait()


def _experts(hs, be, nv, tok, dst, wgu, wd, n_out_rows):
    n_blocks = tok.shape[0]
    ids = lambda b, be_r, nv_r: (b, 0, 0)
    ids_next = lambda b, be_r, nv_r: (jnp.minimum(b + 1, n_blocks - 1), 0, 0)
    smem_ids = functools.partial(pl.BlockSpec, (1, 1, MOE_ROWS), memory_space=pltpu.SMEM)
    grid_spec = pltpu.PrefetchScalarGridSpec(
        num_scalar_prefetch=2,
        grid=(n_blocks,),
        in_specs=[smem_ids(ids), smem_ids(ids_next), smem_ids(ids),
                  pl.BlockSpec(memory_space=pl.ANY),
                  pl.BlockSpec((1,) + wgu.shape[1:], lambda b, be_r, nv_r: (be_r[b], 0, 0)),
                  pl.BlockSpec((1,) + wd.shape[1:], lambda b, be_r, nv_r: (be_r[b], 0, 0))],
        out_specs=pl.BlockSpec(memory_space=pl.ANY),
        scratch_shapes=[pltpu.VMEM((2, MOE_ROWS * SLAB, LANES), F32),
                        pltpu.VMEM((2, MOE_ROWS * SLAB, LANES), F32),
                        pltpu.SemaphoreType.DMA((2,)),
                        pltpu.SemaphoreType.DMA((2,))])
    return pl.pallas_call(
        _expert_kernel,
        grid_spec=grid_spec,
        out_shape=jax.ShapeDtypeStruct((n_out_rows * SLAB, LANES), F32),
        compiler_params=_CP(dimension_semantics=("arbitrary",)),
        name="routed_experts",
    )(be, nv, tok, tok, dst, hs, wgu, wd)


def _moe_plan(idx8, t):
    p = t * TOP_K
    blk_rows = MOE_ROWS
    e_flat = idx8[:TOP_K].T.reshape(p)
    order = jnp.argsort(e_flat).astype(jnp.int32)
    e_sorted = e_flat[order]
    experts = jnp.arange(N_EXPERTS, dtype=jnp.int32)
    starts = jnp.searchsorted(e_sorted, experts, side="left").astype(jnp.int32)
    counts = jnp.searchsorted(e_sorted, experts, side="right").astype(jnp.int32) - starts
    padded = (counts + blk_rows - 1) // blk_rows * blk_rows
    pends = jnp.cumsum(padded)
    pstarts = pends - padded
    n_blocks = p // blk_rows + N_EXPERTS
    block_expert = jnp.minimum(
        jnp.searchsorted(pends, jnp.arange(n_blocks, dtype=jnp.int32) * blk_rows, side="right"),
        N_EXPERTS - 1).astype(jnp.int32)
    nv = (pends[-1] // blk_rows).astype(jnp.int32).reshape(1)
    pos = jnp.arange(n_blocks * blk_rows, dtype=jnp.int32)
    e_pos = jnp.repeat(block_expert, blk_rows)
    local = pos - pstarts[e_pos]
    valid = (local < counts[e_pos]) & (pos < pends[-1])
    flat = order[jnp.clip(starts[e_pos] + local, 0, p - 1)]
    tok = jnp.where(valid, flat // TOP_K, 0).astype(jnp.int32)
    dump = p + ((pos // blk_rows) % 2) * blk_rows + pos % blk_rows
    dst = jnp.where(valid, flat, dump).astype(jnp.int32)
    shape = (n_blocks, 1, blk_rows)
    return block_expert, nv, tok.reshape(shape), dst.reshape(shape), p + 2 * blk_rows


def _combine_kernel(final, x_ref, h2_ref, y_ref, w_ref, mod_ref, fn_ref, wsgu_ref, wsd_ref, o_ref):
    tm = x_ref.shape[0]
    gu = _dot(h2_ref[...], wsgu_ref[...])
    shared = _dot((_silu(gu[:, :D_SHARED]) * gu[:, D_SHARED:]).astype(BF16), wsd_ref[...])
    w = w_ref[...]
    stride = TOP_K * SLAB
    pieces = []
    for s in range(SLAB):
        acc = w[:, 0:1] * y_ref[pl.ds(s, tm, stride=stride), :]
        for k in range(1, TOP_K):
            acc = acc + w[:, k:k + 1] * y_ref[pl.ds(k * SLAB + s, tm, stride=stride), :]
        pieces.append(acc)
    routed = jnp.concatenate(pieces, axis=1)
    xo = x_ref[...] + mod_ref[0, 5:6, :] * (routed + shared)
    if final:
        xo = xo * lax.rsqrt(jnp.mean(xo * xo, axis=-1, keepdims=True) + EPS) * fn_ref[...]
    o_ref[...] = xo


def _combine(final, x_new, h2, y, w_tk, mod, final_norm, wsgu, wsd, mod_index):
    t, d = x_new.shape
    row = lambda r: (r, 0)
    return pl.pallas_call(
        functools.partial(_combine_kernel, final),
        grid=(t // ROW_TILE,),
        in_specs=[pl.BlockSpec((ROW_TILE, d), row),
                  pl.BlockSpec((ROW_TILE, d), row),
                  pl.BlockSpec((ROW_TILE * TOP_K * SLAB, LANES), row),
                  pl.BlockSpec((ROW_TILE, SUBLANES), row),
                  pl.BlockSpec((1, 6, d), mod_index),
                  pl.BlockSpec((1, d), lambda r: (0, 0)),
                  pl.BlockSpec(wsgu.shape, lambda r: (0, 0)),
                  pl.BlockSpec(wsd.shape, lambda r: (0, 0))],
        out_specs=pl.BlockSpec((ROW_TILE, d), row),
        out_shape=jax.ShapeDtypeStruct((t, d), F32),
        compiler_params=_CP(dimension_semantics=("arbitrary",)),
        name="combine_shared",
    )(x_new, h2, y, w_tk, mod, final_norm.reshape(1, d), wsgu, wsd)


def _qk_perm():
    half = DK_R // 2
    perm = np.zeros(RET_QK_W, np.int32)
    for part in range(2):
        for h in range(H_R):
            for j in range(half):
                perm[part * LANES + h * half + j] = h * DK_R + part * half + j
    return perm


def _ret_state_rows():
    half = DK_R // 2
    rows = np.zeros((H_R, DK_R), np.int32)
    for h in range(H_R):
        for j in range(DK_R):
            rows[h, j] = (j // half) * LANES + h * half + j % half
    return rows


def _xbc_channels():
    ch = list(range(BR_W))
    for base in (BR_W, BR_W + G_M * N_M):
        for g in range(G_M):
            grp = list(range(base + g * N_M, base + (g + 1) * N_M))
            ch += grp + grp
    return np.asarray(ch, np.int32)


def _pad_lanes(v):
    return jnp.zeros((1, LANES), F32).at[0, :v.shape[0]].set(v.astype(F32))


def _ret_state_pack(s):
    rows = _ret_state_rows()
    src = np.full((H_R, 2 * LANES), DK_R, np.int32)
    for h in range(H_R):
        src[h, rows[h]] = np.arange(DK_R)
    sz = jnp.concatenate([s, jnp.zeros(s.shape[:3] + (1, DV_R), s.dtype)], axis=3)
    return sz[:, :, np.arange(H_R)[:, None], src, :]


def _ret_state_unpack(s):
    return s[:, :, np.arange(H_R)[:, None], _ret_state_rows(), :]


def _ssm_state_pack(s):
    b = s.shape[0]
    sr = s.reshape(b, 2, H_M // 2, 2, N_M, 1, P_M)
    eye = jnp.eye(2, dtype=s.dtype).reshape(1, 1, 1, 2, 1, 2, 1)
    return (sr * eye).reshape(b, 2, H_M // 2, 2 * N_M, 2 * P_M)


def _ssm_state_unpack(s):
    b = s.shape[0]
    sr = s.reshape(b, 2, H_M // 2, 2, N_M, 2, P_M)
    return jnp.stack([sr[:, :, :, 0, :, 0, :], sr[:, :, :, 1, :, 1, :]], axis=3).reshape(b, 2, H_M, N_M, P_M)


def _grid_rope(l):
    rows = l // GRID_W
    row = jnp.repeat(jnp.arange(rows), GRID_W).astype(F32)
    col = (jnp.arange(rows * GRID_W) % GRID_W).astype(F32)
    n_freq = DK_R // 4
    freqs = ROPE_BASE ** (-jnp.arange(n_freq, dtype=F32) / n_freq)
    ang = jnp.concatenate([row[:, None] * freqs, col[:, None] * freqs], axis=-1)
    return jnp.tile(jnp.cos(ang), (1, H_R)), jnp.tile(jnp.sin(ang), (1, H_R))


def _layer_weights(i, w_in, ssm_conv_w, ssm_conv_b):
    cuts = np.cumsum(IN_SPLITS)[:-1]
    rq, rk, rv, rg, sz, sxbc, sdt, hq, hf, hi, hg, gl = jnp.split(w_in[i], cuts, axis=1)
    perm = _qk_perm()
    ch = _xbc_channels()
    w_ret = jnp.concatenate([rq[:, perm], rk[:, perm] * (DK_R ** -0.5), rv, rg], axis=1).astype(BF16)
    w_ssm = jnp.concatenate([sz, sxbc[:, ch]], axis=1).astype(BF16)
    w_hg = jnp.concatenate([hq, hi, hg], axis=1).astype(BF16)
    w_prec = jnp.concatenate([hf, sdt, jnp.zeros((w_in.shape[1], LANES - SSM_DT_W), F32)], axis=1).astype(BF16)
    conv_w = ssm_conv_w[i][ch].T
    conv_b = ssm_conv_b[i][ch][None, :]
    return (w_ret, w_ssm, w_hg, gl.astype(BF16), w_prec), conv_w, conv_b


def kernel(x_prompt, x_sample, state_ret, state_ssm, state_hgrn, c, c_ctx, ada_w, ada_b, norm1, norm2, w_in,
           ret_decay_logit, ret_gn, ssm_conv_w, ssm_conv_b, ssm_a_log, ssm_dt_bias, ssm_d, ssm_norm,
           hgrn_lb_logits, hgrn_norm, w_branch, w_out, router_w, router_b, exp_w_gate, exp_w_up,
           exp_w_down, sh_w_gate, sh_w_up, sh_w_down, final_norm):
    bc, lc, d = x_prompt.shape
    bl, ll, _ = x_sample.shape
    tc, tl = bc * lc, bl * ll
    t = tc + tl
    assert tc % ROW_TILE == 0 and ll % ROW_TILE == 0 and lc % CHUNK == 0 and ll % CHUNK == 0
    assert tl % lc == 0 and tc % ll == 0 and 1 + bl <= SUBLANES
    mod_index = _mod_index(tc // ROW_TILE, ll // ROW_TILE)

    x_all = jnp.concatenate([x_prompt.reshape(tc, d), x_sample.reshape(tl, d)], axis=0)
    c_all = jnp.concatenate([c_ctx[None, :], c, jnp.zeros((SUBLANES - 1 - bl, d), F32)], axis=0)
    mods = _ada(c_all, ada_w, ada_b)

    lb_all = jnp.cumsum(jax.nn.softmax(hgrn_lb_logits.astype(F32), axis=1), axis=1)
    lb_all = lb_all - lb_all[:, :1]
    cos4, sin4 = _grid_rope(ll)
    lane_head = (np.arange(2 * LANES) % LANES) // (DK_R // 2)

    new_ret, new_ssm, new_hg = [], [], []
    for i in range(DEPTH):
        mod = mods[i].reshape(SUBLANES, 6, d)
        weights, conv_w, conv_b = _layer_weights(i, w_in, ssm_conv_w, ssm_conv_b)
        ret_in, ssm_in, hg_in, gl, prec = _inproj(x_all, mod, norm1[i], weights, (BF16, BF16, BF16, BF16, F32),
                                                  mod_index)

        lg = jax.nn.log_sigmoid(ret_decay_logit[i].astype(F32))
        lgl = lg[:, lane_head]
        lgv = jnp.broadcast_to(lg[:, :, None], (2, H_R, LANES))
        gn = ret_gn[i][None, :]
        zero_ret = jnp.zeros((bc, 2, H_R, 2 * LANES, DV_R), F32)
        o_ret_c, s_ret = _retention(ret_in, 0, bc, lc, None, None, lgl, lgv, gn, zero_ret)
        o_ret_l, _ = _retention(ret_in, tc, bl, ll, cos4, sin4, lgl, lgv, gn, _ret_state_pack(state_ret[:, i]))
        new_ret.append(_ret_state_unpack(s_ret))

        dtb = _pad_lanes(ssm_dt_bias[i].reshape(-1))
        na = _pad_lanes(-jnp.exp(ssm_a_log[i].astype(F32)).reshape(-1))
        dsk = jnp.repeat(ssm_d[i], P_M)[None, :]
        ng = ssm_norm[i][None, :]
        zero_ssm = jnp.zeros((bc, 2, H_M // 2, LANES, LANES), F32)
        o_ssm_c, s_ssm = _ssd(ssm_in, prec, 0, bc, lc, conv_w, conv_b, dtb, na, dsk, ng, zero_ssm)
        o_ssm_l, _ = _ssd(ssm_in, prec, tc, bl, ll, conv_w, conv_b, dtb, na, dsk, ng,
                          _ssm_state_pack(state_ssm[:, i]))
        new_ssm.append(_ssm_state_unpack(s_ssm))

        lb = lb_all[:, i]
        llb = jnp.log(lb).reshape(1, 2 * BR_W)
        oml = (1.0 - lb).reshape(1, 2 * BR_W)
        l1m = jnp.log1p(-lb).reshape(1, 2 * BR_W)
        hn = hgrn_norm[i][None, :]
        zero_hg = jnp.zeros((bc, 2, H_C, DV_C, E_C), F32)
        o_hg_c, s_hg = _hgrn(hg_in, prec, 0, bc, lc, llb, oml, l1m, hn, zero_hg)
        o_hg_l, _ = _hgrn(hg_in, prec, tc, bl, ll, llb, oml, l1m, hn,
                          jnp.swapaxes(state_hgrn[:, i], -1, -2))
        new_hg.append(jnp.swapaxes(s_hg, -1, -2))

        o_ret = jnp.concatenate([o_ret_c, o_ret_l], axis=0)
        o_ssm = jnp.concatenate([o_ssm_c, o_ssm_l], axis=0)
        o_hg = jnp.concatenate([o_hg_c, o_hg_l], axis=0)
        x_new, h2, hs = _merge(x_all, mod, norm2[i], o_ret, o_ssm, o_hg, gl,
                               w_branch[i].astype(BF16), w_out[i].astype(BF16), mod_index)

        idx8, w8 = _router(x_new, mod, norm2[i], router_w[i].T, router_b[i][:, None], mod_index)
        be, nv, tok, dst, n_out_rows = _moe_plan(idx8, t)
        wgu = jnp.concatenate([exp_w_gate[i], exp_w_up[i]], axis=-1).astype(BF16)
        y = _experts(hs, be, nv, tok, dst, wgu, exp_w_down[i].astype(BF16), n_out_rows)
        wsgu = jnp.concatenate([sh_w_gate[i], sh_w_up[i]], axis=-1).astype(BF16)
        x_all = _combine(i == DEPTH - 1, x_new, h2, y, w8.T, mod, final_norm, wsgu,
                         sh_w_down[i].astype(BF16), mod_index)

    y_prompt = x_all[:tc].reshape(bc, lc, d)
    y_sample = x_all[tc:].reshape(bl, ll, d)
    return (y_prompt, y_sample, jnp.stack(new_ret, axis=1), jnp.stack(new_ssm, axis=1),
            jnp.stack(new_hg, axis=1))
```

```python
import functools

import numpy as np
import jax
import jax.numpy as jnp
from jax import lax
from jax.experimental import pallas as pl
from jax.experimental.pallas import tpu as pltpu

F32 = jnp.float32
BF16 = jnp.bfloat16
HIGHEST = lax.Precision.HIGHEST

D_MODEL = 1024
DEPTH = 2
GRID_W = 64
H_R, DK_R, DV_R = 4, 64, 128
H_M, P_M, N_M, G_M, D_CONV = 8, 64, 64, 2, 5
H_C, E_C, DV_C = 4, 128, 128
BR_W = 512
N_BRANCH = 3
RET_QK_W = H_R * DK_R
CONV_CH = BR_W + 2 * G_M * N_M
SSM_DT_W = 2 * H_M
IN_SPLITS = (RET_QK_W, RET_QK_W, BR_W, BR_W, BR_W, CONV_CH, SSM_DT_W, BR_W, 2 * BR_W, BR_W, BR_W,
             N_BRANCH * D_MODEL)
N_EXPERTS, TOP_K, N_GROUPS, TOPK_GROUPS = 64, 6, 8, 4
D_EXPERT = 256
D_SHARED = 256
ROUTED_SCALE = 2.5
EPS = 1e-6
ROPE_BASE = 10000.0

LANES = 128
SUBLANES = 8
SLAB = D_MODEL // LANES
CHUNK = 128
ROW_TILE = 256
MOE_ROWS = 256
VMEM_LIMIT = 56 * 1024 * 1024
NEG_BIG = -1e30

_CP = functools.partial(pltpu.CompilerParams, vmem_limit_bytes=VMEM_LIMIT)


def _sigmoid(x):
    return 1.0 / (1.0 + jnp.exp(-x))


def _silu(x):
    return x * _sigmoid(x)


def _softplus(x):
    return jnp.maximum(x, 0.0) + jnp.log1p(jnp.exp(-jnp.abs(x)))


def _log_sigmoid(x):
    return jnp.minimum(x, 0.0) - jnp.log1p(jnp.exp(-jnp.abs(x)))


def _dot(a, b):
    return jnp.dot(a, b, preferred_element_type=F32)


def _dot_nt(a, b):
    return lax.dot_general(a, b, (((1,), (1,)), ((), ())), preferred_element_type=F32)


def _dot_tn(a, b):
    return lax.dot_general(a, b, (((0,), (0,)), ((), ())), preferred_element_type=F32)


def _tri_dot(tri, x):
    hi = x.astype(BF16)
    r1 = x - hi.astype(F32)
    mid = r1.astype(BF16)
    lo = (r1 - mid.astype(F32)).astype(BF16)
    return _dot(tri, hi) + _dot(tri, mid) + _dot(tri, lo)


def _iota2(shape, dim):
    return lax.broadcasted_iota(jnp.int32, shape, dim)


def _rows(c):
    return pl.ds(pl.multiple_of(c * CHUNK, CHUNK), CHUNK)


def _const_spec(shape):
    nd = len(shape)
    return pl.BlockSpec(shape, lambda *_: (0,) * nd)


def _ada_kernel(c_ref, w_ref, b_ref, o_ref):
    a = _silu(c_ref[...])
    o_ref[0] = jnp.dot(a, w_ref[0], preferred_element_type=F32, precision=HIGHEST) + b_ref[0]


def _ada(c_all, ada_w, ada_b):
    depth, d, n = ada_w.shape
    tn = 1536
    return pl.pallas_call(
        _ada_kernel,
        grid=(depth, n // tn),
        in_specs=[pl.BlockSpec((SUBLANES, d), lambda i, j: (0, 0)),
                  pl.BlockSpec((1, d, tn), lambda i, j: (i, 0, j)),
                  pl.BlockSpec((1, 1, tn), lambda i, j: (i, 0, j))],
        out_specs=pl.BlockSpec((1, SUBLANES, tn), lambda i, j: (i, 0, j)),
        out_shape=jax.ShapeDtypeStruct((depth, SUBLANES, n), F32),
        compiler_params=_CP(dimension_semantics=("arbitrary", "arbitrary")),
        name="ada_mod",
    )(c_all, ada_w, ada_b.reshape(depth, 1, n))


def _mod_index(n_ctx_tiles, tiles_per_seq):
    def index(r):
        return (jnp.where(r < n_ctx_tiles, 0, 1 + (r - n_ctx_tiles) // tiles_per_seq), 0, 0)
    return index


def _modulated_norm(x, g, shift, scale):
    y = x * lax.rsqrt(jnp.mean(x * x, axis=-1, keepdims=True) + EPS) * g
    return y * (1.0 + scale) + shift


def _inproj_kernel(x_ref, mod_ref, n_ref, w0, w1, w2, w3, w4, o0, o1, o2, o3, o4):
    h = _modulated_norm(x_ref[...], n_ref[...], mod_ref[0, 0:1, :], mod_ref[0, 1:2, :]).astype(BF16)
    for w, o in ((w0, o0), (w1, o1), (w2, o2), (w3, o3), (w4, o4)):
        o[...] = _dot(h, w[...]).astype(o.dtype)


def _inproj(x_all, mod, norm_g, weights, out_dtypes, mod_index):
    t, d = x_all.shape
    w_specs = [pl.BlockSpec(w.shape, lambda r: (0, 0), pipeline_mode=pl.Buffered(1)) for w in weights]
    return pl.pallas_call(
        _inproj_kernel,
        grid=(t // ROW_TILE,),
        in_specs=[pl.BlockSpec((ROW_TILE, d), lambda r: (r, 0)),
                  pl.BlockSpec((1, 6, d), mod_index),
                  pl.BlockSpec((1, d), lambda r: (0, 0))] + w_specs,
        out_specs=[pl.BlockSpec((ROW_TILE, w.shape[1]), lambda r: (r, 0)) for w in weights],
        out_shape=[jax.ShapeDtypeStruct((t, w.shape[1]), dt) for w, dt in zip(weights, out_dtypes)],
        compiler_params=_CP(dimension_semantics=("arbitrary",)),
        name="norm_inproj",
    )(x_all, mod, norm_g.reshape(1, d), *weights)


def _ret_kernel(use_rope, n_chunks, in_ref, cos_ref, sin_ref, lgl_ref, lgv_ref, gn_ref, s0_ref,
                o_ref, sfin_ref, qk_s, oacc, st, dm_s):
    qw = 2 * LANES
    ii = _iota2((CHUNK, CHUNK), 0)
    jj = _iota2((CHUNK, CHUNK), 1)
    dist = jnp.abs(ii - jj).astype(F32)
    for h in range(H_R):
        dm_s[h] = (jnp.where(ii >= jj, jnp.exp(dist * lgv_ref[0, h:h + 1, :]), 0.0)
                   + jnp.where(jj >= ii, jnp.exp(dist * lgv_ref[1, h:h + 1, :]), 0.0))
    st[...] = s0_ref[0]

    lane_head = (_iota2((1, qw), 1) % LANES) // (DK_R // 2)
    rr = _iota2((CHUNK, qw), 0).astype(F32)
    lg_f = lgl_ref[0:1, :]
    lg_b = lgl_ref[1:2, :]

    def rope(x, cs, sn):
        x1, x2 = x[:, :LANES], x[:, LANES:]
        return jnp.concatenate([x1 * cs - x2 * sn, x1 * sn + x2 * cs], axis=1)

    def fwd(c, carry):
        rows = _rows(c)
        q = in_ref[rows, 0:qw].astype(F32)
        k = in_ref[rows, qw:2 * qw].astype(F32)
        if use_rope:
            cs, sn = cos_ref[rows, :], sin_ref[rows, :]
            q, k = rope(q, cs, sn), rope(k, cs, sn)
        qk_s[rows, 0:qw] = q
        qk_s[rows, qw:2 * qw] = k
        kb = k.astype(BF16)
        q_dec = (q * jnp.exp((rr + 1.0) * lg_f)).astype(BF16)
        k_dec = k * jnp.exp((CHUNK - 1.0 - rr) * lg_f)
        for h in range(H_R):
            hs = slice(h * DV_R, (h + 1) * DV_R)
            mh = lane_head == h
            vh = in_ref[rows, 2 * qw + h * DV_R:2 * qw + (h + 1) * DV_R]
            s = _dot_nt(jnp.where(mh, q, 0.0).astype(BF16), kb)
            intra = _dot((s * dm_s[h]).astype(BF16), vh)
            sf = st[0, h]
            oacc[rows, hs] = intra + _dot(q_dec, sf.astype(BF16))
            st[0, h] = (sf * jnp.exp(CHUNK * lgv_ref[0, h:h + 1, :])
                        + _dot_tn(jnp.where(mh, k_dec, 0.0).astype(BF16), vh))
        return carry

    lax.fori_loop(0, n_chunks, fwd, 0)

    def bwd(t, carry):
        rows = _rows(n_chunks - 1 - t)
        q = qk_s[rows, 0:qw]
        k = qk_s[rows, qw:2 * qw]
        q_dec = (q * jnp.exp((CHUNK - rr) * lg_b)).astype(BF16)
        k_dec = k * jnp.exp(rr * lg_b)
        for h in range(H_R):
            hs = slice(h * DV_R, (h + 1) * DV_R)
            mh = lane_head == h
            vh = in_ref[rows, 2 * qw + h * DV_R:2 * qw + (h + 1) * DV_R]
            sb = st[1, h]
            oacc[rows, hs] = oacc[rows, hs] + _dot(q_dec, sb.astype(BF16))
            st[1, h] = (sb * jnp.exp(CHUNK * lgv_ref[1, h:h + 1, :])
                        + _dot_tn(jnp.where(mh, k_dec, 0.0).astype(BF16), vh))
        return carry

    lax.fori_loop(0, n_chunks, bwd, 0)
    sfin_ref[0] = st[...]

    def fin(c, carry):
        rows = _rows(c)
        for h in range(H_R):
            hs = slice(h * DV_R, (h + 1) * DV_R)
            o = oacc[rows, hs]
            oc = o - jnp.mean(o, axis=-1, keepdims=True)
            y = oc * lax.rsqrt(jnp.mean(oc * oc, axis=-1, keepdims=True) + EPS) * gn_ref[:, hs]
            g = in_ref[rows, 4 * qw + h * DV_R:4 * qw + (h + 1) * DV_R].astype(F32)
            o_ref[rows, hs] = (y * _silu(g)).astype(o_ref.dtype)
        return carry

    lax.fori_loop(0, n_chunks, fin, 0)


def _retention(ret_in, row0, b, l, cos4, sin4, lgl, lgv, gn, s0):
    use_rope = cos4 is not None
    if not use_rope:
        cos4 = jnp.zeros((SUBLANES, LANES), F32)
        sin4 = cos4
        trig_spec = pl.BlockSpec((SUBLANES, LANES), lambda i: (0, 0))
    else:
        trig_spec = pl.BlockSpec((l, LANES), lambda i: (0, 0))
    blk0 = row0 // l
    n_chunks = l // CHUNK
    sshape = (1, 2, H_R, 2 * LANES, DV_R)
    return pl.pallas_call(
        functools.partial(_ret_kernel, use_rope, n_chunks),
        grid=(b,),
        in_specs=[pl.BlockSpec((l, ret_in.shape[1]), lambda i: (blk0 + i, 0)),
                  trig_spec, trig_spec,
                  _const_spec(lgl.shape), _const_spec(lgv.shape), _const_spec(gn.shape),
                  pl.BlockSpec(sshape, lambda i: (i, 0, 0, 0, 0))],
        out_specs=[pl.BlockSpec((l, BR_W), lambda i: (i, 0)),
                   pl.BlockSpec(sshape, lambda i: (i, 0, 0, 0, 0))],
        out_shape=[jax.ShapeDtypeStruct((b * l, BR_W), BF16),
                   jax.ShapeDtypeStruct((b,) + sshape[1:], F32)],
        scratch_shapes=[pltpu.VMEM((l, 4 * LANES), F32),
                        pltpu.VMEM((l, BR_W), F32),
                        pltpu.VMEM(sshape[1:], F32),
                        pltpu.VMEM((H_R, CHUNK, CHUNK), F32)],
        compiler_params=_CP(dimension_semantics=("arbitrary",)),
        name="retention",
    )(ret_in, cos4, sin4, lgl, lgv, gn, s0)


def _ssd_kernel(n_chunks, in_ref, dt_ref, cw_ref, cb_ref, dtb_ref, na_ref, dsk_ref, ng_ref, s0_ref,
                o_ref, sfin_ref, pad_s, xc_s, y_s, dt_s, cumb_s, st):
    l = in_ref.shape[0]
    cw = 2 * BR_W
    halo = SUBLANES
    pad_s[0:halo, :] = jnp.zeros((halo, cw), F32)
    pad_s[l + halo:l + 2 * halo, :] = jnp.zeros((halo, cw), F32)

    def fill(c, carry):
        rows = _rows(c)
        dst = pl.ds(pl.multiple_of(c * CHUNK + halo, SUBLANES), CHUNK)
        pad_s[dst, :] = in_ref[rows, BR_W:BR_W + cw].astype(F32)
        return carry

    lax.fori_loop(0, n_chunks, fill, 0)
    st[...] = s0_ref[0]

    ii = _iota2((CHUNK, CHUNK), 0)
    jj = _iota2((CHUNK, CHUNK), 1)
    tril = jnp.where(ii >= jj, 1.0, 0.0).astype(BF16)
    triu = jnp.where(jj >= ii, 1.0, 0.0).astype(BF16)
    lane_lo = jj < P_M
    row_lo = ii < N_M
    blockdiag = lane_lo == row_lo
    half = D_CONV // 2

    def pair_cols(vals, h0, h1):
        return jnp.where(lane_lo, vals[:, h0:h0 + 1], vals[:, h1:h1 + 1])

    def fwd(c, carry):
        rows = _rows(c)
        win = pad_s[pl.ds(pl.multiple_of(c * CHUNK, CHUNK), CHUNK + 2 * halo), :]
        acc = win[halo - half:halo - half + CHUNK, :] * cw_ref[0:1, :] + cb_ref[...]
        for w in range(1, D_CONV):
            acc = acc + win[halo - half + w:halo - half + w + CHUNK, :] * cw_ref[w:w + 1, :]
        xc = _silu(acc)
        xc_s[rows, :] = xc.astype(xc_s.dtype)
        x = xc[:, 0:BR_W]
        dt = _softplus(dt_ref[rows, :] + dtb_ref[...])
        la = dt * na_ref[...]
        cumf = _tri_dot(tril, la)
        cumb = _tri_dot(triu, la)
        dt_s[rows, :] = dt
        cumb_s[rows, :] = cumb
        cumf_t, cumb_t, dt_t = cumf.T, cumb.T, dt.T
        totf = cumf[CHUNK - 1:CHUNK, :]
        etotf = jnp.exp(totf)
        kdec = dt * jnp.exp(totf - cumf)
        qdec = jnp.exp(cumf)
        for g in range(G_M):
            b2 = xc[:, BR_W + g * LANES:BR_W + (g + 1) * LANES]
            c2 = xc[:, BR_W + (G_M + g) * LANES:BR_W + (G_M + g + 1) * LANES]
            cbm = _dot_nt(jnp.where(lane_lo, c2, 0.0).astype(BF16), b2.astype(BF16))
            for pp in range(2):
                p = 2 * g + pp
                h0, h1 = 2 * p, 2 * p + 1
                ps = slice(p * LANES, (p + 1) * LANES)
                ms = []
                for h in (h0, h1):
                    hb = H_M + h
                    mf = jnp.where(ii >= jj, jnp.exp(cumf[:, h:h + 1] - cumf_t[h:h + 1, :]), 0.0) * dt_t[h:h + 1, :]
                    mb = (jnp.where(jj >= ii, jnp.exp(cumb[:, hb:hb + 1] - cumb_t[hb:hb + 1, :]), 0.0)
                          * dt_t[hb:hb + 1, :])
                    ms.append((cbm * (mf + mb)).astype(BF16))
                xp = x[:, ps]
                xbd = jnp.concatenate([jnp.where(lane_lo, xp, 0.0), jnp.where(lane_lo, 0.0, xp)], axis=0)
                intra = _dot(jnp.concatenate(ms, axis=1), xbd.astype(BF16))
                sf = st[0, p]
                inter = _dot((c2 * pair_cols(qdec, h0, h1)).astype(BF16), sf.astype(BF16))
                y_s[rows, ps] = intra + inter
                kv = _dot_tn((b2 * pair_cols(kdec, h0, h1)).astype(BF16), xp.astype(BF16))
                arow = jnp.where(row_lo, etotf[:, h0:h0 + 1], etotf[:, h1:h1 + 1])
                st[0, p] = sf * arow + jnp.where(blockdiag, kv, 0.0)
        return carry

    lax.fori_loop(0, n_chunks, fwd, 0)

    def bwd(t, carry):
        rows = _rows(n_chunks - 1 - t)
        xc = xc_s[rows, :].astype(F32)
        x = xc[:, 0:BR_W]
        dt = dt_s[rows, :]
        cumb = cumb_s[rows, :]
        totb = cumb[0:1, :]
        etotb = jnp.exp(totb)
        kdec = dt * jnp.exp(totb - cumb)
        qdec = jnp.exp(cumb)
        for g in range(G_M):
            b2 = xc[:, BR_W + g * LANES:BR_W + (g + 1) * LANES]
            c2 = xc[:, BR_W + (G_M + g) * LANES:BR_W + (G_M + g + 1) * LANES]
            for pp in range(2):
                p = 2 * g + pp
                h0, h1 = H_M + 2 * p, H_M + 2 * p + 1
                ps = slice(p * LANES, (p + 1) * LANES)
                xp = x[:, ps]
                sb = st[1, p]
                y_s[rows, ps] = y_s[rows, ps] + _dot((c2 * pair_cols(qdec, h0, h1)).astype(BF16), sb.astype(BF16))
                kv = _dot_tn((b2 * pair_cols(kdec, h0, h1)).astype(BF16), xp.astype(BF16))
                arow = jnp.where(row_lo, etotb[:, h0:h0 + 1], etotb[:, h1:h1 + 1])
                st[1, p] = sb * arow + jnp.where(blockdiag, kv, 0.0)
        return carry

    lax.fori_loop(0, n_chunks, bwd, 0)
    sfin_ref[0] = st[...]

    def fin(c, carry):
        rows = _rows(c)
        x = xc_s[rows, 0:BR_W].astype(F32)
        z = in_ref[rows, 0:BR_W].astype(F32)
        u = (y_s[rows, :] + x * dsk_ref[...]) * _silu(z)
        y = u * lax.rsqrt(jnp.mean(u * u, axis=-1, keepdims=True) + EPS) * ng_ref[...]
        o_ref[rows, :] = y.astype(o_ref.dtype)
        return carry

    lax.fori_loop(0, n_chunks, fin, 0)


def _ssd(ssm_in, prec, row0, b, l, cw, cb, dtb, na, dsk, ng, s0):
    blk0 = row0 // l
    n_chunks = l // CHUNK
    dt_col = (prec.shape[1] - LANES) // LANES
    sshape = (1, 2, H_M // 2, LANES, LANES)
    return pl.pallas_call(
        functools.partial(_ssd_kernel, n_chunks),
        grid=(b,),
        in_specs=[pl.BlockSpec((l, ssm_in.shape[1]), lambda i: (blk0 + i, 0)),
                  pl.BlockSpec((l, LANES), lambda i: (blk0 + i, dt_col)),
                  _const_spec(cw.shape), _const_spec(cb.shape), _const_spec(dtb.shape),
                  _const_spec(na.shape), _const_spec(dsk.shape), _const_spec(ng.shape),
                  pl.BlockSpec(sshape, lambda i: (i, 0, 0, 0, 0))],
        out_specs=[pl.BlockSpec((l, BR_W), lambda i: (i, 0)),
                   pl.BlockSpec(sshape, lambda i: (i, 0, 0, 0, 0))],
        out_shape=[jax.ShapeDtypeStruct((b * l, BR_W), BF16),
                   jax.ShapeDtypeStruct((b,) + sshape[1:], F32)],
        scratch_shapes=[pltpu.VMEM((l + 2 * SUBLANES, 2 * BR_W), F32),
                        pltpu.VMEM((l, 2 * BR_W), BF16),
                        pltpu.VMEM((l, BR_W), F32),
                        pltpu.VMEM((l, LANES), F32),
                        pltpu.VMEM((l, LANES), F32),
                        pltpu.VMEM(sshape[1:], F32)],
        compiler_params=_CP(dimension_semantics=("arbitrary",)),
        name="ssd",
    )(ssm_in, prec, cw, cb, dtb, na, dsk, ng, s0)


_HG_LEVELS = (8, 16, 32, 64)


def _bcast_group_row(x, j):
    x3 = x.reshape(CHUNK // SUBLANES, SUBLANES, LANES)
    r = jnp.broadcast_to(x3[:, j:j + 1, :], x3.shape)
    return r.reshape(CHUNK, LANES)


def _bcast_block_row(x, size, j):
    pieces = [jnp.broadcast_to(x[b * size + j:b * size + j + 1, :], (size, LANES)) for b in range(CHUNK // size)]
    return pieces[0] if len(pieces) == 1 else jnp.concatenate(pieces, axis=0)


def _hgrn_kernel(n_chunks, in_ref, f_ref, llb_ref, oml_ref, l1m_ref, ng_ref, s0_ref,
                 o_ref, sfin_ref, oacc, st):
    ii = _iota2((CHUNK, CHUNK), 0)
    jj = _iota2((CHUNK, CHUNK), 1)
    tril = jnp.where(ii >= jj, 1.0, 0.0).astype(BF16)
    triu = jnp.where(jj >= ii, 1.0, 0.0).astype(BF16)
    sub = ii % SUBLANES
    same_group = (ii // SUBLANES) == (jj // SUBLANES)
    sel_r = _iota2((SUBLANES * LANES, CHUNK), 0) // LANES
    sel_c = _iota2((SUBLANES * LANES, CHUNK), 1) % SUBLANES
    group_sel = jnp.where(sel_r == sel_c, 1.0, 0.0).astype(BF16)
    st[...] = s0_ref[0]

    def gates(rows, d):
        fr = f_ref[rows, d * BR_W:(d + 1) * BR_W]
        ds_ = slice(d * BR_W, (d + 1) * BR_W)
        a = llb_ref[:, ds_]
        bterm = l1m_ref[:, ds_] + _log_sigmoid(fr)
        logf = jnp.maximum(a, bterm) + jnp.log1p(jnp.exp(-jnp.abs(a - bterm)))
        key = oml_ref[:, ds_] * _sigmoid(-fr)
        return logf, key

    def fwd(c, carry):
        rows = _rows(c)
        logf_f, key_f = gates(rows, 0)
        logf_b, key_b = gates(rows, 1)
        cumf_all = _tri_dot(tril, logf_f)
        cumb_all = _tri_dot(triu, logf_b)
        for h in range(H_C):
            hs = slice(h * LANES, (h + 1) * LANES)
            q = in_ref[rows, hs].astype(F32)
            v = in_ref[rows, BR_W + h * LANES:BR_W + (h + 1) * LANES]
            cumf, cumb = cumf_all[:, hs], cumb_all[:, hs]
            kf, kb = key_f[:, hs], key_b[:, hs]
            sc = jnp.zeros((CHUNK, CHUNK), F32)
            for m in _HG_LEVELS:
                upper = (ii % (2 * m)) >= m
                same_block = (ii // (2 * m)) == (jj // (2 * m))
                ref_f = _bcast_block_row(cumf, 2 * m, m - 1)
                ref_b = _bcast_block_row(cumb, 2 * m, m)
                e_f = jnp.exp(jnp.where(upper, cumf - ref_f, ref_f - cumf))
                e_b = jnp.exp(jnp.where(upper, ref_b - cumb, cumb - ref_b))
                qcat = jnp.concatenate([jnp.where(upper, q * e_f, 0.0), jnp.where(upper, 0.0, q * e_b)], axis=1)
                kcat = jnp.concatenate([jnp.where(upper, 0.0, kf * e_f), jnp.where(upper, kb * e_b, 0.0)], axis=1)
                sc = sc + jnp.where(same_block, _dot_nt(qcat.astype(BF16), kcat.astype(BF16)), 0.0)
            prods = []
            for j in range(SUBLANES):
                arg_f = jnp.where(sub >= j, cumf - _bcast_group_row(cumf, j), NEG_BIG)
                arg_b = jnp.where(sub <= j, cumb - _bcast_group_row(cumb, j), NEG_BIG)
                pj = q * (_bcast_group_row(kf, j) * jnp.exp(arg_f) + _bcast_group_row(kb, j) * jnp.exp(arg_b))
                prods.append(pj.astype(BF16))
            diag = _dot(jnp.concatenate(prods, axis=1), group_sel)
            sc = sc + jnp.where(same_group, diag, 0.0)
            intra = _dot(sc.astype(BF16), v)
            stf = st[0, h]
            totf = cumf[CHUNK - 1:CHUNK, :]
            inter = _dot_nt((q * jnp.exp(cumf)).astype(BF16), stf.astype(BF16))
            oacc[rows, hs] = intra + inter
            st[0, h] = stf * jnp.exp(totf) + _dot_tn(v, (kf * jnp.exp(totf - cumf)).astype(BF16))
        return carry

    lax.fori_loop(0, n_chunks, fwd, 0)

    def bwd(t, carry):
        rows = _rows(n_chunks - 1 - t)
        logf_b, key_b = gates(rows, 1)
        cumb_all = _tri_dot(triu, logf_b)
        for h in range(H_C):
            hs = slice(h * LANES, (h + 1) * LANES)
            q = in_ref[rows, hs].astype(F32)
            v = in_ref[rows, BR_W + h * LANES:BR_W + (h + 1) * LANES]
            cumb, kb = cumb_all[:, hs], key_b[:, hs]
            stb = st[1, h]
            totb = cumb[0:1, :]
            oacc[rows, hs] = oacc[rows, hs] + _dot_nt((q * jnp.exp(cumb)).astype(BF16), stb.astype(BF16))
            st[1, h] = stb * jnp.exp(totb) + _dot_tn(v, (kb * jnp.exp(totb - cumb)).astype(BF16))
        return carry

    lax.fori_loop(0, n_chunks, bwd, 0)
    sfin_ref[0] = st[...]

    def fin(c, carry):
        rows = _rows(c)
        for h in range(H_C):
            hs = slice(h * LANES, (h + 1) * LANES)
            o = oacc[rows, hs]
            y = o * lax.rsqrt(jnp.mean(o * o, axis=-1, keepdims=True) + EPS) * ng_ref[:, hs]
            g = in_ref[rows, 2 * BR_W + h * LANES:2 * BR_W + (h + 1) * LANES].astype(F32)
            o_ref[rows, hs] = (y * _silu(g)).astype(o_ref.dtype)
        return carry

    lax.fori_loop(0, n_chunks, fin, 0)


def _hgrn(hg_in, prec, row0, b, l, llb, oml, l1m, ng, s0):
    blk0 = row0 // l
    n_chunks = l // CHUNK
    sshape = (1, 2, H_C, DV_C, E_C)
    return pl.pallas_call(
        functools.partial(_hgrn_kernel, n_chunks),
        grid=(b,),
        in_specs=[pl.BlockSpec((l, hg_in.shape[1]), lambda i: (blk0 + i, 0)),
                  pl.BlockSpec((l, 2 * BR_W), lambda i: (blk0 + i, 0)),
                  _const_spec(llb.shape), _const_spec(oml.shape), _const_spec(l1m.shape),
                  _const_spec(ng.shape),
                  pl.BlockSpec(sshape, lambda i: (i, 0, 0, 0, 0))],
        out_specs=[pl.BlockSpec((l, BR_W), lambda i: (i, 0)),
                   pl.BlockSpec(sshape, lambda i: (i, 0, 0, 0, 0))],
        out_shape=[jax.ShapeDtypeStruct((b * l, BR_W), BF16),
                   jax.ShapeDtypeStruct((b,) + sshape[1:], F32)],
        scratch_shapes=[pltpu.VMEM((l, BR_W), F32),
                        pltpu.VMEM(sshape[1:], F32)],
        compiler_params=_CP(dimension_semantics=("arbitrary",)),
        name="hgrn2",
    )(hg_in, prec, llb, oml, l1m, ng, s0)


def _merge_kernel(x_ref, mod_ref, n2_ref, o0, o1, o2, gl_ref, wb_ref, wo_ref, xo_ref, h2_ref, hs_ref):
    d = x_ref.shape[1]
    merged = jnp.zeros(x_ref.shape, F32)
    for k, o in enumerate((o0, o1, o2)):
        gate = _sigmoid(gl_ref[:, k * d:(k + 1) * d].astype(F32))
        merged = merged + gate * _dot(o[...], wb_ref[k])
    mix = _dot(merged.astype(BF16), wo_ref[...])
    xn = x_ref[...] + mod_ref[0, 2:3, :] * mix
    xo_ref[...] = xn
    h2 = _modulated_norm(xn, n2_ref[...], mod_ref[0, 3:4, :], mod_ref[0, 4:5, :])
    h2_ref[...] = h2.astype(h2_ref.dtype)
    rows = x_ref.shape[0]
    for s in range(SLAB):
        hs_ref[pl.ds(s, rows, stride=SLAB), :] = h2[:, s * LANES:(s + 1) * LANES]


def _merge(x_all, mod, norm_g, o_ret, o_ssm, o_hg, gl, wb, wo, mod_index):
    t, d = x_all.shape
    row = lambda r: (r, 0)
    return pl.pallas_call(
        _merge_kernel,
        grid=(t // ROW_TILE,),
        in_specs=[pl.BlockSpec((ROW_TILE, d), row),
                  pl.BlockSpec((1, 6, d), mod_index),
                  pl.BlockSpec((1, d), lambda r: (0, 0)),
                  pl.BlockSpec((ROW_TILE, BR_W), row),
                  pl.BlockSpec((ROW_TILE, BR_W), row),
                  pl.BlockSpec((ROW_TILE, BR_W), row),
                  pl.BlockSpec((ROW_TILE, N_BRANCH * d), row),
                  pl.BlockSpec(wb.shape, lambda r: (0, 0, 0), pipeline_mode=pl.Buffered(1)),
                  pl.BlockSpec(wo.shape, lambda r: (0, 0), pipeline_mode=pl.Buffered(1))],
        out_specs=[pl.BlockSpec((ROW_TILE, d), row),
                   pl.BlockSpec((ROW_TILE, d), row),
                   pl.BlockSpec((ROW_TILE * SLAB, LANES), row)],
        out_shape=[jax.ShapeDtypeStruct((t, d), F32),
                   jax.ShapeDtypeStruct((t, d), BF16),
                   jax.ShapeDtypeStruct((t * SLAB, LANES), F32)],
        compiler_params=_CP(dimension_semantics=("arbitrary",)),
        name="merge_outproj",
    )(x_all, mod, norm_g.reshape(1, d), o_ret, o_ssm, o_hg, gl, wb, wo)


def _router_kernel(x_ref, mod_ref, n2_ref, rwt_ref, rb_ref, idx_ref, w_ref):
    tm = x_ref.shape[0]
    h2 = _modulated_norm(x_ref[...], n2_ref[...], mod_ref[0, 3:4, :], mod_ref[0, 4:5, :])
    logits = lax.dot_general(rwt_ref[...], h2, (((1,), (1,)), ((), ())),
                             preferred_element_type=F32, precision=HIGHEST)
    scores = _sigmoid(logits)
    biased = scores + rb_ref[...]
    gsz = N_EXPERTS // N_GROUPS
    neg_inf = -jnp.inf

    b3 = biased.reshape(N_GROUPS, gsz, tm)
    e_in_g = lax.broadcasted_iota(jnp.int32, (N_GROUPS, gsz, tm), 1)
    m1 = jnp.max(b3, axis=1, keepdims=True)
    first = jnp.min(jnp.where(b3 == m1, e_in_g, gsz), axis=1, keepdims=True)
    m2 = jnp.max(jnp.where(e_in_g == first, neg_inf, b3), axis=1, keepdims=True)
    gscore = m1 + m2

    g_iota = lax.broadcasted_iota(jnp.int32, (N_GROUPS, 1, tm), 0)
    chosen = jnp.zeros((N_GROUPS, 1, tm), jnp.int32)
    for _ in range(TOPK_GROUPS):
        m = jnp.max(gscore, axis=0, keepdims=True)
        first = jnp.min(jnp.where(gscore == m, g_iota, N_GROUPS), axis=0, keepdims=True)
        hit = g_iota == first
        chosen = jnp.where(hit, 1, chosen)
        gscore = jnp.where(hit, neg_inf, gscore)
    emask = jnp.broadcast_to(chosen, (N_GROUPS, gsz, tm)).reshape(N_EXPERTS, tm)

    cand = jnp.where(emask > 0, biased, neg_inf)
    e_iota = _iota2((N_EXPERTS, tm), 0)
    idxs, ws = [], []
    for _ in range(TOP_K):
        m = jnp.max(cand, axis=0, keepdims=True)
        first = jnp.min(jnp.where(cand == m, e_iota, N_EXPERTS), axis=0, keepdims=True)
        hit = e_iota == first
        idxs.append(first)
        ws.append(jnp.sum(jnp.where(hit, scores, 0.0), axis=0, keepdims=True))
        cand = jnp.where(hit, neg_inf, cand)
    wsum = ws[0]
    for w in ws[1:]:
        wsum = wsum + w
    pad = SUBLANES - TOP_K
    idx_ref[...] = jnp.concatenate(idxs + [jnp.zeros((pad, tm), jnp.int32)], axis=0)
    w_ref[...] = jnp.concatenate([ROUTED_SCALE * w / wsum for w in ws] + [jnp.zeros((pad, tm), F32)], axis=0)


def _router(x_new, mod, norm_g, rwt, rb, mod_index):
    t, d = x_new.shape
    return pl.pallas_call(
        _router_kernel,
        grid=(t // ROW_TILE,),
        in_specs=[pl.BlockSpec((ROW_TILE, d), lambda r: (r, 0)),
                  pl.BlockSpec((1, 6, d), mod_index),
                  pl.BlockSpec((1, d), lambda r: (0, 0)),
                  pl.BlockSpec(rwt.shape, lambda r: (0, 0)),
                  pl.BlockSpec(rb.shape, lambda r: (0, 0))],
        out_specs=[pl.BlockSpec((SUBLANES, ROW_TILE), lambda r: (0, r)),
                   pl.BlockSpec((SUBLANES, ROW_TILE), lambda r: (0, r))],
        out_shape=[jax.ShapeDtypeStruct((SUBLANES, t), jnp.int32),
                   jax.ShapeDtypeStruct((SUBLANES, t), F32)],
        compiler_params=_CP(dimension_semantics=("arbitrary",)),
        name="router",
    )(x_new, mod, norm_g.reshape(1, d), rwt, rb)


def _expert_kernel(be_ref, nv_ref, tok_ref, tokn_ref, dst_ref, hs_hbm, wg_ref, wu_ref, wdn_ref, y_hbm,
                   xg, yb, wgu_s, wd_s, gsem, ssem):
    b = pl.program_id(0)
    nv = nv_ref[0]
    slot = b % 2
    slab_rows = MOE_ROWS * SLAB

    new_expert = jnp.logical_or(b == 0, be_ref[b] != be_ref[jnp.maximum(b - 1, 0)])

    @pl.when(jnp.logical_and(b < nv, new_expert))
    def _():
        wgu_s[:, :D_EXPERT] = wg_ref[0].astype(BF16)
        wgu_s[:, D_EXPERT:] = wu_ref[0].astype(BF16)
        wd_s[...] = wdn_ref[0].astype(BF16)

    def slab(r):
        return pl.ds(pl.multiple_of(r * SLAB, SLAB), SLAB)

    def gather(ids_ref, s):
        def issue(r, carry):
            pltpu.make_async_copy(hs_hbm.at[slab(ids_ref[0, 0, r])], xg.at[s, slab(r)], gsem.at[s]).start()
            return carry
        lax.fori_loop(0, MOE_ROWS, issue, 0, unroll=8)

    def scatter_copy_all(s):
        return pltpu.make_async_copy(yb.at[s], y_hbm.at[pl.ds(0, slab_rows)], ssem.at[s])

    @pl.when(b == 0)
    def _():
        gather(tok_ref, 0)
        spare0 = y_hbm.shape[0] - 2 * slab_rows
        yb[...] = jnp.zeros(yb.shape, yb.dtype)
        for s in range(2):
            init = pltpu.make_async_copy(yb.at[s], y_hbm.at[pl.ds(spare0 + s * slab_rows, slab_rows)], ssem.at[s])
            init.start()
            init.wait()

    @pl.when(b + 1 < nv)
    def _():
        gather(tokn_ref, 1 - slot)

    @pl.when(b < nv)
    def _():
        pltpu.make_async_copy(hs_hbm.at[pl.ds(0, slab_rows)], xg.at[slot], gsem.at[slot]).wait()
        lhs = jnp.concatenate([xg[slot, pl.ds(k, MOE_ROWS, stride=SLAB), :] for k in range(SLAB)], axis=1)
        gu = _dot(lhs.astype(BF16), wgu_s[...])
        act = _silu(gu[:, :D_EXPERT]) * gu[:, D_EXPERT:]
        y = _dot(act.astype(BF16), wd_s[...])

        @pl.when(b >= 2)
        def _():
            scatter_copy_all(slot).wait()

        for k in range(SLAB):
            yb[slot, pl.ds(k, MOE_ROWS, stride=SLAB), :] = y[:, k * LANES:(k + 1) * LANES]

        def issue(r, carry):
            pltpu.make_async_copy(yb.at[slot, slab(r)], y_hbm.at[slab(dst_ref[0, 0, r])], ssem.at[slot]).start()
            return carry
        lax.fori_loop(0, MOE_ROWS, issue, 0, unroll=8)

        @pl.when(b == nv - 1)
        def _():
            scatter_copy_all(slot).wait()

            @pl.when(nv >= 2)
            def _():
                scatter_copy_all(1 - slot).wait()


def _experts(hs, be, nv, tok, dst, wg, wu, wdn, n_out_rows):
    n_blocks = tok.shape[0]
    ids = lambda b, be_r, nv_r: (b, 0, 0)
    ids_next = lambda b, be_r, nv_r: (jnp.minimum(b + 1, n_blocks - 1), 0, 0)
    by_expert = lambda b, be_r, nv_r: (be_r[b], 0, 0)
    smem_ids = functools.partial(pl.BlockSpec, (1, 1, MOE_ROWS), memory_space=pltpu.SMEM)
    grid_spec = pltpu.PrefetchScalarGridSpec(
        num_scalar_prefetch=2,
        grid=(n_blocks,),
        in_specs=[smem_ids(ids), smem_ids(ids_next), smem_ids(ids),
                  pl.BlockSpec(memory_space=pl.ANY),
                  pl.BlockSpec((1,) + wg.shape[1:], by_expert),
                  pl.BlockSpec((1,) + wu.shape[1:], by_expert),
                  pl.BlockSpec((1,) + wdn.shape[1:], by_expert)],
        out_specs=pl.BlockSpec(memory_space=pl.ANY),
        scratch_shapes=[pltpu.VMEM((2, MOE_ROWS * SLAB, LANES), F32),
                        pltpu.VMEM((2, MOE_ROWS * SLAB, LANES), F32),
                        pltpu.VMEM((wg.shape[1], 2 * D_EXPERT), BF16),
                        pltpu.VMEM(wdn.shape[1:], BF16),
                        pltpu.SemaphoreType.DMA((2,)),
                        pltpu.SemaphoreType.DMA((2,))])
    return pl.pallas_call(
        _expert_kernel,
        grid_spec=grid_spec,
        out_shape=jax.ShapeDtypeStruct((n_out_rows * SLAB, LANES), F32),
        compiler_params=_CP(dimension_semantics=("arbitrary",)),
        name="routed_experts",
    )(be, nv, tok, tok, dst, hs, wg, wu, wdn)


def _moe_plan(idx8, t):
    p = t * TOP_K
    blk_rows = MOE_ROWS
    idx = idx8[:TOP_K]
    order = jnp.argsort(idx.T.reshape(p)).astype(jnp.int32)
    experts = jnp.arange(N_EXPERTS, dtype=jnp.int32)
    counts = jnp.sum((idx[None, :, :] == experts[:, None, None]).astype(jnp.int32), axis=(1, 2))
    starts = jnp.cumsum(counts) - counts
    padded = (counts + blk_rows - 1) // blk_rows * blk_rows
    pends = jnp.cumsum(padded)
    pstarts = pends - padded
    n_blocks = p // blk_rows + N_EXPERTS
    blk0 = jnp.arange(n_blocks, dtype=jnp.int32) * blk_rows
    block_expert = jnp.minimum(jnp.sum((pends[None, :] <= blk0[:, None]).astype(jnp.int32), axis=1),
                               N_EXPERTS - 1)
    nv = (pends[-1] // blk_rows).astype(jnp.int32).reshape(1)
    off = blk0 - pstarts[block_expert]
    n_valid = jnp.clip(counts[block_expert] - off, 0, blk_rows)
    r = jnp.arange(blk_rows, dtype=jnp.int32)
    src = jnp.clip((starts[block_expert] + off)[:, None] + r[None, :], 0, p - 1)
    flat = order[src]
    valid = r[None, :] < n_valid[:, None]
    tok = jnp.where(valid, flat // TOP_K, 0).astype(jnp.int32)
    dump = p + ((blk0 // blk_rows) % 2 * blk_rows)[:, None] + r[None, :]
    dst = jnp.where(valid, flat, dump).astype(jnp.int32)
    shape = (n_blocks, 1, blk_rows)
    return block_expert.astype(jnp.int32), nv, tok.reshape(shape), dst.reshape(shape), p + 2 * blk_rows


def _combine_kernel(final, x_ref, h2_ref, y_ref, w_ref, mod_ref, fn_ref, wsgu_ref, wsd_ref, o_ref):
    tm = x_ref.shape[0]
    gu = _dot(h2_ref[...], wsgu_ref[...])
    shared = _dot((_silu(gu[:, :D_SHARED]) * gu[:, D_SHARED:]).astype(BF16), wsd_ref[...])
    w = w_ref[...]
    stride = TOP_K * SLAB
    pieces = []
    for s in range(SLAB):
        acc = w[:, 0:1] * y_ref[pl.ds(s, tm, stride=stride), :]
        for k in range(1, TOP_K):
            acc = acc + w[:, k:k + 1] * y_ref[pl.ds(k * SLAB + s, tm, stride=stride), :]
        pieces.append(acc)
    routed = jnp.concatenate(pieces, axis=1)
    xo = x_ref[...] + mod_ref[0, 5:6, :] * (routed + shared)
    if final:
        xo = xo * lax.rsqrt(jnp.mean(xo * xo, axis=-1, keepdims=True) + EPS) * fn_ref[...]
    o_ref[...] = xo


def _combine(final, x_new, h2, y, w_tk, mod, final_norm, wsgu, wsd, mod_index):
    t, d = x_new.shape
    row = lambda r: (r, 0)
    return pl.pallas_call(
        functools.partial(_combine_kernel, final),
        grid=(t // ROW_TILE,),
        in_specs=[pl.BlockSpec((ROW_TILE, d), row),
                  pl.BlockSpec((ROW_TILE, d), row),
                  pl.BlockSpec((ROW_TILE * TOP_K * SLAB, LANES), row),
                  pl.BlockSpec((ROW_TILE, SUBLANES), row),
                  pl.BlockSpec((1, 6, d), mod_index),
                  pl.BlockSpec((1, d), lambda r: (0, 0)),
                  pl.BlockSpec(wsgu.shape, lambda r: (0, 0)),
                  pl.BlockSpec(wsd.shape, lambda r: (0, 0))],
        out_specs=pl.BlockSpec((ROW_TILE, d), row),
        out_shape=jax.ShapeDtypeStruct((t, d), F32),
        compiler_params=_CP(dimension_semantics=("arbitrary",)),
        name="combine_shared",
    )(x_new, h2, y, w_tk, mod, final_norm.reshape(1, d), wsgu, wsd)


def _qk_perm():
    half = DK_R // 2
    perm = np.zeros(RET_QK_W, np.int32)
    for part in range(2):
        for h in range(H_R):
            for j in range(half):
                perm[part * LANES + h * half + j] = h * DK_R + part * half + j
    return perm


def _ret_state_rows():
    half = DK_R // 2
    rows = np.zeros((H_R, DK_R), np.int32)
    for h in range(H_R):
        for j in range(DK_R):
            rows[h, j] = (j // half) * LANES + h * half + j % half
    return rows


def _xbc_channels():
    ch = list(range(BR_W))
    for base in (BR_W, BR_W + G_M * N_M):
        for g in range(G_M):
            grp = list(range(base + g * N_M, base + (g + 1) * N_M))
            ch += grp + grp
    return np.asarray(ch, np.int32)


def _pad_lanes(v):
    return jnp.zeros((1, LANES), F32).at[0, :v.shape[0]].set(v.astype(F32))


def _ret_state_pack(s):
    rows = _ret_state_rows()
    src = np.full((H_R, 2 * LANES), DK_R, np.int32)
    for h in range(H_R):
        src[h, rows[h]] = np.arange(DK_R)
    sz = jnp.concatenate([s, jnp.zeros(s.shape[:3] + (1, DV_R), s.dtype)], axis=3)
    return sz[:, :, np.arange(H_R)[:, None], src, :]


def _ret_state_unpack(s):
    return s[:, :, np.arange(H_R)[:, None], _ret_state_rows(), :]


def _ssm_state_pack(s):
    b = s.shape[0]
    sr = s.reshape(b, 2, H_M // 2, 2, N_M, 1, P_M)
    eye = jnp.eye(2, dtype=s.dtype).reshape(1, 1, 1, 2, 1, 2, 1)
    return (sr * eye).reshape(b, 2, H_M // 2, 2 * N_M, 2 * P_M)


def _ssm_state_unpack(s):
    b = s.shape[0]
    sr = s.reshape(b, 2, H_M // 2, 2, N_M, 2, P_M)
    return jnp.stack([sr[:, :, :, 0, :, 0, :], sr[:, :, :, 1, :, 1, :]], axis=3).reshape(b, 2, H_M, N_M, P_M)


def _grid_rope(l):
    rows = l // GRID_W
    row = jnp.repeat(jnp.arange(rows), GRID_W).astype(F32)
    col = (jnp.arange(rows * GRID_W) % GRID_W).astype(F32)
    n_freq = DK_R // 4
    freqs = ROPE_BASE ** (-jnp.arange(n_freq, dtype=F32) / n_freq)
    ang = jnp.concatenate([row[:, None] * freqs, col[:, None] * freqs], axis=-1)
    return jnp.tile(jnp.cos(ang), (1, H_R)), jnp.tile(jnp.sin(ang), (1, H_R))


def _layer_weights(i, w_in, ssm_conv_w, ssm_conv_b):
    cuts = np.cumsum(IN_SPLITS)[:-1]
    rq, rk, rv, rg, sz, sxbc, sdt, hq, hf, hi, hg, gl = jnp.split(w_in[i], cuts, axis=1)
    perm = _qk_perm()
    ch = _xbc_channels()
    w_ret = jnp.concatenate([rq[:, perm], rk[:, perm] * (DK_R ** -0.5), rv, rg], axis=1).astype(BF16)
    w_ssm = jnp.concatenate([sz, sxbc[:, ch]], axis=1).astype(BF16)
    w_hg = jnp.concatenate([hq, hi, hg], axis=1).astype(BF16)
    w_prec = jnp.concatenate([hf, sdt, jnp.zeros((w_in.shape[1], LANES - SSM_DT_W), F32)], axis=1).astype(BF16)
    conv_w = ssm_conv_w[i][ch].T
    conv_b = ssm_conv_b[i][ch][None, :]
    return (w_ret, w_ssm, w_hg, gl.astype(BF16), w_prec), conv_w, conv_b


def kernel(x_prompt, x_sample, state_ret, state_ssm, state_hgrn, c, c_ctx, ada_w, ada_b, norm1, norm2, w_in,
           ret_decay_logit, ret_gn, ssm_conv_w, ssm_conv_b, ssm_a_log, ssm_dt_bias, ssm_d, ssm_norm,
           hgrn_lb_logits, hgrn_norm, w_branch, w_out, router_w, router_b, exp_w_gate, exp_w_up,
           exp_w_down, sh_w_gate, sh_w_up, sh_w_down, final_norm):
    bc, lc, d = x_prompt.shape
    bl, ll, _ = x_sample.shape
    tc, tl = bc * lc, bl * ll
    t = tc + tl
    assert tc % ROW_TILE == 0 and ll % ROW_TILE == 0 and lc % CHUNK == 0 and ll % CHUNK == 0
    assert tl % lc == 0 and tc % ll == 0 and 1 + bl <= SUBLANES
    mod_index = _mod_index(tc // ROW_TILE, ll // ROW_TILE)

    x_all = jnp.concatenate([x_prompt.reshape(tc, d), x_sample.reshape(tl, d)], axis=0)
    c_all = jnp.concatenate([c_ctx[None, :], c, jnp.zeros((SUBLANES - 1 - bl, d), F32)], axis=0)
    mods = _ada(c_all, ada_w, ada_b)

    lb_all = jnp.cumsum(jax.nn.softmax(hgrn_lb_logits.astype(F32), axis=1), axis=1)
    lb_all = lb_all - lb_all[:, :1]
    cos4, sin4 = _grid_rope(ll)
    lane_head = (np.arange(2 * LANES) % LANES) // (DK_R // 2)

    new_ret, new_ssm, new_hg = [], [], []
    for i in range(DEPTH):
        mod = mods[i].reshape(SUBLANES, 6, d)
        weights, conv_w, conv_b = _layer_weights(i, w_in, ssm_conv_w, ssm_conv_b)
        ret_in, ssm_in, hg_in, gl, prec = _inproj(x_all, mod, norm1[i], weights, (BF16, BF16, BF16, BF16, F32),
                                                  mod_index)

        lg = jax.nn.log_sigmoid(ret_decay_logit[i].astype(F32))
        lgl = lg[:, lane_head]
        lgv = jnp.broadcast_to(lg[:, :, None], (2, H_R, LANES))
        gn = ret_gn[i][None, :]
        zero_ret = jnp.zeros((bc, 2, H_R, 2 * LANES, DV_R), F32)
        o_ret_c, s_ret = _retention(ret_in, 0, bc, lc, None, None, lgl, lgv, gn, zero_ret)
        o_ret_l, _ = _retention(ret_in, tc, bl, ll, cos4, sin4, lgl, lgv, gn, _ret_state_pack(state_ret[:, i]))
        new_ret.append(_ret_state_unpack(s_ret))

        dtb = _pad_lanes(ssm_dt_bias[i].reshape(-1))
        na = _pad_lanes(-jnp.exp(ssm_a_log[i].astype(F32)).reshape(-1))
        dsk = jnp.repeat(ssm_d[i], P_M)[None, :]
        ng = ssm_norm[i][None, :]
        zero_ssm = jnp.zeros((bc, 2, H_M // 2, LANES, LANES), F32)
        o_ssm_c, s_ssm = _ssd(ssm_in, prec, 0, bc, lc, conv_w, conv_b, dtb, na, dsk, ng, zero_ssm)
        o_ssm_l, _ = _ssd(ssm_in, prec, tc, bl, ll, conv_w, conv_b, dtb, na, dsk, ng,
                          _ssm_state_pack(state_ssm[:, i]))
        new_ssm.append(_ssm_state_unpack(s_ssm))

        lb = lb_all[:, i]
        llb = jnp.log(lb).reshape(1, 2 * BR_W)
        oml = (1.0 - lb).reshape(1, 2 * BR_W)
        l1m = jnp.log1p(-lb).reshape(1, 2 * BR_W)
        hn = hgrn_norm[i][None, :]
        zero_hg = jnp.zeros((bc, 2, H_C, DV_C, E_C), F32)
        o_hg_c, s_hg = _hgrn(hg_in, prec, 0, bc, lc, llb, oml, l1m, hn, zero_hg)
        o_hg_l, _ = _hgrn(hg_in, prec, tc, bl, ll, llb, oml, l1m, hn,
                          jnp.swapaxes(state_hgrn[:, i], -1, -2))
        new_hg.append(jnp.swapaxes(s_hg, -1, -2))

        o_ret = jnp.concatenate([o_ret_c, o_ret_l], axis=0)
        o_ssm = jnp.concatenate([o_ssm_c, o_ssm_l], axis=0)
        o_hg = jnp.concatenate([o_hg_c, o_hg_l], axis=0)
        x_new, h2, hs = _merge(x_all, mod, norm2[i], o_ret, o_ssm, o_hg, gl,
                               w_branch[i].astype(BF16), w_out[i].astype(BF16), mod_index)

        idx8, w8 = _router(x_new, mod, norm2[i], router_w[i].T, router_b[i][:, None], mod_index)
        be, nv, tok, dst, n_out_rows = _moe_plan(idx8, t)
        y = _experts(hs, be, nv, tok, dst, exp_w_gate[i], exp_w_up[i], exp_w_down[i], n_out_rows)
        wsgu = jnp.concatenate([sh_w_gate[i], sh_w_up[i]], axis=-1).astype(BF16)
        x_all = _combine(i == DEPTH - 1, x_new, h2, y, w8.T, mod, final_norm, wsgu,
                         sh_w_down[i].astype(BF16), mod_index)

    y_prompt = x_all[:tc].reshape(bc, lc, d)
    y_sample = x_all[tc:].reshape(bl, ll, d)
    return (y_prompt, y_sample, jnp.stack(new_ret, axis=1), jnp.stack(new_ssm, axis=1),
            jnp.stack(new_hg, axis=1))
```

```python
import functools

import numpy as np
import jax
import jax.numpy as jnp
from jax import lax
from jax.experimental import pallas as pl
from jax.experimental.pallas import tpu as pltpu

F32 = jnp.float32
BF16 = jnp.bfloat16
HIGHEST = lax.Precision.HIGHEST

D_MODEL = 1024
DEPTH = 2
GRID_W = 64
H_R, DK_R, DV_R = 4, 64, 128
H_M, P_M, N_M, G_M, D_CONV = 8, 64, 64, 2, 5
H_C, E_C, DV_C = 4, 128, 128
BR_W = 512
N_BRANCH = 3
RET_QK_W = H_R * DK_R
CONV_CH = BR_W + 2 * G_M * N_M
SSM_DT_W = 2 * H_M
IN_SPLITS = (RET_QK_W, RET_QK_W, BR_W, BR_W, BR_W, CONV_CH, SSM_DT_W, BR_W, 2 * BR_W, BR_W, BR_W,
             N_BRANCH * D_MODEL)
N_EXPERTS, TOP_K, N_GROUPS, TOPK_GROUPS = 64, 6, 8, 4
D_EXPERT = 256
D_SHARED = 256
ROUTED_SCALE = 2.5
EPS = 1e-6
ROPE_BASE = 10000.0

LANES = 128
SUBLANES = 8
SLAB = D_MODEL // LANES
CHUNK = 128
ROW_TILE = 256
MOE_ROWS = 256
VMEM_LIMIT = 56 * 1024 * 1024
NEG_BIG = -1e30

_CP = functools.partial(pltpu.CompilerParams, vmem_limit_bytes=VMEM_LIMIT)


def _sigmoid(x):
    return 1.0 / (1.0 + jnp.exp(-x))


def _silu(x):
    return x * _sigmoid(x)


def _softplus(x):
    return jnp.maximum(x, 0.0) + jnp.log1p(jnp.exp(-jnp.abs(x)))


def _log_sigmoid(x):
    return jnp.minimum(x, 0.0) - jnp.log1p(jnp.exp(-jnp.abs(x)))


def _dot(a, b):
    return jnp.dot(a, b, preferred_element_type=F32)


def _dot_nt(a, b):
    return lax.dot_general(a, b, (((1,), (1,)), ((), ())), preferred_element_type=F32)


def _dot_tn(a, b):
    return lax.dot_general(a, b, (((0,), (0,)), ((), ())), preferred_element_type=F32)


def _tri_dot(tri, x):
    hi = x.astype(BF16)
    r1 = x - hi.astype(F32)
    mid = r1.astype(BF16)
    lo = (r1 - mid.astype(F32)).astype(BF16)
    return _dot(tri, hi) + _dot(tri, mid) + _dot(tri, lo)


def _iota2(shape, dim):
    return lax.broadcasted_iota(jnp.int32, shape, dim)


def _rows(c):
    return pl.ds(pl.multiple_of(c * CHUNK, CHUNK), CHUNK)


def _const_spec(shape):
    nd = len(shape)
    return pl.BlockSpec(shape, lambda *_: (0,) * nd)


def _ada_kernel(c_ref, w_ref, b_ref, o_ref):
    a = _silu(c_ref[...])
    o_ref[0] = jnp.dot(a, w_ref[0], preferred_element_type=F32, precision=HIGHEST) + b_ref[0]


def _ada(c_all, ada_w, ada_b):
    depth, d, n = ada_w.shape
    tn = 1536
    return pl.pallas_call(
        _ada_kernel,
        grid=(depth, n // tn),
        in_specs=[pl.BlockSpec((SUBLANES, d), lambda i, j: (0, 0)),
                  pl.BlockSpec((1, d, tn), lambda i, j: (i, 0, j)),
                  pl.BlockSpec((1, 1, tn), lambda i, j: (i, 0, j))],
        out_specs=pl.BlockSpec((1, SUBLANES, tn), lambda i, j: (i, 0, j)),
        out_shape=jax.ShapeDtypeStruct((depth, SUBLANES, n), F32),
        compiler_params=_CP(dimension_semantics=("arbitrary", "arbitrary")),
        name="ada_mod",
    )(c_all, ada_w, ada_b.reshape(depth, 1, n))


def _mod_index(n_ctx_tiles, tiles_per_seq):
    def index(r):
        return (jnp.where(r < n_ctx_tiles, 0, 1 + (r - n_ctx_tiles) // tiles_per_seq), 0, 0)
    return index


def _modulated_norm(x, g, shift, scale):
    y = x * lax.rsqrt(jnp.mean(x * x, axis=-1, keepdims=True) + EPS) * g
    return y * (1.0 + scale) + shift


def _inproj_kernel(x_ref, mod_ref, n_ref, w0, w1, w2, w3, w4, o0, o1, o2, o3, o4):
    h = _modulated_norm(x_ref[...], n_ref[...], mod_ref[0, 0:1, :], mod_ref[0, 1:2, :]).astype(BF16)
    for w, o in ((w0, o0), (w1, o1), (w2, o2), (w3, o3), (w4, o4)):
        o[...] = _dot(h, w[...]).astype(o.dtype)


def _inproj(x_all, mod, norm_g, weights, out_dtypes, mod_index):
    t, d = x_all.shape
    w_specs = [pl.BlockSpec(w.shape, lambda r: (0, 0), pipeline_mode=pl.Buffered(1)) for w in weights]
    return pl.pallas_call(
        _inproj_kernel,
        grid=(t // ROW_TILE,),
        in_specs=[pl.BlockSpec((ROW_TILE, d), lambda r: (r, 0)),
                  pl.BlockSpec((1, 6, d), mod_index),
                  pl.BlockSpec((1, d), lambda r: (0, 0))] + w_specs,
        out_specs=[pl.BlockSpec((ROW_TILE, w.shape[1]), lambda r: (r, 0)) for w in weights],
        out_shape=[jax.ShapeDtypeStruct((t, w.shape[1]), dt) for w, dt in zip(weights, out_dtypes)],
        compiler_params=_CP(dimension_semantics=("arbitrary",)),
        name="norm_inproj",
    )(x_all, mod, norm_g.reshape(1, d), *weights)


def _ret_kernel(use_rope, n_chunks, in_ref, cos_ref, sin_ref, lgl_ref, lgv_ref, gn_ref, s0_ref,
                o_ref, sfin_ref, qk_s, oacc, st, dm_s):
    qw = 2 * LANES
    ii = _iota2((CHUNK, CHUNK), 0)
    jj = _iota2((CHUNK, CHUNK), 1)
    dist = jnp.abs(ii - jj).astype(F32)
    for h in range(H_R):
        dm_s[h] = (jnp.where(ii >= jj, jnp.exp(dist * lgv_ref[0, h:h + 1, :]), 0.0)
                   + jnp.where(jj >= ii, jnp.exp(dist * lgv_ref[1, h:h + 1, :]), 0.0))
    st[...] = s0_ref[0]

    lane_head = (_iota2((1, qw), 1) % LANES) // (DK_R // 2)
    rr = _iota2((CHUNK, qw), 0).astype(F32)
    lg_f = lgl_ref[0:1, :]
    lg_b = lgl_ref[1:2, :]

    def rope(x, cs, sn):
        x1, x2 = x[:, :LANES], x[:, LANES:]
        return jnp.concatenate([x1 * cs - x2 * sn, x1 * sn + x2 * cs], axis=1)

    def fwd(c, carry):
        rows = _rows(c)
        q = in_ref[rows, 0:qw].astype(F32)
        k = in_ref[rows, qw:2 * qw].astype(F32)
        if use_rope:
            cs, sn = cos_ref[rows, :], sin_ref[rows, :]
            q, k = rope(q, cs, sn), rope(k, cs, sn)
        qk_s[rows, 0:qw] = q
        qk_s[rows, qw:2 * qw] = k
        kb = k.astype(BF16)
        q_dec = (q * jnp.exp((rr + 1.0) * lg_f)).astype(BF16)
        k_dec = k * jnp.exp((CHUNK - 1.0 - rr) * lg_f)
        for h in range(H_R):
            hs = slice(h * DV_R, (h + 1) * DV_R)
            mh = lane_head == h
            vh = in_ref[rows, 2 * qw + h * DV_R:2 * qw + (h + 1) * DV_R]
            s = _dot_nt(jnp.where(mh, q, 0.0).astype(BF16), kb)
            intra = _dot((s * dm_s[h]).astype(BF16), vh)
            sf = st[0, h]
            oacc[rows, hs] = intra + _dot(q_dec, sf.astype(BF16))
            st[0, h] = (sf * jnp.exp(CHUNK * lgv_ref[0, h:h + 1, :])
                        + _dot_tn(jnp.where(mh, k_dec, 0.0).astype(BF16), vh))
        return carry

    lax.fori_loop(0, n_chunks, fwd, 0)

    def bwd(t, carry):
        rows = _rows(n_chunks - 1 - t)
        q = qk_s[rows, 0:qw]
        k = qk_s[rows, qw:2 * qw]
        q_dec = (q * jnp.exp((CHUNK - rr) * lg_b)).astype(BF16)
        k_dec = k * jnp.exp(rr * lg_b)
        for h in range(H_R):
            hs = slice(h * DV_R, (h + 1) * DV_R)
            mh = lane_head == h
            vh = in_ref[rows, 2 * qw + h * DV_R:2 * qw + (h + 1) * DV_R]
            sb = st[1, h]
            oacc[rows, hs] = oacc[rows, hs] + _dot(q_dec, sb.astype(BF16))
            st[1, h] = (sb * jnp.exp(CHUNK * lgv_ref[1, h:h + 1, :])
                        + _dot_tn(jnp.where(mh, k_dec, 0.0).astype(BF16), vh))
        return carry

    lax.fori_loop(0, n_chunks, bwd, 0)
    sfin_ref[0] = st[...]

    def fin(c, carry):
        rows = _rows(c)
        for h in range(H_R):
            hs = slice(h * DV_R, (h + 1) * DV_R)
            o = oacc[rows, hs]
            oc = o - jnp.mean(o, axis=-1, keepdims=True)
            y = oc * lax.rsqrt(jnp.mean(oc * oc, axis=-1, keepdims=True) + EPS) * gn_ref[:, hs]
            g = in_ref[rows, 4 * qw + h * DV_R:4 * qw + (h + 1) * DV_R].astype(F32)
            o_ref[rows, hs] = (y * _silu(g)).astype(o_ref.dtype)
        return carry

    lax.fori_loop(0, n_chunks, fin, 0)


def _retention(ret_in, row0, b, l, cos4, sin4, lgl, lgv, gn, s0):
    use_rope = cos4 is not None
    if not use_rope:
        cos4 = jnp.zeros((SUBLANES, LANES), F32)
        sin4 = cos4
        trig_spec = pl.BlockSpec((SUBLANES, LANES), lambda i: (0, 0))
    else:
        trig_spec = pl.BlockSpec((l, LANES), lambda i: (0, 0))
    blk0 = row0 // l
    n_chunks = l // CHUNK
    sshape = (1, 2, H_R, 2 * LANES, DV_R)
    return pl.pallas_call(
        functools.partial(_ret_kernel, use_rope, n_chunks),
        grid=(b,),
        in_specs=[pl.BlockSpec((l, ret_in.shape[1]), lambda i: (blk0 + i, 0)),
                  trig_spec, trig_spec,
                  _const_spec(lgl.shape), _const_spec(lgv.shape), _const_spec(gn.shape),
                  pl.BlockSpec(sshape, lambda i: (i, 0, 0, 0, 0))],
        out_specs=[pl.BlockSpec((l, BR_W), lambda i: (i, 0)),
                   pl.BlockSpec(sshape, lambda i: (i, 0, 0, 0, 0))],
        out_shape=[jax.ShapeDtypeStruct((b * l, BR_W), BF16),
                   jax.ShapeDtypeStruct((b,) + sshape[1:], F32)],
        scratch_shapes=[pltpu.VMEM((l, 4 * LANES), F32),
                        pltpu.VMEM((l, BR_W), F32),
                        pltpu.VMEM(sshape[1:], F32),
                        pltpu.VMEM((H_R, CHUNK, CHUNK), F32)],
        compiler_params=_CP(dimension_semantics=("arbitrary",)),
        name="retention",
    )(ret_in, cos4, sin4, lgl, lgv, gn, s0)


def _ssd_kernel(n_chunks, in_ref, dt_ref, cw_ref, cb_ref, dtb_ref, na_ref, dsk_ref, ng_ref, s0_ref,
                o_ref, sfin_ref, pad_s, xc_s, y_s, dt_s, cumb_s, st):
    l = in_ref.shape[0]
    cw = 2 * BR_W
    halo = SUBLANES
    pad_s[0:halo, :] = jnp.zeros((halo, cw), F32)
    pad_s[l + halo:l + 2 * halo, :] = jnp.zeros((halo, cw), F32)

    def fill(c, carry):
        rows = _rows(c)
        dst = pl.ds(pl.multiple_of(c * CHUNK + halo, SUBLANES), CHUNK)
        pad_s[dst, :] = in_ref[rows, BR_W:BR_W + cw].astype(F32)
        return carry

    lax.fori_loop(0, n_chunks, fill, 0)
    st[...] = s0_ref[0]

    ii = _iota2((CHUNK, CHUNK), 0)
    jj = _iota2((CHUNK, CHUNK), 1)
    tril = jnp.where(ii >= jj, 1.0, 0.0).astype(BF16)
    triu = jnp.where(jj >= ii, 1.0, 0.0).astype(BF16)
    lane_lo = jj < P_M
    row_lo = ii < N_M
    blockdiag = lane_lo == row_lo
    half = D_CONV // 2

    def pair_cols(vals, h0, h1):
        return jnp.where(lane_lo, vals[:, h0:h0 + 1], vals[:, h1:h1 + 1])

    def fwd(c, carry):
        rows = _rows(c)
        win = pad_s[pl.ds(pl.multiple_of(c * CHUNK, CHUNK), CHUNK + 2 * halo), :]
        acc = win[halo - half:halo - half + CHUNK, :] * cw_ref[0:1, :] + cb_ref[...]
        for w in range(1, D_CONV):
            acc = acc + win[halo - half + w:halo - half + w + CHUNK, :] * cw_ref[w:w + 1, :]
        xc = _silu(acc)
        xc_s[rows, :] = xc.astype(xc_s.dtype)
        x = xc[:, 0:BR_W]
        dt = _softplus(dt_ref[rows, :] + dtb_ref[...])
        la = dt * na_ref[...]
        cumf = _tri_dot(tril, la)
        cumb = _tri_dot(triu, la)
        dt_s[rows, :] = dt
        cumb_s[rows, :] = cumb
        cumf_t, cumb_t, dt_t = cumf.T, cumb.T, dt.T
        totf = cumf[CHUNK - 1:CHUNK, :]
        etotf = jnp.exp(totf)
        kdec = dt * jnp.exp(totf - cumf)
        qdec = jnp.exp(cumf)
        for g in range(G_M):
            b2 = xc[:, BR_W + g * LANES:BR_W + (g + 1) * LANES]
            c2 = xc[:, BR_W + (G_M + g) * LANES:BR_W + (G_M + g + 1) * LANES]
            cbm = _dot_nt(jnp.where(lane_lo, c2, 0.0).astype(BF16), b2.astype(BF16))
            for pp in range(2):
                p = 2 * g + pp
                h0, h1 = 2 * p, 2 * p + 1
                ps = slice(p * LANES, (p + 1) * LANES)
                ms = []
                for h in (h0, h1):
                    hb = H_M + h
                    mf = jnp.where(ii >= jj, jnp.exp(cumf[:, h:h + 1] - cumf_t[h:h + 1, :]), 0.0) * dt_t[h:h + 1, :]
                    mb = (jnp.where(jj >= ii, jnp.exp(cumb[:, hb:hb + 1] - cumb_t[hb:hb + 1, :]), 0.0)
                          * dt_t[hb:hb + 1, :])
                    ms.append((cbm * (mf + mb)).astype(BF16))
                xp = x[:, ps]
                xbd = jnp.concatenate([jnp.where(lane_lo, xp, 0.0), jnp.where(lane_lo, 0.0, xp)], axis=0)
                intra = _dot(jnp.concatenate(ms, axis=1), xbd.astype(BF16))
                sf = st[0, p]
                inter = _dot((c2 * pair_cols(qdec, h0, h1)).astype(BF16), sf.astype(BF16))
                y_s[rows, ps] = intra + inter
                kv = _dot_tn((b2 * pair_cols(kdec, h0, h1)).astype(BF16), xp.astype(BF16))
                arow = jnp.where(row_lo, etotf[:, h0:h0 + 1], etotf[:, h1:h1 + 1])
                st[0, p] = sf * arow + jnp.where(blockdiag, kv, 0.0)
        return carry

    lax.fori_loop(0, n_chunks, fwd, 0)

    def bwd(t, carry):
        rows = _rows(n_chunks - 1 - t)
        xc = xc_s[rows, :].astype(F32)
        x = xc[:, 0:BR_W]
        dt = dt_s[rows, :]
        cumb = cumb_s[rows, :]
        totb = cumb[0:1, :]
        etotb = jnp.exp(totb)
        kdec = dt * jnp.exp(totb - cumb)
        qdec = jnp.exp(cumb)
        for g in range(G_M):
            b2 = xc[:, BR_W + g * LANES:BR_W + (g + 1) * LANES]
            c2 = xc[:, BR_W + (G_M + g) * LANES:BR_W + (G_M + g + 1) * LANES]
            for pp in range(2):
                p = 2 * g + pp
                h0, h1 = H_M + 2 * p, H_M + 2 * p + 1
                ps = slice(p * LANES, (p + 1) * LANES)
                xp = x[:, ps]
                sb = st[1, p]
                y_s[rows, ps] = y_s[rows, ps] + _dot((c2 * pair_cols(qdec, h0, h1)).astype(BF16), sb.astype(BF16))
                kv = _dot_tn((b2 * pair_cols(kdec, h0, h1)).astype(BF16), xp.astype(BF16))
                arow = jnp.where(row_lo, etotb[:, h0:h0 + 1], etotb[:, h1:h1 + 1])
                st[1, p] = sb * arow + jnp.where(blockdiag, kv, 0.0)
        return carry

    lax.fori_loop(0, n_chunks, bwd, 0)
    sfin_ref[0] = st[...]

    def fin(c, carry):
        rows = _rows(c)
        x = xc_s[rows, 0:BR_W].astype(F32)
        z = in_ref[rows, 0:BR_W].astype(F32)
        u = (y_s[rows, :] + x * dsk_ref[...]) * _silu(z)
        y = u * lax.rsqrt(jnp.mean(u * u, axis=-1, keepdims=True) + EPS) * ng_ref[...]
        o_ref[rows, :] = y.astype(o_ref.dtype)
        return carry

    lax.fori_loop(0, n_chunks, fin, 0)


def _ssd(ssm_in, prec, row0, b, l, cw, cb, dtb, na, dsk, ng, s0):
    blk0 = row0 // l
    n_chunks = l // CHUNK
    dt_col = (prec.shape[1] - LANES) // LANES
    sshape = (1, 2, H_M // 2, LANES, LANES)
    return pl.pallas_call(
        functools.partial(_ssd_kernel, n_chunks),
        grid=(b,),
        in_specs=[pl.BlockSpec((l, ssm_in.shape[1]), lambda i: (blk0 + i, 0)),
                  pl.BlockSpec((l, LANES), lambda i: (blk0 + i, dt_col)),
                  _const_spec(cw.shape), _const_spec(cb.shape), _const_spec(dtb.shape),
                  _const_spec(na.shape), _const_spec(dsk.shape), _const_spec(ng.shape),
                  pl.BlockSpec(sshape, lambda i: (i, 0, 0, 0, 0))],
        out_specs=[pl.BlockSpec((l, BR_W), lambda i: (i, 0)),
                   pl.BlockSpec(sshape, lambda i: (i, 0, 0, 0, 0))],
        out_shape=[jax.ShapeDtypeStruct((b * l, BR_W), BF16),
                   jax.ShapeDtypeStruct((b,) + sshape[1:], F32)],
        scratch_shapes=[pltpu.VMEM((l + 2 * SUBLANES, 2 * BR_W), F32),
                        pltpu.VMEM((l, 2 * BR_W), BF16),
                        pltpu.VMEM((l, BR_W), F32),
                        pltpu.VMEM((l, LANES), F32),
                        pltpu.VMEM((l, LANES), F32),
                        pltpu.VMEM(sshape[1:], F32)],
        compiler_params=_CP(dimension_semantics=("arbitrary",)),
        name="ssd",
    )(ssm_in, prec, cw, cb, dtb, na, dsk, ng, s0)


_HG_LEVELS = (64, 32, 16, 8, 4, 2)
LOG2_E = 1.4426950408889634


def _block_ref_rows(x, m, row):
    size = 2 * m
    if size >= 2 * SUBLANES:
        return _bcast_block_row(x, size, row)
    out = _bcast_group_row(x, row)
    sub = _iota2((CHUNK, LANES), 0) % SUBLANES
    for b in range(1, SUBLANES // size):
        out = jnp.where(sub >= b * size, _bcast_group_row(x, b * size + row), out)
    return out


def _bcast_group_row(x, j):
    x3 = x.reshape(CHUNK // SUBLANES, SUBLANES, LANES)
    r = jnp.broadcast_to(x3[:, j:j + 1, :], x3.shape)
    return r.reshape(CHUNK, LANES)


def _bcast_block_row(x, size, j):
    pieces = [jnp.broadcast_to(x[b * size + j:b * size + j + 1, :], (size, LANES)) for b in range(CHUNK // size)]
    return pieces[0] if len(pieces) == 1 else jnp.concatenate(pieces, axis=0)


def _hgrn_kernel(n_chunks, in_ref, f_ref, llb_ref, oml_ref, l1m_ref, ng_ref, s0_ref,
                 o_ref, sfin_ref, oacc, st):
    ii = _iota2((CHUNK, CHUNK), 0)
    jj = _iota2((CHUNK, CHUNK), 1)
    tril = jnp.where(ii >= jj, 1.0, 0.0).astype(BF16)
    triu = jnp.where(jj >= ii, 1.0, 0.0).astype(BF16)
    odd = (ii % 2) == 1
    same_pair = (ii // 2) == (jj // 2)
    ones_sq = jnp.ones((LANES, CHUNK), BF16)
    st[...] = s0_ref[0]

    def gates(rows, d):
        fr = f_ref[rows, d * BR_W:(d + 1) * BR_W]
        ds_ = slice(d * BR_W, (d + 1) * BR_W)
        a = llb_ref[:, ds_]
        bterm = l1m_ref[:, ds_] + _log_sigmoid(fr)
        logf = jnp.maximum(a, bterm) + jnp.log1p(jnp.exp(-jnp.abs(a - bterm)))
        key = oml_ref[:, ds_] * _sigmoid(-fr)
        return logf * LOG2_E, key

    def fwd(c, carry):
        rows = _rows(c)
        lgf_all, key_f = gates(rows, 0)
        lgb_all, key_b = gates(rows, 1)
        cumf_all = _tri_dot(tril, lgf_all)
        cumb_all = _tri_dot(triu, lgb_all)
        for h in range(H_C):
            hs = slice(h * LANES, (h + 1) * LANES)
            q = in_ref[rows, hs].astype(F32)
            v = in_ref[rows, BR_W + h * LANES:BR_W + (h + 1) * LANES]
            cumf, cumb = cumf_all[:, hs], cumb_all[:, hs]
            kf, kb = key_f[:, hs], key_b[:, hs]
            sc = jnp.zeros((CHUNK, CHUNK), F32)
            for m in _HG_LEVELS:
                upper = (ii % (2 * m)) >= m
                same_block = (ii // (2 * m)) == (jj // (2 * m))
                ref_f = _block_ref_rows(cumf, m, m - 1)
                ref_b = _block_ref_rows(cumb, m, m)
                e_f = jnp.exp2(jnp.where(upper, cumf - ref_f, ref_f - cumf))
                e_b = jnp.exp2(jnp.where(upper, ref_b - cumb, cumb - ref_b))
                qcat = jnp.concatenate([jnp.where(upper, q * e_f, 0.0), jnp.where(upper, 0.0, q * e_b)], axis=1)
                kcat = jnp.concatenate([jnp.where(upper, 0.0, kf * e_f), jnp.where(upper, kb * e_b, 0.0)], axis=1)
                sc = sc + jnp.where(same_block, _dot_nt(qcat.astype(BF16), kcat.astype(BF16)), 0.0)
            qcat = jnp.concatenate([jnp.where(odd, q * jnp.exp2(lgf_all[:, hs]), 0.0),
                                    jnp.where(odd, 0.0, q * jnp.exp2(lgb_all[:, hs]))], axis=1)
            kcat = jnp.concatenate([jnp.where(odd, 0.0, kf), jnp.where(odd, kb, 0.0)], axis=1)
            sc = sc + jnp.where(same_pair, _dot_nt(qcat.astype(BF16), kcat.astype(BF16)), 0.0)
            sc = sc + jnp.where(ii == jj, _dot((q * (kf + kb)).astype(BF16), ones_sq), 0.0)
            intra = _dot(sc.astype(BF16), v)
            stf = st[0, h]
            totf = cumf[CHUNK - 1:CHUNK, :]
            inter = _dot_nt((q * jnp.exp2(cumf)).astype(BF16), stf.astype(BF16))
            oacc[rows, hs] = intra + inter
            st[0, h] = stf * jnp.exp2(totf) + _dot_tn(v, (kf * jnp.exp2(totf - cumf)).astype(BF16))
        return carry

    lax.fori_loop(0, n_chunks, fwd, 0)

    def bwd(t, carry):
        rows = _rows(n_chunks - 1 - t)
        lgb_all, key_b = gates(rows, 1)
        cumb_all = _tri_dot(triu, lgb_all)
        for h in range(H_C):
            hs = slice(h * LANES, (h + 1) * LANES)
            q = in_ref[rows, hs].astype(F32)
            v = in_ref[rows, BR_W + h * LANES:BR_W + (h + 1) * LANES]
            cumb, kb = cumb_all[:, hs], key_b[:, hs]
            stb = st[1, h]
            totb = cumb[0:1, :]
            oacc[rows, hs] = oacc[rows, hs] + _dot_nt((q * jnp.exp2(cumb)).astype(BF16), stb.astype(BF16))
            st[1, h] = stb * jnp.exp2(totb) + _dot_tn(v, (kb * jnp.exp2(totb - cumb)).astype(BF16))
        return carry

    lax.fori_loop(0, n_chunks, bwd, 0)
    sfin_ref[0] = st[...]

    def fin(c, carry):
        rows = _rows(c)
        for h in range(H_C):
            hs = slice(h * LANES, (h + 1) * LANES)
            o = oacc[rows, hs]
            y = o * lax.rsqrt(jnp.mean(o * o, axis=-1, keepdims=True) + EPS) * ng_ref[:, hs]
            g = in_ref[rows, 2 * BR_W + h * LANES:2 * BR_W + (h + 1) * LANES].astype(F32)
            o_ref[rows, hs] = (y * _silu(g)).astype(o_ref.dtype)
        return carry

    lax.fori_loop(0, n_chunks, fin, 0)


def _hgrn(hg_in, prec, row0, b, l, llb, oml, l1m, ng, s0):
    blk0 = row0 // l
    n_chunks = l // CHUNK
    sshape = (1, 2, H_C, DV_C, E_C)
    return pl.pallas_call(
        functools.partial(_hgrn_kernel, n_chunks),
        grid=(b,),
        in_specs=[pl.BlockSpec((l, hg_in.shape[1]), lambda i: (blk0 + i, 0)),
                  pl.BlockSpec((l, 2 * BR_W), lambda i: (blk0 + i, 0)),
                  _const_spec(llb.shape), _const_spec(oml.shape), _const_spec(l1m.shape),
                  _const_spec(ng.shape),
                  pl.BlockSpec(sshape, lambda i: (i, 0, 0, 0, 0))],
        out_specs=[pl.BlockSpec((l, BR_W), lambda i: (i, 0)),
                   pl.BlockSpec(sshape, lambda i: (i, 0, 0, 0, 0))],
        out_shape=[jax.ShapeDtypeStruct((b * l, BR_W), BF16),
                   jax.ShapeDtypeStruct((b,) + sshape[1:], F32)],
        scratch_shapes=[pltpu.VMEM((l, BR_W), F32),
                        pltpu.VMEM(sshape[1:], F32)],
        compiler_params=_CP(dimension_semantics=("arbitrary",)),
        name="hgrn2",
    )(hg_in, prec, llb, oml, l1m, ng, s0)


def _merge_kernel(x_ref, mod_ref, n2_ref, o0, o1, o2, gl_ref, wb_ref, wo_ref, xo_ref, h2_ref):
    d = x_ref.shape[1]
    merged = jnp.zeros(x_ref.shape, F32)
    for k, o in enumerate((o0, o1, o2)):
        gate = _sigmoid(gl_ref[:, k * d:(k + 1) * d].astype(F32))
        merged = merged + gate * _dot(o[...], wb_ref[k])
    mix = _dot(merged.astype(BF16), wo_ref[...])
    xn = x_ref[...] + mod_ref[0, 2:3, :] * mix
    xo_ref[...] = xn
    h2 = _modulated_norm(xn, n2_ref[...], mod_ref[0, 3:4, :], mod_ref[0, 4:5, :])
    h2_ref[...] = h2.astype(h2_ref.dtype)


def _merge(x_all, mod, norm_g, o_ret, o_ssm, o_hg, gl, wb, wo, mod_index):
    t, d = x_all.shape
    row = lambda r: (r, 0)
    return pl.pallas_call(
        _merge_kernel,
        grid=(t // ROW_TILE,),
        in_specs=[pl.BlockSpec((ROW_TILE, d), row),
                  pl.BlockSpec((1, 6, d), mod_index),
                  pl.BlockSpec((1, d), lambda r: (0, 0)),
                  pl.BlockSpec((ROW_TILE, BR_W), row),
                  pl.BlockSpec((ROW_TILE, BR_W), row),
                  pl.BlockSpec((ROW_TILE, BR_W), row),
                  pl.BlockSpec((ROW_TILE, N_BRANCH * d), row),
                  pl.BlockSpec(wb.shape, lambda r: (0, 0, 0), pipeline_mode=pl.Buffered(1)),
                  pl.BlockSpec(wo.shape, lambda r: (0, 0), pipeline_mode=pl.Buffered(1))],
        out_specs=[pl.BlockSpec((ROW_TILE, d), row),
                   pl.BlockSpec((ROW_TILE, d), row)],
        out_shape=[jax.ShapeDtypeStruct((t, d), F32),
                   jax.ShapeDtypeStruct((t, d), BF16)],
        compiler_params=_CP(dimension_semantics=("arbitrary",)),
        name="merge_outproj",
    )(x_all, mod, norm_g.reshape(1, d), o_ret, o_ssm, o_hg, gl, wb, wo)


def _router_kernel(x_ref, mod_ref, n2_ref, rwt_ref, rb_ref, rloc_ref, w_ref, cnt_ref):
    tm = x_ref.shape[0]
    h2 = _modulated_norm(x_ref[...], n2_ref[...], mod_ref[0, 3:4, :], mod_ref[0, 4:5, :])
    logits = lax.dot_general(rwt_ref[...], h2, (((1,), (1,)), ((), ())),
                             preferred_element_type=F32, precision=HIGHEST)
    scores = _sigmoid(logits)
    biased = scores + rb_ref[...]
    gsz = N_EXPERTS // N_GROUPS
    neg_inf = -jnp.inf

    b3 = biased.reshape(N_GROUPS, gsz, tm)
    e_in_g = lax.broadcasted_iota(jnp.int32, (N_GROUPS, gsz, tm), 1)
    m1 = jnp.max(b3, axis=1, keepdims=True)
    first = jnp.min(jnp.where(b3 == m1, e_in_g, gsz), axis=1, keepdims=True)
    m2 = jnp.max(jnp.where(e_in_g == first, neg_inf, b3), axis=1, keepdims=True)
    gscore = m1 + m2

    g_iota = lax.broadcasted_iota(jnp.int32, (N_GROUPS, 1, tm), 0)
    chosen = jnp.zeros((N_GROUPS, 1, tm), jnp.int32)
    for _ in range(TOPK_GROUPS):
        m = jnp.max(gscore, axis=0, keepdims=True)
        first = jnp.min(jnp.where(gscore == m, g_iota, N_GROUPS), axis=0, keepdims=True)
        hit = g_iota == first
        chosen = jnp.where(hit, 1, chosen)
        gscore = jnp.where(hit, neg_inf, gscore)
    emask = jnp.broadcast_to(chosen, (N_GROUPS, gsz, tm)).reshape(N_EXPERTS, tm)

    cand = jnp.where(emask > 0, biased, neg_inf)
    e_iota = _iota2((N_EXPERTS, tm), 0)
    hits, ws = [], []
    for _ in range(TOP_K):
        m = jnp.max(cand, axis=0, keepdims=True)
        first = jnp.min(jnp.where(cand == m, e_iota, N_EXPERTS), axis=0, keepdims=True)
        hit = e_iota == first
        hits.append(hit)
        ws.append(jnp.sum(jnp.where(hit, scores, 0.0), axis=0, keepdims=True))
        cand = jnp.where(hit, neg_inf, cand)
    wsum = ws[0]
    for w in ws[1:]:
        wsum = wsum + w
    pad = SUBLANES - TOP_K
    w_ref[...] = jnp.concatenate([ROUTED_SCALE * w / wsum for w in ws] + [jnp.zeros((pad, tm), F32)], axis=0)

    picked = jnp.zeros((N_EXPERTS, tm), F32)
    for hit in hits:
        picked = jnp.where(hit, 1.0, picked)
    picked = picked.astype(BF16)
    earlier_tok = jnp.where(_iota2((tm, tm), 0) < _iota2((tm, tm), 1), 1.0, 0.0).astype(BF16)
    before_in_expert = _dot(picked, earlier_tok)
    count_rep = _dot(picked, jnp.ones((tm, tm), BF16))
    lower_expert = jnp.where(_iota2((N_EXPERTS, N_EXPERTS), 1) < _iota2((N_EXPERTS, N_EXPERTS), 0), 1.0, 0.0)
    pos = _dot(lower_expert.astype(BF16), count_rep.astype(BF16)) + before_in_expert
    rloc = [jnp.sum(jnp.where(hit, pos, 0.0), axis=0, keepdims=True).astype(jnp.int32) for hit in hits]
    rloc_ref[...] = jnp.concatenate(rloc + [jnp.zeros((pad, tm), jnp.int32)], axis=0)
    cnt_ref[0] = _dot_nt(jnp.ones((SUBLANES, tm), BF16), picked)


def _router(x_new, mod, norm_g, rwt, rb, mod_index):
    t, d = x_new.shape
    n_tiles = t // ROW_TILE
    return pl.pallas_call(
        _router_kernel,
        grid=(n_tiles,),
        in_specs=[pl.BlockSpec((ROW_TILE, d), lambda r: (r, 0)),
                  pl.BlockSpec((1, 6, d), mod_index),
                  pl.BlockSpec((1, d), lambda r: (0, 0)),
                  pl.BlockSpec(rwt.shape, lambda r: (0, 0)),
                  pl.BlockSpec(rb.shape, lambda r: (0, 0))],
        out_specs=[pl.BlockSpec((SUBLANES, ROW_TILE), lambda r: (0, r)),
                   pl.BlockSpec((SUBLANES, ROW_TILE), lambda r: (0, r)),
                   pl.BlockSpec((1, SUBLANES, N_EXPERTS), lambda r: (r, 0, 0))],
        out_shape=[jax.ShapeDtypeStruct((SUBLANES, t), jnp.int32),
                   jax.ShapeDtypeStruct((SUBLANES, t), F32),
                   jax.ShapeDtypeStruct((n_tiles, SUBLANES, N_EXPERTS), F32)],
        compiler_params=_CP(dimension_semantics=("arbitrary",)),
        name="router",
    )(x_new, mod, norm_g.reshape(1, d), rwt, rb)


_SEG_BITS = tuple(1 << s for s in range(ROW_TILE.bit_length() - 1, -1, -1))
TILE_SLOTS = ROW_TILE * TOP_K


def _segment_copies(cnt_ref, lstart_ref, goff_ref, make_copy):
    def per_expert(e, carry):
        n = cnt_ref[0, 0, e]
        loc = lstart_ref[0, 0, e]
        glob = goff_ref[0, 0, e]
        done = jnp.int32(0)
        for bit in _SEG_BITS:
            take = jnp.bitwise_and(n, bit)

            @pl.when(take != 0)
            def _():
                make_copy(pl.ds(pl.multiple_of((loc + done) * SLAB, SLAB), bit * SLAB),
                          pl.ds(pl.multiple_of((glob + done) * SLAB, SLAB), bit * SLAB)).start()

            done = done + take
        return carry

    lax.fori_loop(0, N_EXPERTS, per_expert, 0)


def _dispatch_kernel(h2_ref, rloc_ref, cnt_ref, lstart_ref, goff_ref, xs_hbm, buf, sem):
    i = pl.program_id(0)
    n_tiles = pl.num_programs(0)
    slot = i % 2
    tm = h2_ref.shape[0]

    def all_copies(s):
        return pltpu.make_async_copy(buf.at[s], xs_hbm.at[pl.ds(0, TILE_SLOTS * SLAB)], sem.at[s])

    @pl.when(i >= 2)
    def _():
        all_copies(slot).wait()

    rid = _iota2((TILE_SLOTS, tm), 0)
    perm = jnp.zeros((TILE_SLOTS, tm), F32)
    for k in range(TOP_K):
        perm = jnp.where(rid == rloc_ref[k:k + 1, :], 1.0, perm)
    perm = perm.astype(BF16)
    for s in range(0, SLAB, 2):
        xs = _dot(perm, h2_ref[:, s * LANES:(s + 2) * LANES])
        buf[slot, pl.ds(s, TILE_SLOTS, stride=SLAB), :] = xs[:, :LANES]
        buf[slot, pl.ds(s + 1, TILE_SLOTS, stride=SLAB), :] = xs[:, LANES:]

    _segment_copies(cnt_ref, lstart_ref, goff_ref,
                    lambda loc, glob: pltpu.make_async_copy(buf.at[slot, loc], xs_hbm.at[glob], sem.at[slot]))

    @pl.when(i == n_tiles - 1)
    def _():
        all_copies(slot).wait()

        @pl.when(n_tiles >= 2)
        def _():
            all_copies(1 - slot).wait()


def _dispatch(h2, rloc, cnt, lstart, goff):
    t, d = h2.shape
    n_tiles = t // ROW_TILE
    smem_tile = functools.partial(pl.BlockSpec, (1, 1, N_EXPERTS), lambda i: (i, 0, 0), memory_space=pltpu.SMEM)
    return pl.pallas_call(
        _dispatch_kernel,
        grid=(n_tiles,),
        in_specs=[pl.BlockSpec((ROW_TILE, d), lambda i: (i, 0)),
                  pl.BlockSpec((SUBLANES, ROW_TILE), lambda i: (0, i)),
                  smem_tile(), smem_tile(), smem_tile()],
        out_specs=pl.BlockSpec(memory_space=pl.ANY),
        out_shape=jax.ShapeDtypeStruct((t * TOP_K * SLAB, LANES), F32),
        scratch_shapes=[pltpu.VMEM((2, TILE_SLOTS * SLAB, LANES), F32),
                        pltpu.SemaphoreType.DMA((2,))],
        compiler_params=_CP(dimension_semantics=("arbitrary",)),
        name="moe_dispatch",
    )(h2, rloc, cnt, lstart, goff)


def _expert_kernel(blk_ref, exp_ref, lo_ref, hi_ref, n_ref, x_ref, wg_ref, wu_ref, wdn_ref, y_ref,
                   wgu_s, wd_s):
    i = pl.program_id(0)
    prev = jnp.maximum(i - 1, 0)
    valid = i < n_ref[0]
    new_expert = jnp.logical_or(i == 0, exp_ref[i] != exp_ref[prev])
    first_of_block = jnp.logical_or(i == 0, blk_ref[i] != blk_ref[prev])

    @pl.when(jnp.logical_and(valid, new_expert))
    def _():
        wgu_s[:, :D_EXPERT] = wg_ref[0].astype(BF16)
        wgu_s[:, D_EXPERT:] = wu_ref[0].astype(BF16)
        wd_s[...] = wdn_ref[0].astype(BF16)

    @pl.when(valid)
    def _():
        lhs = jnp.concatenate([x_ref[pl.ds(k, MOE_ROWS, stride=SLAB), :] for k in range(SLAB)], axis=1)
        gu = _dot(lhs.astype(BF16), wgu_s[...])
        act = _silu(gu[:, :D_EXPERT]) * gu[:, D_EXPERT:]
        y = _dot(act.astype(BF16), wd_s[...])
        rows = _iota2((MOE_ROWS, LANES), 0)
        mine = jnp.logical_and(rows >= lo_ref[i], rows < hi_ref[i])

        @pl.when(first_of_block)
        def _():
            for k in range(SLAB):
                y_ref[pl.ds(k, MOE_ROWS, stride=SLAB), :] = jnp.where(mine, y[:, k * LANES:(k + 1) * LANES], 0.0)

        @pl.when(jnp.logical_not(first_of_block))
        def _():
            for k in range(SLAB):
                rows_k = pl.ds(k, MOE_ROWS, stride=SLAB)
                y_ref[rows_k, :] = jnp.where(mine, y[:, k * LANES:(k + 1) * LANES], y_ref[rows_k, :])


def _experts(xs, items, wg, wu, wdn):
    blk, exp, lo, hi, n_items = items
    by_block = lambda i, blk_r, exp_r, lo_r, hi_r, n_r: (blk_r[i], 0)
    by_expert = lambda i, blk_r, exp_r, lo_r, hi_r, n_r: (exp_r[i], 0, 0)
    grid_spec = pltpu.PrefetchScalarGridSpec(
        num_scalar_prefetch=5,
        grid=(blk.shape[0],),
        in_specs=[pl.BlockSpec((MOE_ROWS * SLAB, LANES), by_block),
                  pl.BlockSpec((1,) + wg.shape[1:], by_expert),
                  pl.BlockSpec((1,) + wu.shape[1:], by_expert),
                  pl.BlockSpec((1,) + wdn.shape[1:], by_expert)],
        out_specs=pl.BlockSpec((MOE_ROWS * SLAB, LANES), by_block),
        scratch_shapes=[pltpu.VMEM((wg.shape[1], 2 * D_EXPERT), BF16),
                        pltpu.VMEM(wdn.shape[1:], BF16)])
    return pl.pallas_call(
        _expert_kernel,
        grid_spec=grid_spec,
        out_shape=jax.ShapeDtypeStruct(xs.shape, F32),
        compiler_params=_CP(dimension_semantics=("arbitrary",)),
        name="routed_experts",
    )(blk, exp, lo, hi, n_items, xs, wg, wu, wdn)


def _moe_plan(cnt_tiles, t):
    p = t * TOP_K
    assert p % MOE_ROWS == 0
    cnt = cnt_tiles[:, 0, :].astype(jnp.int32)
    totals = jnp.sum(cnt, axis=0)
    ends = jnp.cumsum(totals)
    starts = ends - totals
    goff = starts[None, :] + jnp.cumsum(cnt, axis=0) - cnt
    lstart = jnp.cumsum(cnt, axis=1) - cnt
    n_blk = jnp.where(totals > 0, (ends - 1) // MOE_ROWS - starts // MOE_ROWS + 1, 0)
    item_end = jnp.cumsum(n_blk)
    n_items = item_end[-1]
    max_items = p // MOE_ROWS + N_EXPERTS
    it = jnp.minimum(jnp.arange(max_items, dtype=jnp.int32), n_items - 1)
    exp = jnp.minimum(jnp.sum((item_end[None, :] <= it[:, None]).astype(jnp.int32), axis=1), N_EXPERTS - 1)
    sel = (exp[:, None] == jnp.arange(N_EXPERTS, dtype=jnp.int32)[None, :]).astype(jnp.int32)
    pick = lambda v: jnp.sum(sel * v[None, :], axis=1)
    blk = pick(starts) // MOE_ROWS + it - pick(item_end - n_blk)
    lo = jnp.maximum(pick(starts), blk * MOE_ROWS) - blk * MOE_ROWS
    hi = jnp.minimum(pick(ends), (blk + 1) * MOE_ROWS) - blk * MOE_ROWS
    shape = (cnt.shape[0], 1, N_EXPERTS)
    items = tuple(a.astype(jnp.int32) for a in (blk, exp, lo, hi, n_items.reshape(1)))
    return cnt.reshape(shape), lstart.reshape(shape).astype(jnp.int32), goff.reshape(shape).astype(jnp.int32), items


def _combine_kernel(final, x_ref, h2_ref, w_ref, rloc_ref, mod_ref, fn_ref, wsgu_ref, wsd_ref,
                    cnt_ref, lstart_ref, goff_ref, cnt_n, lstart_n, goff_n, y_hbm, o_ref, ybuf, sem):
    i = pl.program_id(0)
    n_tiles = pl.num_programs(0)
    slot = i % 2
    tm = x_ref.shape[0]

    def fetch(refs, s):
        _segment_copies(*refs, lambda loc, glob: pltpu.make_async_copy(y_hbm.at[glob], ybuf.at[s, loc], sem.at[s]))

    @pl.when(i == 0)
    def _():
        fetch((cnt_ref, lstart_ref, goff_ref), 0)

    @pl.when(i + 1 < n_tiles)
    def _():
        fetch((cnt_n, lstart_n, goff_n), 1 - slot)

    gu = _dot(h2_ref[...], wsgu_ref[...])
    shared = _dot((_silu(gu[:, :D_SHARED]) * gu[:, D_SHARED:]).astype(BF16), wsd_ref[...])

    rid = _iota2((tm, TILE_SLOTS), 1)
    w = w_ref[...]
    unsort = jnp.zeros((tm, TILE_SLOTS), F32)
    for k in range(TOP_K):
        unsort = jnp.where(rid == rloc_ref[:, k:k + 1], w[:, k:k + 1], unsort)
    unsort = unsort.astype(BF16)

    pltpu.make_async_copy(y_hbm.at[pl.ds(0, TILE_SLOTS * SLAB)], ybuf.at[slot], sem.at[slot]).wait()
    pieces = [_dot(unsort, ybuf[slot, pl.ds(s, TILE_SLOTS, stride=SLAB), :].astype(BF16)) for s in range(SLAB)]
    routed = jnp.concatenate(pieces, axis=1)
    xo = x_ref[...] + mod_ref[0, 5:6, :] * (routed + shared)
    if final:
        xo = xo * lax.rsqrt(jnp.mean(xo * xo, axis=-1, keepdims=True) + EPS) * fn_ref[...]
    o_ref[...] = xo


def _combine(final, x_new, h2, y, w_tk, rloc_tk, plan, mod, final_norm, wsgu, wsd, mod_index):
    t, d = x_new.shape
    n_tiles = t // ROW_TILE
    row = lambda r: (r, 0)
    cur = lambda r: (r, 0, 0)
    nxt = lambda r: (jnp.minimum(r + 1, n_tiles - 1), 0, 0)
    smem_tile = functools.partial(pl.BlockSpec, (1, 1, N_EXPERTS), memory_space=pltpu.SMEM)
    cnt, lstart, goff = plan
    return pl.pallas_call(
        functools.partial(_combine_kernel, final),
        grid=(n_tiles,),
        in_specs=[pl.BlockSpec((ROW_TILE, d), row),
                  pl.BlockSpec((ROW_TILE, d), row),
                  pl.BlockSpec((ROW_TILE, SUBLANES), row),
                  pl.BlockSpec((ROW_TILE, SUBLANES), row),
                  pl.BlockSpec((1, 6, d), mod_index),
                  pl.BlockSpec((1, d), lambda r: (0, 0)),
                  pl.BlockSpec(wsgu.shape, lambda r: (0, 0)),
                  pl.BlockSpec(wsd.shape, lambda r: (0, 0)),
                  smem_tile(cur), smem_tile(cur), smem_tile(cur),
                  smem_tile(nxt), smem_tile(nxt), smem_tile(nxt),
                  pl.BlockSpec(memory_space=pl.ANY)],
        out_specs=pl.BlockSpec((ROW_TILE, d), row),
        out_shape=jax.ShapeDtypeStruct((t, d), F32),
        scratch_shapes=[pltpu.VMEM((2, TILE_SLOTS * SLAB, LANES), F32),
                        pltpu.SemaphoreType.DMA((2,))],
        compiler_params=_CP(dimension_semantics=("arbitrary",)),
        name="combine_shared",
    )(x_new, h2, w_tk, rloc_tk, mod, final_norm.reshape(1, d), wsgu, wsd,
      cnt, lstart, goff, cnt, lstart, goff, y)


def _qk_perm():
    half = DK_R // 2
    perm = np.zeros(RET_QK_W, np.int32)
    for part in range(2):
        for h in range(H_R):
            for j in range(half):
                perm[part * LANES + h * half + j] = h * DK_R + part * half + j
    return perm


def _ret_state_rows():
    half = DK_R // 2
    rows = np.zeros((H_R, DK_R), np.int32)
    for h in range(H_R):
        for j in range(DK_R):
            rows[h, j] = (j // half) * LANES + h * half + j % half
    return rows


def _xbc_channels():
    ch = list(range(BR_W))
    for base in (BR_W, BR_W + G_M * N_M):
        for g in range(G_M):
            grp = list(range(base + g * N_M, base + (g + 1) * N_M))
            ch += grp + grp
    return np.asarray(ch, np.int32)


def _pad_lanes(v):
    return jnp.zeros((1, LANES), F32).at[0, :v.shape[0]].set(v.astype(F32))


def _ret_state_pack(s):
    rows = _ret_state_rows()
    src = np.full((H_R, 2 * LANES), DK_R, np.int32)
    for h in range(H_R):
        src[h, rows[h]] = np.arange(DK_R)
    sz = jnp.concatenate([s, jnp.zeros(s.shape[:3] + (1, DV_R), s.dtype)], axis=3)
    return sz[:, :, np.arange(H_R)[:, None], src, :]


def _ret_state_unpack(s):
    return s[:, :, np.arange(H_R)[:, None], _ret_state_rows(), :]


def _ssm_state_pack(s):
    b = s.shape[0]
    sr = s.reshape(b, 2, H_M // 2, 2, N_M, 1, P_M)
    eye = jnp.eye(2, dtype=s.dtype).reshape(1, 1, 1, 2, 1, 2, 1)
    return (sr * eye).reshape(b, 2, H_M // 2, 2 * N_M, 2 * P_M)


def _ssm_state_unpack(s):
    b = s.shape[0]
    sr = s.reshape(b, 2, H_M // 2, 2, N_M, 2, P_M)
    return jnp.stack([sr[:, :, :, 0, :, 0, :], sr[:, :, :, 1, :, 1, :]], axis=3).reshape(b, 2, H_M, N_M, P_M)


def _grid_rope(l):
    rows = l // GRID_W
    row = jnp.repeat(jnp.arange(rows), GRID_W).astype(F32)
    col = (jnp.arange(rows * GRID_W) % GRID_W).astype(F32)
    n_freq = DK_R // 4
    freqs = ROPE_BASE ** (-jnp.arange(n_freq, dtype=F32) / n_freq)
    ang = jnp.concatenate([row[:, None] * freqs, col[:, None] * freqs], axis=-1)
    return jnp.tile(jnp.cos(ang), (1, H_R)), jnp.tile(jnp.sin(ang), (1, H_R))


def _layer_weights(i, w_in, ssm_conv_w, ssm_conv_b):
    cuts = np.cumsum(IN_SPLITS)[:-1]
    rq, rk, rv, rg, sz, sxbc, sdt, hq, hf, hi, hg, gl = jnp.split(w_in[i], cuts, axis=1)
    perm = _qk_perm()
    ch = _xbc_channels()
    w_ret = jnp.concatenate([rq[:, perm], rk[:, perm] * (DK_R ** -0.5), rv, rg], axis=1).astype(BF16)
    w_ssm = jnp.concatenate([sz, sxbc[:, ch]], axis=1).astype(BF16)
    w_hg = jnp.concatenate([hq, hi, hg], axis=1).astype(BF16)
    w_prec = jnp.concatenate([hf, sdt, jnp.zeros((w_in.shape[1], LANES - SSM_DT_W), F32)], axis=1).astype(BF16)
    conv_w = ssm_conv_w[i][ch].T
    conv_b = ssm_conv_b[i][ch][None, :]
    return (w_ret, w_ssm, w_hg, gl.astype(BF16), w_prec), conv_w, conv_b


def kernel(x_prompt, x_sample, state_ret, state_ssm, state_hgrn, c, c_ctx, ada_w, ada_b, norm1, norm2, w_in,
           ret_decay_logit, ret_gn, ssm_conv_w, ssm_conv_b, ssm_a_log, ssm_dt_bias, ssm_d, ssm_norm,
           hgrn_lb_logits, hgrn_norm, w_branch, w_out, router_w, router_b, exp_w_gate, exp_w_up,
           exp_w_down, sh_w_gate, sh_w_up, sh_w_down, final_norm):
    bc, lc, d = x_prompt.shape
    bl, ll, _ = x_sample.shape
    tc, tl = bc * lc, bl * ll
    t = tc + tl
    assert tc % ROW_TILE == 0 and ll % ROW_TILE == 0 and lc % CHUNK == 0 and ll % CHUNK == 0
    assert tl % lc == 0 and tc % ll == 0 and 1 + bl <= SUBLANES
    mod_index = _mod_index(tc // ROW_TILE, ll // ROW_TILE)

    x_all = jnp.concatenate([x_prompt.reshape(tc, d), x_sample.reshape(tl, d)], axis=0)
    c_all = jnp.concatenate([c_ctx[None, :], c, jnp.zeros((SUBLANES - 1 - bl, d), F32)], axis=0)
    mods = _ada(c_all, ada_w, ada_b)

    lb_all = jnp.cumsum(jax.nn.softmax(hgrn_lb_logits.astype(F32), axis=1), axis=1)
    lb_all = lb_all - lb_all[:, :1]
    cos4, sin4 = _grid_rope(ll)
    lane_head = (np.arange(2 * LANES) % LANES) // (DK_R // 2)

    new_ret, new_ssm, new_hg = [], [], []
    for i in range(DEPTH):
        mod = mods[i].reshape(SUBLANES, 6, d)
        weights, conv_w, conv_b = _layer_weights(i, w_in, ssm_conv_w, ssm_conv_b)
        ret_in, ssm_in, hg_in, gl, prec = _inproj(x_all, mod, norm1[i], weights, (BF16, BF16, BF16, BF16, F32),
                                                  mod_index)

        lg = jax.nn.log_sigmoid(ret_decay_logit[i].astype(F32))
        lgl = lg[:, lane_head]
        lgv = jnp.broadcast_to(lg[:, :, None], (2, H_R, LANES))
        gn = ret_gn[i][None, :]
        zero_ret = jnp.zeros((bc, 2, H_R, 2 * LANES, DV_R), F32)
        o_ret_c, s_ret = _retention(ret_in, 0, bc, lc, None, None, lgl, lgv, gn, zero_ret)
        o_ret_l, _ = _retention(ret_in, tc, bl, ll, cos4, sin4, lgl, lgv, gn, _ret_state_pack(state_ret[:, i]))
        new_ret.append(_ret_state_unpack(s_ret))

        dtb = _pad_lanes(ssm_dt_bias[i].reshape(-1))
        na = _pad_lanes(-jnp.exp(ssm_a_log[i].astype(F32)).reshape(-1))
        dsk = jnp.repeat(ssm_d[i], P_M)[None, :]
        ng = ssm_norm[i][None, :]
        zero_ssm = jnp.zeros((bc, 2, H_M // 2, LANES, LANES), F32)
        o_ssm_c, s_ssm = _ssd(ssm_in, prec, 0, bc, lc, conv_w, conv_b, dtb, na, dsk, ng, zero_ssm)
        o_ssm_l, _ = _ssd(ssm_in, prec, tc, bl, ll, conv_w, conv_b, dtb, na, dsk, ng,
                          _ssm_state_pack(state_ssm[:, i]))
        new_ssm.append(_ssm_state_unpack(s_ssm))

        lb = lb_all[:, i]
        llb = jnp.log(lb).reshape(1, 2 * BR_W)
        oml = (1.0 - lb).reshape(1, 2 * BR_W)
        l1m = jnp.log1p(-lb).reshape(1, 2 * BR_W)
        hn = hgrn_norm[i][None, :]
        zero_hg = jnp.zeros((bc, 2, H_C, DV_C, E_C), F32)
        o_hg_c, s_hg = _hgrn(hg_in, prec, 0, bc, lc, llb, oml, l1m, hn, zero_hg)
        o_hg_l, _ = _hgrn(hg_in, prec, tc, bl, ll, llb, oml, l1m, hn,
                          jnp.swapaxes(state_hgrn[:, i], -1, -2))
        new_hg.append(jnp.swapaxes(s_hg, -1, -2))

        o_ret = jnp.concatenate([o_ret_c, o_ret_l], axis=0)
        o_ssm = jnp.concatenate([o_ssm_c, o_ssm_l], axis=0)
        o_hg = jnp.concatenate([o_hg_c, o_hg_l], axis=0)
        x_new, h2 = _merge(x_all, mod, norm2[i], o_ret, o_ssm, o_hg, gl,
                           w_branch[i].astype(BF16), w_out[i].astype(BF16), mod_index)

        rloc, w8, cnt_tiles = _router(x_new, mod, norm2[i], router_w[i].T, router_b[i][:, None], mod_index)
        cnt, lstart, goff, items = _moe_plan(cnt_tiles, t)
        xs = _dispatch(h2, rloc, cnt, lstart, goff)
        y = _experts(xs, items, exp_w_gate[i], exp_w_up[i], exp_w_down[i])
        wsgu = jnp.concatenate([sh_w_gate[i], sh_w_up[i]], axis=-1).astype(BF16)
        x_all = _combine(i == DEPTH - 1, x_new, h2, y, w8.T, rloc.T, (cnt, lstart, goff), mod, final_norm, wsgu,
                         sh_w_down[i].astype(BF16), mod_index)

    y_prompt = x_all[:tc].reshape(bc, lc, d)
    y_sample = x_all[tc:].reshape(bl, ll, d)
    return (y_prompt, y_sample, jnp.stack(new_ret, axis=1), jnp.stack(new_ssm, axis=1),
            jnp.stack(new_hg, axis=1))
```

```python
import functools

import numpy as np
import jax
import jax.numpy as jnp
from jax import lax
from jax.experimental import pallas as pl
from jax.experimental.pallas import tpu as pltpu

F32 = jnp.float32
BF16 = jnp.bfloat16
HIGHEST = lax.Precision.HIGHEST

D_MODEL = 1024
DEPTH = 2
GRID_W = 64
H_R, DK_R, DV_R = 4, 64, 128
H_M, P_M, N_M, G_M, D_CONV = 8, 64, 64, 2, 5
H_C, E_C, DV_C = 4, 128, 128
BR_W = 512
N_BRANCH = 3
RET_QK_W = H_R * DK_R
CONV_CH = BR_W + 2 * G_M * N_M
SSM_DT_W = 2 * H_M
IN_SPLITS = (RET_QK_W, RET_QK_W, BR_W, BR_W, BR_W, CONV_CH, SSM_DT_W, BR_W, 2 * BR_W, BR_W, BR_W,
             N_BRANCH * D_MODEL)
N_EXPERTS, TOP_K, N_GROUPS, TOPK_GROUPS = 64, 6, 8, 4
D_EXPERT = 256
D_SHARED = 256
ROUTED_SCALE = 2.5
EPS = 1e-6
ROPE_BASE = 10000.0

LANES = 128
SUBLANES = 8
SLAB = D_MODEL // LANES
CHUNK = 128
ROW_TILE = 256
MOE_ROWS = 256
VMEM_LIMIT = 56 * 1024 * 1024
NEG_BIG = -1e30

_CP = functools.partial(pltpu.CompilerParams, vmem_limit_bytes=VMEM_LIMIT)


def _sigmoid(x):
    return 1.0 / (1.0 + jnp.exp(-x))


def _silu(x):
    return x * _sigmoid(x)


def _softplus(x):
    return jnp.maximum(x, 0.0) + jnp.log1p(jnp.exp(-jnp.abs(x)))


def _log_sigmoid(x):
    return jnp.minimum(x, 0.0) - jnp.log1p(jnp.exp(-jnp.abs(x)))


def _dot(a, b):
    return jnp.dot(a, b, preferred_element_type=F32)


def _dot_nt(a, b):
    return lax.dot_general(a, b, (((1,), (1,)), ((), ())), preferred_element_type=F32)


def _dot_tn(a, b):
    return lax.dot_general(a, b, (((0,), (0,)), ((), ())), preferred_element_type=F32)


def _tri_dot(tri, x):
    hi = x.astype(BF16)
    r1 = x - hi.astype(F32)
    mid = r1.astype(BF16)
    lo = (r1 - mid.astype(F32)).astype(BF16)
    return _dot(tri, hi) + _dot(tri, mid) + _dot(tri, lo)


def _iota2(shape, dim):
    return lax.broadcasted_iota(jnp.int32, shape, dim)


def _rows(c):
    return pl.ds(pl.multiple_of(c * CHUNK, CHUNK), CHUNK)


def _const_spec(shape):
    nd = len(shape)
    return pl.BlockSpec(shape, lambda *_: (0,) * nd)


def _ada_kernel(c_ref, w_ref, b_ref, o_ref):
    a = _silu(c_ref[...])
    o_ref[0] = jnp.dot(a, w_ref[0], preferred_element_type=F32, precision=HIGHEST) + b_ref[0]


def _ada(c_all, ada_w, ada_b):
    depth, d, n = ada_w.shape
    tn = 1536
    return pl.pallas_call(
        _ada_kernel,
        grid=(depth, n // tn),
        in_specs=[pl.BlockSpec((SUBLANES, d), lambda i, j: (0, 0)),
                  pl.BlockSpec((1, d, tn), lambda i, j: (i, 0, j)),
                  pl.BlockSpec((1, 1, tn), lambda i, j: (i, 0, j))],
        out_specs=pl.BlockSpec((1, SUBLANES, tn), lambda i, j: (i, 0, j)),
        out_shape=jax.ShapeDtypeStruct((depth, SUBLANES, n), F32),
        compiler_params=_CP(dimension_semantics=("arbitrary", "arbitrary")),
        name="ada_mod",
    )(c_all, ada_w, ada_b.reshape(depth, 1, n))


def _mod_index(n_ctx_tiles, tiles_per_seq):
    def index(r):
        return (jnp.where(r < n_ctx_tiles, 0, 1 + (r - n_ctx_tiles) // tiles_per_seq), 0, 0)
    return index


def _modulated_norm(x, g, shift, scale):
    y = x * lax.rsqrt(jnp.mean(x * x, axis=-1, keepdims=True) + EPS) * g
    return y * (1.0 + scale) + shift


def _group_specs(cols, n_ctx_tiles):
    return [pl.BlockSpec((ROW_TILE, cols), lambda r: (jnp.minimum(r, n_ctx_tiles - 1), 0)),
            pl.BlockSpec((ROW_TILE, cols), lambda r: (jnp.maximum(r - n_ctx_tiles, 0), 0))]


def _group_pick(n_ctx_tiles, ctx_ref, lat_ref):
    return jnp.where(pl.program_id(0) < n_ctx_tiles, ctx_ref[...], lat_ref[...])


def _inproj_kernel(n_ctx_tiles, xc_ref, xl_ref, mod_ref, n_ref, w0, w1, w2, w3, w4, o0, o1, o2, o3, o4):
    x = _group_pick(n_ctx_tiles, xc_ref, xl_ref)
    h = _modulated_norm(x, n_ref[...], mod_ref[0, 0:1, :], mod_ref[0, 1:2, :]).astype(BF16)
    for w, o in ((w0, o0), (w1, o1), (w2, o2), (w3, o3), (w4, o4)):
        o[...] = _dot(h, w[...]).astype(o.dtype)


def _inproj(x_pair, mod, norm_g, weights, out_dtypes, mod_index):
    xc, xl = x_pair
    d = xc.shape[1]
    t = xc.shape[0] + xl.shape[0]
    n_ctx_tiles = xc.shape[0] // ROW_TILE
    w_specs = [pl.BlockSpec(w.shape, lambda r: (0, 0), pipeline_mode=pl.Buffered(1)) for w in weights]
    return pl.pallas_call(
        functools.partial(_inproj_kernel, n_ctx_tiles),
        grid=(t // ROW_TILE,),
        in_specs=_group_specs(d, n_ctx_tiles) + [pl.BlockSpec((1, 6, d), mod_index),
                                                 pl.BlockSpec((1, d), lambda r: (0, 0))] + w_specs,
        out_specs=[pl.BlockSpec((ROW_TILE, w.shape[1]), lambda r: (r, 0)) for w in weights],
        out_shape=[jax.ShapeDtypeStruct((t, w.shape[1]), dt) for w, dt in zip(weights, out_dtypes)],
        compiler_params=_CP(dimension_semantics=("arbitrary",)),
        name="norm_inproj",
    )(xc, xl, mod, norm_g.reshape(1, d), *weights)


def _ret_kernel(is_ctx, n_chunks, in_ref, cos_ref, sin_ref, lgl_ref, lgv_ref, gn_ref, *rest):
    if is_ctx:
        o_ref, sfin_ref, qk_s, oacc, st, dm_s = rest
    else:
        s0_ref, o_ref, qk_s, oacc, st, dm_s = rest
    use_rope = not is_ctx
    qw = 2 * LANES
    ii = _iota2((CHUNK, CHUNK), 0)
    jj = _iota2((CHUNK, CHUNK), 1)
    dist = jnp.abs(ii - jj).astype(F32)
    for h in range(H_R):
        dm_s[h] = (jnp.where(ii >= jj, jnp.exp(dist * lgv_ref[0, h:h + 1, :]), 0.0)
                   + jnp.where(jj >= ii, jnp.exp(dist * lgv_ref[1, h:h + 1, :]), 0.0))
    st[...] = jnp.zeros(st.shape, F32) if is_ctx else s0_ref[0]

    lane_head = (_iota2((1, qw), 1) % LANES) // (DK_R // 2)
    rr = _iota2((CHUNK, qw), 0).astype(F32)
    lg_f = lgl_ref[0:1, :]
    lg_b = lgl_ref[1:2, :]

    def rope(x, cs, sn):
        x1, x2 = x[:, :LANES], x[:, LANES:]
        return jnp.concatenate([x1 * cs - x2 * sn, x1 * sn + x2 * cs], axis=1)

    def fwd(c, carry):
        rows = _rows(c)
        q = in_ref[rows, 0:qw].astype(F32)
        k = in_ref[rows, qw:2 * qw].astype(F32)
        if use_rope:
            cs, sn = cos_ref[rows, :], sin_ref[rows, :]
            q, k = rope(q, cs, sn), rope(k, cs, sn)
        qk_s[rows, 0:qw] = q
        qk_s[rows, qw:2 * qw] = k
        kb = k.astype(BF16)
        q_dec = (q * jnp.exp((rr + 1.0) * lg_f)).astype(BF16)
        k_dec = k * jnp.exp((CHUNK - 1.0 - rr) * lg_f)
        for h in range(H_R):
            hs = slice(h * DV_R, (h + 1) * DV_R)
            mh = lane_head == h
            vh = in_ref[rows, 2 * qw + h * DV_R:2 * qw + (h + 1) * DV_R]
            s = _dot_nt(jnp.where(mh, q, 0.0).astype(BF16), kb)
            intra = _dot((s * dm_s[h]).astype(BF16), vh)
            sf = st[0, h]
            oacc[rows, hs] = intra + _dot(q_dec, sf.astype(BF16))
            st[0, h] = (sf * jnp.exp(CHUNK * lgv_ref[0, h:h + 1, :])
                        + _dot_tn(jnp.where(mh, k_dec, 0.0).astype(BF16), vh))
        return carry

    lax.fori_loop(0, n_chunks, fwd, 0)

    def bwd(t, carry):
        rows = _rows(n_chunks - 1 - t)
        q = qk_s[rows, 0:qw]
        k = qk_s[rows, qw:2 * qw]
        q_dec = (q * jnp.exp((CHUNK - rr) * lg_b)).astype(BF16)
        k_dec = k * jnp.exp(rr * lg_b)
        for h in range(H_R):
            hs = slice(h * DV_R, (h + 1) * DV_R)
            mh = lane_head == h
            vh = in_ref[rows, 2 * qw + h * DV_R:2 * qw + (h + 1) * DV_R]
            sb = st[1, h]
            oacc[rows, hs] = oacc[rows, hs] + _dot(q_dec, sb.astype(BF16))
            st[1, h] = (sb * jnp.exp(CHUNK * lgv_ref[1, h:h + 1, :])
                        + _dot_tn(jnp.where(mh, k_dec, 0.0).astype(BF16), vh))
        return carry

    lax.fori_loop(0, n_chunks, bwd, 0)
    if is_ctx:
        half = DK_R // 2
        for d in range(2):
            for h in range(H_R):
                sfin_ref[0, d, h, 0:half, :] = st[d, h, h * half:(h + 1) * half, :]
                sfin_ref[0, d, h, half:DK_R, :] = st[d, h, LANES + h * half:LANES + (h + 1) * half, :]

    def fin(c, carry):
        rows = _rows(c)
        for h in range(H_R):
            hs = slice(h * DV_R, (h + 1) * DV_R)
            o = oacc[rows, hs]
            oc = o - jnp.mean(o, axis=-1, keepdims=True)
            y = oc * lax.rsqrt(jnp.mean(oc * oc, axis=-1, keepdims=True) + EPS) * gn_ref[:, hs]
            g = in_ref[rows, 4 * qw + h * DV_R:4 * qw + (h + 1) * DV_R].astype(F32)
            o_ref[rows, hs] = (y * _silu(g)).astype(o_ref.dtype)
        return carry

    lax.fori_loop(0, n_chunks, fin, 0)


def _state_io(s0, b, packed_shape, final_shape):
    lead = lambda shape: pl.BlockSpec((1,) + shape, lambda i: (i,) + (0,) * len(shape))
    if s0 is None:
        return [], [], [lead(final_shape)], [jax.ShapeDtypeStruct((b,) + final_shape, F32)]
    return [lead(packed_shape)], [s0], [], []


def _retention(ret_in, row0, b, l, cos4, sin4, lgl, lgv, gn, s0):
    is_ctx = s0 is None
    if is_ctx:
        cos4 = jnp.zeros((SUBLANES, LANES), F32)
        sin4 = cos4
        trig_spec = pl.BlockSpec((SUBLANES, LANES), lambda i: (0, 0))
    else:
        trig_spec = pl.BlockSpec((l, LANES), lambda i: (0, 0))
    blk0 = row0 // l
    packed = (2, H_R, 2 * LANES, DV_R)
    s_in, s_args, s_out, s_shapes = _state_io(s0, b, packed, (2, H_R, DK_R, DV_R))
    return pl.pallas_call(
        functools.partial(_ret_kernel, is_ctx, l // CHUNK),
        grid=(b,),
        in_specs=[pl.BlockSpec((l, ret_in.shape[1]), lambda i: (blk0 + i, 0)),
                  trig_spec, trig_spec,
                  _const_spec(lgl.shape), _const_spec(lgv.shape), _const_spec(gn.shape)] + s_in,
        out_specs=[pl.BlockSpec((l, BR_W), lambda i: (i, 0))] + s_out,
        out_shape=[jax.ShapeDtypeStruct((b * l, BR_W), BF16)] + s_shapes,
        scratch_shapes=[pltpu.VMEM((l, 4 * LANES), F32),
                        pltpu.VMEM((l, BR_W), F32),
                        pltpu.VMEM(packed, F32),
                        pltpu.VMEM((H_R, CHUNK, CHUNK), F32)],
        compiler_params=_CP(dimension_semantics=("arbitrary",)),
        name="retention",
    )(ret_in, cos4, sin4, lgl, lgv, gn, *s_args)


def _ssd_kernel(is_ctx, n_chunks, in_ref, dt_ref, cw_ref, cb_ref, dtb_ref, na_ref, dsk_ref, ng_ref, *rest):
    if is_ctx:
        o_ref, sfin_ref, pad_s, xc_s, y_s, dt_s, cumb_s, st = rest
    else:
        s0_ref, o_ref, pad_s, xc_s, y_s, dt_s, cumb_s, st = rest
    l = in_ref.shape[0]
    cw = 2 * BR_W
    halo = SUBLANES
    pad_s[0:halo, :] = jnp.zeros((halo, cw), F32)
    pad_s[l + halo:l + 2 * halo, :] = jnp.zeros((halo, cw), F32)

    def fill(c, carry):
        rows = _rows(c)
        dst = pl.ds(pl.multiple_of(c * CHUNK + halo, SUBLANES), CHUNK)
        pad_s[dst, :] = in_ref[rows, BR_W:BR_W + cw].astype(F32)
        return carry

    lax.fori_loop(0, n_chunks, fill, 0)
    st[...] = jnp.zeros(st.shape, F32) if is_ctx else s0_ref[0]

    ii = _iota2((CHUNK, CHUNK), 0)
    jj = _iota2((CHUNK, CHUNK), 1)
    tril = jnp.where(ii >= jj, 1.0, 0.0).astype(BF16)
    triu = jnp.where(jj >= ii, 1.0, 0.0).astype(BF16)
    lane_lo = jj < P_M
    row_lo = ii < N_M
    blockdiag = lane_lo == row_lo
    half = D_CONV // 2

    def pair_cols(vals, h0, h1):
        return jnp.where(lane_lo, vals[:, h0:h0 + 1], vals[:, h1:h1 + 1])

    def fwd(c, carry):
        rows = _rows(c)
        win = pad_s[pl.ds(pl.multiple_of(c * CHUNK, CHUNK), CHUNK + 2 * halo), :]
        acc = win[halo - half:halo - half + CHUNK, :] * cw_ref[0:1, :] + cb_ref[...]
        for w in range(1, D_CONV):
            acc = acc + win[halo - half + w:halo - half + w + CHUNK, :] * cw_ref[w:w + 1, :]
        xc = _silu(acc)
        xc_s[rows, :] = xc.astype(xc_s.dtype)
        x = xc[:, 0:BR_W]
        dt = _softplus(dt_ref[rows, :] + dtb_ref[...])
        la = dt * na_ref[...]
        cumf = _tri_dot(tril, la)
        cumb = _tri_dot(triu, la)
        dt_s[rows, :] = dt
        cumb_s[rows, :] = cumb
        cumf_t, cumb_t, dt_t = cumf.T, cumb.T, dt.T
        totf = cumf[CHUNK - 1:CHUNK, :]
        etotf = jnp.exp(totf)
        kdec = dt * jnp.exp(totf - cumf)
        qdec = jnp.exp(cumf)
        for g in range(G_M):
            b2 = xc[:, BR_W + g * LANES:BR_W + (g + 1) * LANES]
            c2 = xc[:, BR_W + (G_M + g) * LANES:BR_W + (G_M + g + 1) * LANES]
            cbm = _dot_nt(jnp.where(lane_lo, c2, 0.0).astype(BF16), b2.astype(BF16))
            for pp in range(2):
                p = 2 * g + pp
                h0, h1 = 2 * p, 2 * p + 1
                ps = slice(p * LANES, (p + 1) * LANES)
                ms = []
                for h in (h0, h1):
                    hb = H_M + h
                    mf = jnp.where(ii >= jj, jnp.exp(cumf[:, h:h + 1] - cumf_t[h:h + 1, :]), 0.0) * dt_t[h:h + 1, :]
                    mb = (jnp.where(jj >= ii, jnp.exp(cumb[:, hb:hb + 1] - cumb_t[hb:hb + 1, :]), 0.0)
                          * dt_t[hb:hb + 1, :])
                    ms.append((cbm * (mf + mb)).astype(BF16))
                xp = x[:, ps]
                xbd = jnp.concatenate([jnp.where(lane_lo, xp, 0.0), jnp.where(lane_lo, 0.0, xp)], axis=0)
                intra = _dot(jnp.concatenate(ms, axis=1), xbd.astype(BF16))
                sf = st[0, p]
                inter = _dot((c2 * pair_cols(qdec, h0, h1)).astype(BF16), sf.astype(BF16))
                y_s[rows, ps] = intra + inter
                kv = _dot_tn((b2 * pair_cols(kdec, h0, h1)).astype(BF16), xp.astype(BF16))
                arow = jnp.where(row_lo, etotf[:, h0:h0 + 1], etotf[:, h1:h1 + 1])
                st[0, p] = sf * arow + jnp.where(blockdiag, kv, 0.0)
        return carry

    lax.fori_loop(0, n_chunks, fwd, 0)

    def bwd(t, carry):
        rows = _rows(n_chunks - 1 - t)
        xc = xc_s[rows, :].astype(F32)
        x = xc[:, 0:BR_W]
        dt = dt_s[rows, :]
        cumb = cumb_s[rows, :]
        totb = cumb[0:1, :]
        etotb = jnp.exp(totb)
        kdec = dt * jnp.exp(totb - cumb)
        qdec = jnp.exp(cumb)
        for g in range(G_M):
            b2 = xc[:, BR_W + g * LANES:BR_W + (g + 1) * LANES]
            c2 = xc[:, BR_W + (G_M + g) * LANES:BR_W + (G_M + g + 1) * LANES]
            for pp in range(2):
                p = 2 * g + pp
                h0, h1 = H_M + 2 * p, H_M + 2 * p + 1
                ps = slice(p * LANES, (p + 1) * LANES)
                xp = x[:, ps]
                sb = st[1, p]
                y_s[rows, ps] = y_s[rows, ps] + _dot((c2 * pair_cols(qdec, h0, h1)).astype(BF16), sb.astype(BF16))
                kv = _dot_tn((b2 * pair_cols(kdec, h0, h1)).astype(BF16), xp.astype(BF16))
                arow = jnp.where(row_lo, etotb[:, h0:h0 + 1], etotb[:, h1:h1 + 1])
                st[1, p] = sb * arow + jnp.where(blockdiag, kv, 0.0)
        return carry

    lax.fori_loop(0, n_chunks, bwd, 0)
    if is_ctx:
        for d in range(2):
            for p in range(H_M // 2):
                pair = st[d, p]
                sfin_ref[0, d, 2 * p] = pair[0:N_M, 0:P_M]
                sfin_ref[0, d, 2 * p + 1] = pair[N_M:2 * N_M, P_M:2 * P_M]

    def fin(c, carry):
        rows = _rows(c)
        x = xc_s[rows, 0:BR_W].astype(F32)
        z = in_ref[rows, 0:BR_W].astype(F32)
        u = (y_s[rows, :] + x * dsk_ref[...]) * _silu(z)
        y = u * lax.rsqrt(jnp.mean(u * u, axis=-1, keepdims=True) + EPS) * ng_ref[...]
        o_ref[rows, :] = y.astype(o_ref.dtype)
        return carry

    lax.fori_loop(0, n_chunks, fin, 0)


def _ssd(ssm_in, prec, row0, b, l, cw, cb, dtb, na, dsk, ng, s0):
    blk0 = row0 // l
    dt_col = (prec.shape[1] - LANES) // LANES
    packed = (2, H_M // 2, LANES, LANES)
    s_in, s_args, s_out, s_shapes = _state_io(s0, b, packed, (2, H_M, N_M, P_M))
    return pl.pallas_call(
        functools.partial(_ssd_kernel, s0 is None, l // CHUNK),
        grid=(b,),
        in_specs=[pl.BlockSpec((l, ssm_in.shape[1]), lambda i: (blk0 + i, 0)),
                  pl.BlockSpec((l, LANES), lambda i: (blk0 + i, dt_col)),
                  _const_spec(cw.shape), _const_spec(cb.shape), _const_spec(dtb.shape),
                  _const_spec(na.shape), _const_spec(dsk.shape), _const_spec(ng.shape)] + s_in,
        out_specs=[pl.BlockSpec((l, BR_W), lambda i: (i, 0))] + s_out,
        out_shape=[jax.ShapeDtypeStruct((b * l, BR_W), BF16)] + s_shapes,
        scratch_shapes=[pltpu.VMEM((l + 2 * SUBLANES, 2 * BR_W), F32),
                        pltpu.VMEM((l, 2 * BR_W), BF16),
                        pltpu.VMEM((l, BR_W), F32),
                        pltpu.VMEM((l, LANES), F32),
                        pltpu.VMEM((l, LANES), F32),
                        pltpu.VMEM(packed, F32)],
        compiler_params=_CP(dimension_semantics=("arbitrary",)),
        name="ssd",
    )(ssm_in, prec, cw, cb, dtb, na, dsk, ng, *s_args)


_HG_LEVELS = (64, 32, 16, 8, 4, 2)
LOG2_E = 1.4426950408889634


def _block_ref_rows(x, m, row):
    size = 2 * m
    if size >= 2 * SUBLANES:
        return _bcast_block_row(x, size, row)
    out = _bcast_group_row(x, row)
    sub = _iota2((CHUNK, LANES), 0) % SUBLANES
    for b in range(1, SUBLANES // size):
        out = jnp.where(sub >= b * size, _bcast_group_row(x, b * size + row), out)
    return out


def _bcast_group_row(x, j):
    x3 = x.reshape(CHUNK // SUBLANES, SUBLANES, LANES)
    r = jnp.broadcast_to(x3[:, j:j + 1, :], x3.shape)
    return r.reshape(CHUNK, LANES)


def _bcast_block_row(x, size, j):
    pieces = [jnp.broadcast_to(x[b * size + j:b * size + j + 1, :], (size, LANES)) for b in range(CHUNK // size)]
    return pieces[0] if len(pieces) == 1 else jnp.concatenate(pieces, axis=0)


def _hgrn_kernel(is_ctx, n_chunks, in_ref, f_ref, llb_ref, oml_ref, l1m_ref, ng_ref, *rest):
    if is_ctx:
        o_ref, sfin_ref, oacc, st = rest
    else:
        s0_ref, o_ref, oacc, st = rest
    ii = _iota2((CHUNK, CHUNK), 0)
    jj = _iota2((CHUNK, CHUNK), 1)
    tril = jnp.where(ii >= jj, 1.0, 0.0).astype(BF16)
    triu = jnp.where(jj >= ii, 1.0, 0.0).astype(BF16)
    odd = (ii % 2) == 1
    same_pair = (ii // 2) == (jj // 2)
    ones_sq = jnp.ones((LANES, CHUNK), BF16)
    if is_ctx:
        st[...] = jnp.zeros(st.shape, F32)
    else:
        for d in range(2):
            for h in range(H_C):
                st[d, h] = s0_ref[0, d, h].T

    def gates(rows, d):
        fr = f_ref[rows, d * BR_W:(d + 1) * BR_W]
        ds_ = slice(d * BR_W, (d + 1) * BR_W)
        a = llb_ref[:, ds_]
        bterm = l1m_ref[:, ds_] + _log_sigmoid(fr)
        logf = jnp.maximum(a, bterm) + jnp.log1p(jnp.exp(-jnp.abs(a - bterm)))
        key = oml_ref[:, ds_] * _sigmoid(-fr)
        return logf * LOG2_E, key

    def fwd(c, carry):
        rows = _rows(c)
        lgf_all, key_f = gates(rows, 0)
        lgb_all, key_b = gates(rows, 1)
        cumf_all = _tri_dot(tril, lgf_all)
        cumb_all = _tri_dot(triu, lgb_all)
        for h in range(H_C):
            hs = slice(h * LANES, (h + 1) * LANES)
            q = in_ref[rows, hs].astype(F32)
            v = in_ref[rows, BR_W + h * LANES:BR_W + (h + 1) * LANES]
            cumf, cumb = cumf_all[:, hs], cumb_all[:, hs]
            kf, kb = key_f[:, hs], key_b[:, hs]
            sc = jnp.zeros((CHUNK, CHUNK), F32)
            for m in _HG_LEVELS:
                upper = (ii % (2 * m)) >= m
                same_block = (ii // (2 * m)) == (jj // (2 * m))
                ref_f = _block_ref_rows(cumf, m, m - 1)
                ref_b = _block_ref_rows(cumb, m, m)
                e_f = jnp.exp2(jnp.where(upper, cumf - ref_f, ref_f - cumf))
                e_b = jnp.exp2(jnp.where(upper, ref_b - cumb, cumb - ref_b))
                qcat = jnp.concatenate([jnp.where(upper, q * e_f, 0.0), jnp.where(upper, 0.0, q * e_b)], axis=1)
                kcat = jnp.concatenate([jnp.where(upper, 0.0, kf * e_f), jnp.where(upper, kb * e_b, 0.0)], axis=1)
                sc = sc + jnp.where(same_block, _dot_nt(qcat.astype(BF16), kcat.astype(BF16)), 0.0)
            qcat = jnp.concatenate([jnp.where(odd, q * jnp.exp2(lgf_all[:, hs]), 0.0),
                                    jnp.where(odd, 0.0, q * jnp.exp2(lgb_all[:, hs]))], axis=1)
            kcat = jnp.concatenate([jnp.where(odd, 0.0, kf), jnp.where(odd, kb, 0.0)], axis=1)
            sc = sc + jnp.where(same_pair, _dot_nt(qcat.astype(BF16), kcat.astype(BF16)), 0.0)
            sc = sc + jnp.where(ii == jj, _dot((q * (kf + kb)).astype(BF16), ones_sq), 0.0)
            intra = _dot(sc.astype(BF16), v)
            stf = st[0, h]
            totf = cumf[CHUNK - 1:CHUNK, :]
            inter = _dot_nt((q * jnp.exp2(cumf)).astype(BF16), stf.astype(BF16))
            oacc[rows, hs] = intra + inter
            st[0, h] = stf * jnp.exp2(totf) + _dot_tn(v, (kf * jnp.exp2(totf - cumf)).astype(BF16))
        return carry

    lax.fori_loop(0, n_chunks, fwd, 0)

    def bwd(t, carry):
        rows = _rows(n_chunks - 1 - t)
        lgb_all, key_b = gates(rows, 1)
        cumb_all = _tri_dot(triu, lgb_all)
        for h in range(H_C):
            hs = slice(h * LANES, (h + 1) * LANES)
            q = in_ref[rows, hs].astype(F32)
            v = in_ref[rows, BR_W + h * LANES:BR_W + (h + 1) * LANES]
            cumb, kb = cumb_all[:, hs], key_b[:, hs]
            stb = st[1, h]
            totb = cumb[0:1, :]
            oacc[rows, hs] = oacc[rows, hs] + _dot_nt((q * jnp.exp2(cumb)).astype(BF16), stb.astype(BF16))
            st[1, h] = stb * jnp.exp2(totb) + _dot_tn(v, (kb * jnp.exp2(totb - cumb)).astype(BF16))
        return carry

    lax.fori_loop(0, n_chunks, bwd, 0)
    if is_ctx:
        for d in range(2):
            for h in range(H_C):
                sfin_ref[0, d, h] = st[d, h].T

    def fin(c, carry):
        rows = _rows(c)
        for h in range(H_C):
            hs = slice(h * LANES, (h + 1) * LANES)
            o = oacc[rows, hs]
            y = o * lax.rsqrt(jnp.mean(o * o, axis=-1, keepdims=True) + EPS) * ng_ref[:, hs]
            g = in_ref[rows, 2 * BR_W + h * LANES:2 * BR_W + (h + 1) * LANES].astype(F32)
            o_ref[rows, hs] = (y * _silu(g)).astype(o_ref.dtype)
        return carry

    lax.fori_loop(0, n_chunks, fin, 0)


def _hgrn(hg_in, prec, row0, b, l, llb, oml, l1m, ng, s0):
    blk0 = row0 // l
    s_in, s_args, s_out, s_shapes = _state_io(s0, b, (2, H_C, E_C, DV_C), (2, H_C, E_C, DV_C))
    return pl.pallas_call(
        functools.partial(_hgrn_kernel, s0 is None, l // CHUNK),
        grid=(b,),
        in_specs=[pl.BlockSpec((l, hg_in.shape[1]), lambda i: (blk0 + i, 0)),
                  pl.BlockSpec((l, 2 * BR_W), lambda i: (blk0 + i, 0)),
                  _const_spec(llb.shape), _const_spec(oml.shape), _const_spec(l1m.shape),
                  _const_spec(ng.shape)] + s_in,
        out_specs=[pl.BlockSpec((l, BR_W), lambda i: (i, 0))] + s_out,
        out_shape=[jax.ShapeDtypeStruct((b * l, BR_W), BF16)] + s_shapes,
        scratch_shapes=[pltpu.VMEM((l, BR_W), F32),
                        pltpu.VMEM((2, H_C, DV_C, E_C), F32)],
        compiler_params=_CP(dimension_semantics=("arbitrary",)),
        name="hgrn2",
    )(hg_in, prec, llb, oml, l1m, ng, *s_args)


def _merge_kernel(n_ctx_tiles, xc_ref, xl_ref, mod_ref, n2_ref, o0c, o0l, o1c, o1l, o2c, o2l, gl_ref, wb_ref, wo_ref,
                  xo_ref, h2_ref):
    d = xc_ref.shape[1]
    merged = jnp.zeros(xc_ref.shape, F32)
    for k, (oc, ol) in enumerate(((o0c, o0l), (o1c, o1l), (o2c, o2l))):
        gate = _sigmoid(gl_ref[:, k * d:(k + 1) * d].astype(F32))
        merged = merged + gate * _dot(_group_pick(n_ctx_tiles, oc, ol), wb_ref[k])
    mix = _dot(merged.astype(BF16), wo_ref[...])
    xn = _group_pick(n_ctx_tiles, xc_ref, xl_ref) + mod_ref[0, 2:3, :] * mix
    xo_ref[...] = xn
    h2 = _modulated_norm(xn, n2_ref[...], mod_ref[0, 3:4, :], mod_ref[0, 4:5, :])
    h2_ref[...] = h2.astype(h2_ref.dtype)


def _merge(x_pair, mod, norm_g, o_pairs, gl, wb, wo, mod_index):
    xc, xl = x_pair
    d = xc.shape[1]
    t = xc.shape[0] + xl.shape[0]
    n_ctx_tiles = xc.shape[0] // ROW_TILE
    row = lambda r: (r, 0)
    o_specs, o_args = [], []
    for pair in o_pairs:
        o_specs += _group_specs(BR_W, n_ctx_tiles)
        o_args += list(pair)
    return pl.pallas_call(
        functools.partial(_merge_kernel, n_ctx_tiles),
        grid=(t // ROW_TILE,),
        in_specs=_group_specs(d, n_ctx_tiles) + [pl.BlockSpec((1, 6, d), mod_index),
                                                 pl.BlockSpec((1, d), lambda r: (0, 0))] + o_specs + [
                  pl.BlockSpec((ROW_TILE, N_BRANCH * d), row),
                  pl.BlockSpec(wb.shape, lambda r: (0, 0, 0), pipeline_mode=pl.Buffered(1)),
                  pl.BlockSpec(wo.shape, lambda r: (0, 0), pipeline_mode=pl.Buffered(1))],
        out_specs=[pl.BlockSpec((ROW_TILE, d), row),
                   pl.BlockSpec((ROW_TILE, d), row)],
        out_shape=[jax.ShapeDtypeStruct((t, d), F32),
                   jax.ShapeDtypeStruct((t, d), BF16)],
        compiler_params=_CP(dimension_semantics=("arbitrary",)),
        name="merge_outproj",
    )(xc, xl, mod, norm_g.reshape(1, d), *o_args, gl, wb, wo)


def _router_kernel(x_ref, mod_ref, n2_ref, rwt_ref, rb_ref, rloc_ref, w_ref, cnt_ref):
    tm = x_ref.shape[0]
    h2 = _modulated_norm(x_ref[...], n2_ref[...], mod_ref[0, 3:4, :], mod_ref[0, 4:5, :])
    logits = lax.dot_general(rwt_ref[...], h2, (((1,), (1,)), ((), ())),
                             preferred_element_type=F32, precision=HIGHEST)
    scores = _sigmoid(logits)
    biased = scores + rb_ref[...]
    gsz = N_EXPERTS // N_GROUPS
    neg_inf = -jnp.inf

    b3 = biased.reshape(N_GROUPS, gsz, tm)
    e_in_g = lax.broadcasted_iota(jnp.int32, (N_GROUPS, gsz, tm), 1)
    m1 = jnp.max(b3, axis=1, keepdims=True)
    first = jnp.min(jnp.where(b3 == m1, e_in_g, gsz), axis=1, keepdims=True)
    m2 = jnp.max(jnp.where(e_in_g == first, neg_inf, b3), axis=1, keepdims=True)
    gscore = m1 + m2

    g_iota = lax.broadcasted_iota(jnp.int32, (N_GROUPS, 1, tm), 0)
    chosen = jnp.zeros((N_GROUPS, 1, tm), jnp.int32)
    for _ in range(TOPK_GROUPS):
        m = jnp.max(gscore, axis=0, keepdims=True)
        first = jnp.min(jnp.where(gscore == m, g_iota, N_GROUPS), axis=0, keepdims=True)
        hit = g_iota == first
        chosen = jnp.where(hit, 1, chosen)
        gscore = jnp.where(hit, neg_inf, gscore)
    emask = jnp.broadcast_to(chosen, (N_GROUPS, gsz, tm)).reshape(N_EXPERTS, tm)

    cand = jnp.where(emask > 0, biased, neg_inf)
    e_iota = _iota2((N_EXPERTS, tm), 0)
    hits, ws = [], []
    for _ in range(TOP_K):
        m = jnp.max(cand, axis=0, keepdims=True)
        first = jnp.min(jnp.where(cand == m, e_iota, N_EXPERTS), axis=0, keepdims=True)
        hit = e_iota == first
        hits.append(hit)
        ws.append(jnp.sum(jnp.where(hit, scores, 0.0), axis=0, keepdims=True))
        cand = jnp.where(hit, neg_inf, cand)
    wsum = ws[0]
    for w in ws[1:]:
        wsum = wsum + w
    pad = SUBLANES - TOP_K
    w_ref[...] = jnp.concatenate([ROUTED_SCALE * w / wsum for w in ws] + [jnp.zeros((pad, tm), F32)], axis=0)

    picked = jnp.zeros((N_EXPERTS, tm), F32)
    for hit in hits:
        picked = jnp.where(hit, 1.0, picked)
    picked = picked.astype(BF16)
    earlier_tok = jnp.where(_iota2((tm, tm), 0) < _iota2((tm, tm), 1), 1.0, 0.0).astype(BF16)
    before_in_expert = _dot(picked, earlier_tok)
    count_rep = _dot(picked, jnp.ones((tm, tm), BF16))
    lower_expert = jnp.where(_iota2((N_EXPERTS, N_EXPERTS), 1) < _iota2((N_EXPERTS, N_EXPERTS), 0), 1.0, 0.0)
    pos = _dot(lower_expert.astype(BF16), count_rep.astype(BF16)) + before_in_expert
    rloc = [jnp.sum(jnp.where(hit, pos, 0.0), axis=0, keepdims=True).astype(jnp.int32) for hit in hits]
    rloc_ref[...] = jnp.concatenate(rloc + [jnp.zeros((pad, tm), jnp.int32)], axis=0)
    cnt_ref[0] = _dot_nt(jnp.ones((SUBLANES, tm), BF16), picked)


def _router(x_new, mod, norm_g, rwt, rb, mod_index):
    t, d = x_new.shape
    n_tiles = t // ROW_TILE
    return pl.pallas_call(
        _router_kernel,
        grid=(n_tiles,),
        in_specs=[pl.BlockSpec((ROW_TILE, d), lambda r: (r, 0)),
                  pl.BlockSpec((1, 6, d), mod_index),
                  pl.BlockSpec((1, d), lambda r: (0, 0)),
                  pl.BlockSpec(rwt.shape, lambda r: (0, 0)),
                  pl.BlockSpec(rb.shape, lambda r: (0, 0))],
        out_specs=[pl.BlockSpec((SUBLANES, ROW_TILE), lambda r: (0, r)),
                   pl.BlockSpec((SUBLANES, ROW_TILE), lambda r: (0, r)),
                   pl.BlockSpec((1, SUBLANES, N_EXPERTS), lambda r: (r, 0, 0))],
        out_shape=[jax.ShapeDtypeStruct((SUBLANES, t), jnp.int32),
                   jax.ShapeDtypeStruct((SUBLANES, t), F32),
                   jax.ShapeDtypeStruct((n_tiles, SUBLANES, N_EXPERTS), F32)],
        compiler_params=_CP(dimension_semantics=("arbitrary",)),
        name="router",
    )(x_new, mod, norm_g.reshape(1, d), rwt, rb)


SEG_ROWS = 16
TILE_SLOTS = ROW_TILE * TOP_K


def _segment_copies(cnt_ref, lstart_ref, goff_ref, make_copy):
    def rows(first, n_rows):
        return pl.ds(pl.multiple_of(first * SLAB, SLAB), n_rows * SLAB)

    def per_expert(e, carry):
        n = cnt_ref[0, 0, e]
        loc = lstart_ref[0, 0, e]
        glob = goff_ref[0, 0, e]

        def run(j, c):
            make_copy(rows(loc + j * SEG_ROWS, SEG_ROWS), rows(glob + j * SEG_ROWS, SEG_ROWS)).start()
            return c

        lax.fori_loop(0, n // SEG_ROWS, run, 0)
        done = n // SEG_ROWS * SEG_ROWS
        bit = SEG_ROWS // 2
        while bit:
            take = jnp.bitwise_and(n, bit)

            @pl.when(take != 0)
            def _(done=done, bit=bit):
                make_copy(rows(loc + done, bit), rows(glob + done, bit)).start()

            done = done + take
            bit //= 2
        return carry

    lax.fori_loop(0, N_EXPERTS, per_expert, 0)


def _dispatch_kernel(h2_ref, rloc_ref, cnt_ref, lstart_ref, goff_ref, xs_hbm, buf, sem):
    i = pl.program_id(0)
    n_tiles = pl.num_programs(0)
    slot = i % 2
    tm = h2_ref.shape[0]

    def all_copies(s):
        return pltpu.make_async_copy(buf.at[s], xs_hbm.at[pl.ds(0, TILE_SLOTS * SLAB)], sem.at[s])

    @pl.when(i >= 2)
    def _():
        all_copies(slot).wait()

    rid = _iota2((TILE_SLOTS, tm), 0)
    perm = jnp.zeros((TILE_SLOTS, tm), F32)
    for k in range(TOP_K):
        perm = jnp.where(rid == rloc_ref[k:k + 1, :], 1.0, perm)
    perm = perm.astype(BF16)
    for s in range(0, SLAB, 2):
        xs = _dot(perm, h2_ref[:, s * LANES:(s + 2) * LANES])
        buf[slot, pl.ds(s, TILE_SLOTS, stride=SLAB), :] = xs[:, :LANES]
        buf[slot, pl.ds(s + 1, TILE_SLOTS, stride=SLAB), :] = xs[:, LANES:]

    _segment_copies(cnt_ref, lstart_ref, goff_ref,
                    lambda loc, glob: pltpu.make_async_copy(buf.at[slot, loc], xs_hbm.at[glob], sem.at[slot]))

    @pl.when(i == n_tiles - 1)
    def _():
        all_copies(slot).wait()

        @pl.when(n_tiles >= 2)
        def _():
            all_copies(1 - slot).wait()


def _dispatch(h2, rloc, cnt, lstart, goff):
    t, d = h2.shape
    n_tiles = t // ROW_TILE
    smem_tile = functools.partial(pl.BlockSpec, (1, 1, N_EXPERTS), lambda i: (i, 0, 0), memory_space=pltpu.SMEM)
    return pl.pallas_call(
        _dispatch_kernel,
        grid=(n_tiles,),
        in_specs=[pl.BlockSpec((ROW_TILE, d), lambda i: (i, 0)),
                  pl.BlockSpec((SUBLANES, ROW_TILE), lambda i: (0, i)),
                  smem_tile(), smem_tile(), smem_tile()],
        out_specs=pl.BlockSpec(memory_space=pl.ANY),
        out_shape=jax.ShapeDtypeStruct((t * TOP_K * SLAB, LANES), F32),
        scratch_shapes=[pltpu.VMEM((2, TILE_SLOTS * SLAB, LANES), F32),
                        pltpu.SemaphoreType.DMA((2,))],
        compiler_params=_CP(dimension_semantics=("arbitrary",)),
        name="moe_dispatch",
    )(h2, rloc, cnt, lstart, goff)


def _expert_kernel(blk_ref, exp_ref, lo_ref, hi_ref, n_ref, x_ref, wg_ref, wu_ref, wdn_ref, y_ref,
                   wgu_s, wd_s):
    i = pl.program_id(0)
    prev = jnp.maximum(i - 1, 0)
    valid = i < n_ref[0]
    new_expert = jnp.logical_or(i == 0, exp_ref[i] != exp_ref[prev])
    first_of_block = jnp.logical_or(i == 0, blk_ref[i] != blk_ref[prev])

    @pl.when(jnp.logical_and(valid, new_expert))
    def _():
        wgu_s[:, :D_EXPERT] = wg_ref[0, 0].astype(BF16)
        wgu_s[:, D_EXPERT:] = wu_ref[0, 0].astype(BF16)
        wd_s[...] = wdn_ref[0, 0].astype(BF16)

    @pl.when(valid)
    def _():
        lhs = jnp.concatenate([x_ref[pl.ds(k, MOE_ROWS, stride=SLAB), :] for k in range(SLAB)], axis=1)
        gu = _dot(lhs.astype(BF16), wgu_s[...])
        act = _silu(gu[:, :D_EXPERT]) * gu[:, D_EXPERT:]
        y = _dot(act.astype(BF16), wd_s[...])
        rows = _iota2((MOE_ROWS, LANES), 0)
        mine = jnp.logical_and(rows >= lo_ref[i], rows < hi_ref[i])

        @pl.when(first_of_block)
        def _():
            for k in range(SLAB):
                y_ref[pl.ds(k, MOE_ROWS, stride=SLAB), :] = jnp.where(mine, y[:, k * LANES:(k + 1) * LANES], 0.0)

        @pl.when(jnp.logical_not(first_of_block))
        def _():
            for k in range(SLAB):
                rows_k = pl.ds(k, MOE_ROWS, stride=SLAB)
                y_ref[rows_k, :] = jnp.where(mine, y[:, k * LANES:(k + 1) * LANES], y_ref[rows_k, :])


def _experts(xs, items, layer, wg, wu, wdn):
    blk, exp, lo, hi, n_items = items
    by_block = lambda i, blk_r, exp_r, lo_r, hi_r, n_r: (blk_r[i], 0)
    by_expert = lambda i, blk_r, exp_r, lo_r, hi_r, n_r: (layer, exp_r[i], 0, 0)
    grid_spec = pltpu.PrefetchScalarGridSpec(
        num_scalar_prefetch=5,
        grid=(blk.shape[0],),
        in_specs=[pl.BlockSpec((MOE_ROWS * SLAB, LANES), by_block),
                  pl.BlockSpec((1, 1) + wg.shape[2:], by_expert),
                  pl.BlockSpec((1, 1) + wu.shape[2:], by_expert),
                  pl.BlockSpec((1, 1) + wdn.shape[2:], by_expert)],
        out_specs=pl.BlockSpec((MOE_ROWS * SLAB, LANES), by_block),
        scratch_shapes=[pltpu.VMEM((wg.shape[2], 2 * D_EXPERT), BF16),
                        pltpu.VMEM(wdn.shape[2:], BF16)])
    return pl.pallas_call(
        _expert_kernel,
        grid_spec=grid_spec,
        out_shape=jax.ShapeDtypeStruct(xs.shape, F32),
        compiler_params=_CP(dimension_semantics=("arbitrary",)),
        name="routed_experts",
    )(blk, exp, lo, hi, n_items, xs, wg, wu, wdn)


def _moe_plan(cnt_tiles, t):
    p = t * TOP_K
    assert p % MOE_ROWS == 0
    cnt = cnt_tiles[:, 0, :].astype(jnp.int32)
    totals = jnp.sum(cnt, axis=0)
    ends = jnp.cumsum(totals)
    starts = ends - totals
    goff = starts[None, :] + jnp.cumsum(cnt, axis=0) - cnt
    lstart = jnp.cumsum(cnt, axis=1) - cnt
    n_blk = jnp.where(totals > 0, (ends - 1) // MOE_ROWS - starts // MOE_ROWS + 1, 0)
    item_end = jnp.cumsum(n_blk)
    n_items = item_end[-1]
    max_items = p // MOE_ROWS + N_EXPERTS
    it = jnp.minimum(jnp.arange(max_items, dtype=jnp.int32), jnp.maximum(n_items - 1, 0))
    exp = jnp.minimum(jnp.sum((item_end[None, :] <= it[:, None]).astype(jnp.int32), axis=1), N_EXPERTS - 1)
    sel = (exp[:, None] == jnp.arange(N_EXPERTS, dtype=jnp.int32)[None, :]).astype(jnp.int32)
    pick = lambda v: jnp.sum(sel * v[None, :], axis=1)
    blk = jnp.clip(pick(starts) // MOE_ROWS + it - pick(item_end - n_blk), 0, p // MOE_ROWS - 1)
    lo = jnp.maximum(pick(starts), blk * MOE_ROWS) - blk * MOE_ROWS
    hi = jnp.minimum(pick(ends), (blk + 1) * MOE_ROWS) - blk * MOE_ROWS
    shape = (cnt.shape[0], 1, N_EXPERTS)
    items = tuple(a.astype(jnp.int32) for a in (blk, exp, lo, hi, n_items.reshape(1)))
    return cnt.reshape(shape), lstart.reshape(shape).astype(jnp.int32), goff.reshape(shape).astype(jnp.int32), items


def _combine_kernel(final, n_ctx_tiles, x_ref, h2_ref, w_ref, rloc_ref, mod_ref, fn_ref, wsgu_ref, wsd_ref,
                    cnt_ref, lstart_ref, goff_ref, cnt_n, lstart_n, goff_n, y_hbm, oc_ref, ol_ref, ybuf, sem):
    i = pl.program_id(0)
    n_tiles = pl.num_programs(0)
    slot = i % 2
    tm = x_ref.shape[0]

    def fetch(refs, s):
        _segment_copies(*refs, lambda loc, glob: pltpu.make_async_copy(y_hbm.at[glob], ybuf.at[s, loc], sem.at[s]))

    @pl.when(i == 0)
    def _():
        fetch((cnt_ref, lstart_ref, goff_ref), 0)

    @pl.when(i + 1 < n_tiles)
    def _():
        fetch((cnt_n, lstart_n, goff_n), 1 - slot)

    gu = _dot(h2_ref[...], wsgu_ref[...])
    shared = _dot((_silu(gu[:, :D_SHARED]) * gu[:, D_SHARED:]).astype(BF16), wsd_ref[...])

    rid = _iota2((tm, TILE_SLOTS), 1)
    w = w_ref[...]
    unsort = jnp.zeros((tm, TILE_SLOTS), F32)
    for k in range(TOP_K):
        unsort = jnp.where(rid == rloc_ref[:, k:k + 1], w[:, k:k + 1], unsort)
    unsort = unsort.astype(BF16)

    pltpu.make_async_copy(y_hbm.at[pl.ds(0, TILE_SLOTS * SLAB)], ybuf.at[slot], sem.at[slot]).wait()
    pieces = [_dot(unsort, ybuf[slot, pl.ds(s, TILE_SLOTS, stride=SLAB), :].astype(BF16)) for s in range(SLAB)]
    routed = jnp.concatenate(pieces, axis=1)
    xo = x_ref[...] + mod_ref[0, 5:6, :] * (routed + shared)
    if final:
        xo = xo * lax.rsqrt(jnp.mean(xo * xo, axis=-1, keepdims=True) + EPS) * fn_ref[...]

    @pl.when(i < n_ctx_tiles)
    def _():
        oc_ref[...] = xo

    @pl.when(i >= n_ctx_tiles)
    def _():
        ol_ref[...] = xo


def _combine(final, n_ctx_tiles, x_new, h2, y, w_tk, rloc_tk, plan, mod, final_norm, wsgu, wsd, mod_index):
    t, d = x_new.shape
    n_tiles = t // ROW_TILE
    row = lambda r: (r, 0)
    cur = lambda r: (r, 0, 0)
    nxt = lambda r: (jnp.minimum(r + 1, n_tiles - 1), 0, 0)
    smem_tile = functools.partial(pl.BlockSpec, (1, 1, N_EXPERTS), memory_space=pltpu.SMEM)
    cnt, lstart, goff = plan
    t_ctx = n_ctx_tiles * ROW_TILE
    return pl.pallas_call(
        functools.partial(_combine_kernel, final, n_ctx_tiles),
        grid=(n_tiles,),
        in_specs=[pl.BlockSpec((ROW_TILE, d), row),
                  pl.BlockSpec((ROW_TILE, d), row),
                  pl.BlockSpec((ROW_TILE, SUBLANES), row),
                  pl.BlockSpec((ROW_TILE, SUBLANES), row),
                  pl.BlockSpec((1, 6, d), mod_index),
                  pl.BlockSpec((1, d), lambda r: (0, 0)),
                  pl.BlockSpec(wsgu.shape, lambda r: (0, 0)),
                  pl.BlockSpec(wsd.shape, lambda r: (0, 0)),
                  smem_tile(cur), smem_tile(cur), smem_tile(cur),
                  smem_tile(nxt), smem_tile(nxt), smem_tile(nxt),
                  pl.BlockSpec(memory_space=pl.ANY)],
        out_specs=_group_specs(d, n_ctx_tiles),
        out_shape=[jax.ShapeDtypeStruct((t_ctx, d), F32), jax.ShapeDtypeStruct((t - t_ctx, d), F32)],
        scratch_shapes=[pltpu.VMEM((2, TILE_SLOTS * SLAB, LANES), F32),
                        pltpu.SemaphoreType.DMA((2,))],
        compiler_params=_CP(dimension_semantics=("arbitrary",)),
        name="combine_shared",
    )(x_new, h2, w_tk, rloc_tk, mod, final_norm.reshape(1, d), wsgu, wsd,
      cnt, lstart, goff, cnt, lstart, goff, y)


def _qk_perm():
    half = DK_R // 2
    perm = np.zeros(RET_QK_W, np.int32)
    for part in range(2):
        for h in range(H_R):
            for j in range(half):
                perm[part * LANES + h * half + j] = h * DK_R + part * half + j
    return perm


def _ret_state_rows():
    half = DK_R // 2
    rows = np.zeros((H_R, DK_R), np.int32)
    for h in range(H_R):
        for j in range(DK_R):
            rows[h, j] = (j // half) * LANES + h * half + j % half
    return rows


def _xbc_channels():
    ch = list(range(BR_W))
    for base in (BR_W, BR_W + G_M * N_M):
        for g in range(G_M):
            grp = list(range(base + g * N_M, base + (g + 1) * N_M))
            ch += grp + grp
    return np.asarray(ch, np.int32)


def _pad_lanes(v):
    return jnp.zeros((1, LANES), F32).at[0, :v.shape[0]].set(v.astype(F32))


def _ret_state_pack(s):
    rows = _ret_state_rows()
    src = np.full((H_R, 2 * LANES), DK_R, np.int32)
    for h in range(H_R):
        src[h, rows[h]] = np.arange(DK_R)
    sz = jnp.concatenate([s, jnp.zeros(s.shape[:3] + (1, DV_R), s.dtype)], axis=3)
    return sz[:, :, np.arange(H_R)[:, None], src, :]


def _ssm_state_pack(s):
    b = s.shape[0]
    sr = s.reshape(b, 2, H_M // 2, 2, N_M, 1, P_M)
    eye = jnp.eye(2, dtype=s.dtype).reshape(1, 1, 1, 2, 1, 2, 1)
    return (sr * eye).reshape(b, 2, H_M // 2, 2 * N_M, 2 * P_M)


def _grid_rope(l):
    rows = l // GRID_W
    row = jnp.repeat(jnp.arange(rows), GRID_W).astype(F32)
    col = (jnp.arange(rows * GRID_W) % GRID_W).astype(F32)
    n_freq = DK_R // 4
    freqs = ROPE_BASE ** (-jnp.arange(n_freq, dtype=F32) / n_freq)
    ang = jnp.concatenate([row[:, None] * freqs, col[:, None] * freqs], axis=-1)
    return jnp.tile(jnp.cos(ang), (1, H_R)), jnp.tile(jnp.sin(ang), (1, H_R))


def _layer_weights(i, w_in, ssm_conv_w, ssm_conv_b):
    cuts = np.cumsum(IN_SPLITS)[:-1]
    rq, rk, rv, rg, sz, sxbc, sdt, hq, hf, hi, hg, gl = jnp.split(w_in[i], cuts, axis=1)
    perm = _qk_perm()
    ch = _xbc_channels()
    w_ret = jnp.concatenate([rq[:, perm], rk[:, perm] * (DK_R ** -0.5), rv, rg], axis=1).astype(BF16)
    w_ssm = jnp.concatenate([sz, sxbc[:, ch]], axis=1).astype(BF16)
    w_hg = jnp.concatenate([hq, hi, hg], axis=1).astype(BF16)
    w_prec = jnp.concatenate([hf, sdt, jnp.zeros((w_in.shape[1], LANES - SSM_DT_W), F32)], axis=1).astype(BF16)
    conv_w = ssm_conv_w[i][ch].T
    conv_b = ssm_conv_b[i][ch][None, :]
    return (w_ret, w_ssm, w_hg, gl.astype(BF16), w_prec), conv_w, conv_b


def kernel(x_prompt, x_sample, state_ret, state_ssm, state_hgrn, c, c_ctx, ada_w, ada_b, norm1, norm2, w_in,
           ret_decay_logit, ret_gn, ssm_conv_w, ssm_conv_b, ssm_a_log, ssm_dt_bias, ssm_d, ssm_norm,
           hgrn_lb_logits, hgrn_norm, w_branch, w_out, router_w, router_b, exp_w_gate, exp_w_up,
           exp_w_down, sh_w_gate, sh_w_up, sh_w_down, final_norm):
    bc, lc, d = x_prompt.shape
    bl, ll, _ = x_sample.shape
    tc, tl = bc * lc, bl * ll
    t = tc + tl
    assert tc % ROW_TILE == 0 and ll % ROW_TILE == 0 and lc % CHUNK == 0 and ll % CHUNK == 0
    assert tl % lc == 0 and tc % ll == 0 and 1 + bl <= SUBLANES
    mod_index = _mod_index(tc // ROW_TILE, ll // ROW_TILE)

    n_ctx_tiles = tc // ROW_TILE
    x_pair = (x_prompt.reshape(tc, d), x_sample.reshape(tl, d))
    c_all = jnp.concatenate([c_ctx[None, :], c, jnp.zeros((SUBLANES - 1 - bl, d), F32)], axis=0)
    mods = _ada(c_all, ada_w, ada_b)

    lb_all = jnp.cumsum(jax.nn.softmax(hgrn_lb_logits.astype(F32), axis=1), axis=1)
    lb_all = lb_all - lb_all[:, :1]
    cos4, sin4 = _grid_rope(ll)
    lane_head = (np.arange(2 * LANES) % LANES) // (DK_R // 2)

    new_ret, new_ssm, new_hg = [], [], []
    for i in range(DEPTH):
        mod = mods[i].reshape(SUBLANES, 6, d)
        weights, conv_w, conv_b = _layer_weights(i, w_in, ssm_conv_w, ssm_conv_b)
        ret_in, ssm_in, hg_in, gl, prec = _inproj(x_pair, mod, norm1[i], weights, (BF16, BF16, BF16, BF16, F32),
                                                  mod_index)

        lg = jax.nn.log_sigmoid(ret_decay_logit[i].astype(F32))
        lgl = lg[:, lane_head]
        lgv = jnp.broadcast_to(lg[:, :, None], (2, H_R, LANES))
        gn = ret_gn[i][None, :]
        o_ret_c, s_ret = _retention(ret_in, 0, bc, lc, None, None, lgl, lgv, gn, None)
        o_ret_l, = _retention(ret_in, tc, bl, ll, cos4, sin4, lgl, lgv, gn, _ret_state_pack(state_ret[:, i]))
        new_ret.append(s_ret)

        dtb = _pad_lanes(ssm_dt_bias[i].reshape(-1))
        na = _pad_lanes(-jnp.exp(ssm_a_log[i].astype(F32)).reshape(-1))
        dsk = jnp.repeat(ssm_d[i], P_M)[None, :]
        ng = ssm_norm[i][None, :]
        o_ssm_c, s_ssm = _ssd(ssm_in, prec, 0, bc, lc, conv_w, conv_b, dtb, na, dsk, ng, None)
        o_ssm_l, = _ssd(ssm_in, prec, tc, bl, ll, conv_w, conv_b, dtb, na, dsk, ng,
                        _ssm_state_pack(state_ssm[:, i]))
        new_ssm.append(s_ssm)

        lb = lb_all[:, i]
        llb = jnp.log(lb).reshape(1, 2 * BR_W)
        oml = (1.0 - lb).reshape(1, 2 * BR_W)
        l1m = jnp.log1p(-lb).reshape(1, 2 * BR_W)
        hn = hgrn_norm[i][None, :]
        o_hg_c, s_hg = _hgrn(hg_in, prec, 0, bc, lc, llb, oml, l1m, hn, None)
        o_hg_l, = _hgrn(hg_in, prec, tc, bl, ll, llb, oml, l1m, hn, state_hgrn[:, i])
        new_hg.append(s_hg)

        x_new, h2 = _merge(x_pair, mod, norm2[i], ((o_ret_c, o_ret_l), (o_ssm_c, o_ssm_l), (o_hg_c, o_hg_l)), gl,
                           w_branch[i].astype(BF16), w_out[i].astype(BF16), mod_index)

        rloc, w8, cnt_tiles = _router(x_new, mod, norm2[i], router_w[i].T, router_b[i][:, None], mod_index)
        cnt, lstart, goff, items = _moe_plan(cnt_tiles, t)
        xs = _dispatch(h2, rloc, cnt, lstart, goff)
        y = _experts(xs, items, i, exp_w_gate, exp_w_up, exp_w_down)
        wsgu = jnp.concatenate([sh_w_gate[i], sh_w_up[i]], axis=-1).astype(BF16)
        x_pair = _combine(i == DEPTH - 1, n_ctx_tiles, x_new, h2, y, w8.T, rloc.T, (cnt, lstart, goff), mod,
                          final_norm, wsgu, sh_w_down[i].astype(BF16), mod_index)

    y_prompt = x_pair[0].reshape(bc, lc, d)
    y_sample = x_pair[1].reshape(bl, ll, d)
    return (y_prompt, y_sample, jnp.stack(new_ret, axis=1), jnp.stack(new_ssm, axis=1),
            jnp.stack(new_hg, axis=1))
```

```python
import functools

import numpy as np
import jax
import jax.numpy as jnp
from jax import lax
from jax.experimental import pallas as pl
from jax.experimental.pallas import tpu as pltpu

F32 = jnp.float32
BF16 = jnp.bfloat16
HIGHEST = lax.Precision.HIGHEST

D_MODEL = 1024
DEPTH = 2
GRID_W = 64
H_R, DK_R, DV_R = 4, 64, 128
H_M, P_M, N_M, G_M, D_CONV = 8, 64, 64, 2, 5
H_C, E_C, DV_C = 4, 128, 128
BR_W = 512
N_BRANCH = 3
RET_QK_W = H_R * DK_R
CONV_CH = BR_W + 2 * G_M * N_M
SSM_DT_W = 2 * H_M
IN_SPLITS = (RET_QK_W, RET_QK_W, BR_W, BR_W, BR_W, CONV_CH, SSM_DT_W, BR_W, 2 * BR_W, BR_W, BR_W,
             N_BRANCH * D_MODEL)
N_EXPERTS, TOP_K, N_GROUPS, TOPK_GROUPS = 64, 6, 8, 4
D_EXPERT = 256
D_SHARED = 256
ROUTED_SCALE = 2.5
EPS = 1e-6
ROPE_BASE = 10000.0

LANES = 128
SUBLANES = 8
SLAB = D_MODEL // LANES
CHUNK = 128
ROW_TILE = 256
MOE_ROWS = 256
VMEM_LIMIT = 56 * 1024 * 1024
NEG_BIG = -1e30

_CP = functools.partial(pltpu.CompilerParams, vmem_limit_bytes=VMEM_LIMIT)


def _sigmoid(x):
    return 1.0 / (1.0 + jnp.exp(-x))


def _silu(x):
    return x * _sigmoid(x)


def _softplus(x):
    return jnp.maximum(x, 0.0) + jnp.log1p(jnp.exp(-jnp.abs(x)))


def _log_sigmoid(x):
    return jnp.minimum(x, 0.0) - jnp.log1p(jnp.exp(-jnp.abs(x)))


def _dot(a, b):
    return jnp.dot(a, b, preferred_element_type=F32)


def _dot_nt(a, b):
    return lax.dot_general(a, b, (((1,), (1,)), ((), ())), preferred_element_type=F32)


def _dot_tn(a, b):
    return lax.dot_general(a, b, (((0,), (0,)), ((), ())), preferred_element_type=F32)


def _tri_dot(tri, x):
    hi = x.astype(BF16)
    r1 = x - hi.astype(F32)
    mid = r1.astype(BF16)
    lo = (r1 - mid.astype(F32)).astype(BF16)
    return _dot(tri, hi) + _dot(tri, mid) + _dot(tri, lo)


def _iota2(shape, dim):
    return lax.broadcasted_iota(jnp.int32, shape, dim)


def _rows(c):
    return pl.ds(pl.multiple_of(c * CHUNK, CHUNK), CHUNK)


def _const_spec(shape):
    nd = len(shape)
    return pl.BlockSpec(shape, lambda *_: (0,) * nd)


def _ada_kernel(c_ref, w_ref, b_ref, o_ref):
    a = _silu(c_ref[...])
    o_ref[0] = jnp.dot(a, w_ref[0], preferred_element_type=F32, precision=HIGHEST) + b_ref[0]


def _ada(c_all, ada_w, ada_b):
    depth, d, n = ada_w.shape
    tn = 1536
    return pl.pallas_call(
        _ada_kernel,
        grid=(depth, n // tn),
        in_specs=[pl.BlockSpec((SUBLANES, d), lambda i, j: (0, 0)),
                  pl.BlockSpec((1, d, tn), lambda i, j: (i, 0, j)),
                  pl.BlockSpec((1, 1, tn), lambda i, j: (i, 0, j))],
        out_specs=pl.BlockSpec((1, SUBLANES, tn), lambda i, j: (i, 0, j)),
        out_shape=jax.ShapeDtypeStruct((depth, SUBLANES, n), F32),
        compiler_params=_CP(dimension_semantics=("arbitrary", "arbitrary")),
        name="ada_mod",
    )(c_all, ada_w, ada_b.reshape(depth, 1, n))


def _mod_index(n_ctx_tiles, tiles_per_seq):
    def index(r):
        return (jnp.where(r < n_ctx_tiles, 0, 1 + (r - n_ctx_tiles) // tiles_per_seq), 0, 0)
    return index


def _modulated_norm(x, g, shift, scale):
    y = x * lax.rsqrt(jnp.mean(x * x, axis=-1, keepdims=True) + EPS) * g
    return y * (1.0 + scale) + shift


def _group_specs(cols, n_ctx_tiles):
    return [pl.BlockSpec((ROW_TILE, cols), lambda r: (jnp.minimum(r, n_ctx_tiles - 1), 0)),
            pl.BlockSpec((ROW_TILE, cols), lambda r: (jnp.maximum(r - n_ctx_tiles, 0), 0))]


def _group_pick(n_ctx_tiles, ctx_ref, lat_ref):
    return jnp.where(pl.program_id(0) < n_ctx_tiles, ctx_ref[...], lat_ref[...])


def _inproj_kernel(n_ctx_tiles, xc_ref, xl_ref, mod_ref, n_ref, w0, w1, w2, w3, w4, o0, o1, o2, o3, o4):
    x = _group_pick(n_ctx_tiles, xc_ref, xl_ref)
    h = _modulated_norm(x, n_ref[...], mod_ref[0, 0:1, :], mod_ref[0, 1:2, :]).astype(BF16)
    for w, o in ((w0, o0), (w1, o1), (w2, o2), (w3, o3), (w4, o4)):
        o[...] = _dot(h, w[...]).astype(o.dtype)


def _inproj(x_pair, mod, norm_g, weights, out_dtypes, mod_index):
    xc, xl = x_pair
    d = xc.shape[1]
    t = xc.shape[0] + xl.shape[0]
    n_ctx_tiles = xc.shape[0] // ROW_TILE
    w_specs = [pl.BlockSpec(w.shape, lambda r: (0, 0), pipeline_mode=pl.Buffered(1)) for w in weights]
    return pl.pallas_call(
        functools.partial(_inproj_kernel, n_ctx_tiles),
        grid=(t // ROW_TILE,),
        in_specs=_group_specs(d, n_ctx_tiles) + [pl.BlockSpec((1, 6, d), mod_index),
                                                 pl.BlockSpec((1, d), lambda r: (0, 0))] + w_specs,
        out_specs=[pl.BlockSpec((ROW_TILE, w.shape[1]), lambda r: (r, 0)) for w in weights],
        out_shape=[jax.ShapeDtypeStruct((t, w.shape[1]), dt) for w, dt in zip(weights, out_dtypes)],
        compiler_params=_CP(dimension_semantics=("arbitrary",)),
        name="norm_inproj",
    )(xc, xl, mod, norm_g.reshape(1, d), *weights)


def _ret_kernel(is_ctx, n_chunks, in_ref, cos_ref, sin_ref, lgl_ref, lgv_ref, gn_ref, *rest):
    if is_ctx:
        o_ref, sfin_ref, qk_s, oacc, st, dm_s = rest
    else:
        s0_ref, o_ref, qk_s, oacc, st, dm_s = rest
    use_rope = not is_ctx
    qw = 2 * LANES
    ii = _iota2((CHUNK, CHUNK), 0)
    jj = _iota2((CHUNK, CHUNK), 1)
    dist = jnp.abs(ii - jj).astype(F32)
    for h in range(H_R):
        dm_s[h] = (jnp.where(ii >= jj, jnp.exp(dist * lgv_ref[0, h:h + 1, :]), 0.0)
                   + jnp.where(jj >= ii, jnp.exp(dist * lgv_ref[1, h:h + 1, :]), 0.0))
    st[...] = jnp.zeros(st.shape, F32) if is_ctx else s0_ref[0]

    lane_head = (_iota2((1, qw), 1) % LANES) // (DK_R // 2)
    rr = _iota2((CHUNK, qw), 0).astype(F32)
    lg_f = lgl_ref[0:1, :]
    lg_b = lgl_ref[1:2, :]

    def rope(x, cs, sn):
        x1, x2 = x[:, :LANES], x[:, LANES:]
        return jnp.concatenate([x1 * cs - x2 * sn, x1 * sn + x2 * cs], axis=1)

    def fwd(c, carry):
        rows = _rows(c)
        q = in_ref[rows, 0:qw].astype(F32)
        k = in_ref[rows, qw:2 * qw].astype(F32)
        if use_rope:
            cs, sn = cos_ref[rows, :], sin_ref[rows, :]
            q, k = rope(q, cs, sn), rope(k, cs, sn)
        qk_s[rows, 0:qw] = q
        qk_s[rows, qw:2 * qw] = k
        kb = k.astype(BF16)
        q_dec = (q * jnp.exp((rr + 1.0) * lg_f)).astype(BF16)
        k_dec = k * jnp.exp((CHUNK - 1.0 - rr) * lg_f)
        for h in range(H_R):
            hs = slice(h * DV_R, (h + 1) * DV_R)
            mh = lane_head == h
            vh = in_ref[rows, 2 * qw + h * DV_R:2 * qw + (h + 1) * DV_R]
            s = _dot_nt(jnp.where(mh, q, 0.0).astype(BF16), kb)
            intra = _dot((s * dm_s[h]).astype(BF16), vh)
            sf = st[0, h]
            oacc[rows, hs] = intra + _dot(q_dec, sf.astype(BF16))
            st[0, h] = (sf * jnp.exp(CHUNK * lgv_ref[0, h:h + 1, :])
                        + _dot_tn(jnp.where(mh, k_dec, 0.0).astype(BF16), vh))
        return carry

    lax.fori_loop(0, n_chunks, fwd, 0)

    def bwd(t, carry):
        rows = _rows(n_chunks - 1 - t)
        q = qk_s[rows, 0:qw]
        k = qk_s[rows, qw:2 * qw]
        q_dec = (q * jnp.exp((CHUNK - rr) * lg_b)).astype(BF16)
        k_dec = k * jnp.exp(rr * lg_b)
        for h in range(H_R):
            hs = slice(h * DV_R, (h + 1) * DV_R)
            mh = lane_head == h
            vh = in_ref[rows, 2 * qw + h * DV_R:2 * qw + (h + 1) * DV_R]
            sb = st[1, h]
            oacc[rows, hs] = oacc[rows, hs] + _dot(q_dec, sb.astype(BF16))
            st[1, h] = (sb * jnp.exp(CHUNK * lgv_ref[1, h:h + 1, :])
                        + _dot_tn(jnp.where(mh, k_dec, 0.0).astype(BF16), vh))
        return carry

    lax.fori_loop(0, n_chunks, bwd, 0)
    if is_ctx:
        half = DK_R // 2
        for d in range(2):
            for h in range(H_R):
                sfin_ref[0, d, h, 0:half, :] = st[d, h, h * half:(h + 1) * half, :]
                sfin_ref[0, d, h, half:DK_R, :] = st[d, h, LANES + h * half:LANES + (h + 1) * half, :]

    def fin(c, carry):
        rows = _rows(c)
        for h in range(H_R):
            hs = slice(h * DV_R, (h + 1) * DV_R)
            o = oacc[rows, hs]
            oc = o - jnp.mean(o, axis=-1, keepdims=True)
            y = oc * lax.rsqrt(jnp.mean(oc * oc, axis=-1, keepdims=True) + EPS) * gn_ref[:, hs]
            g = in_ref[rows, 4 * qw + h * DV_R:4 * qw + (h + 1) * DV_R].astype(F32)
            o_ref[rows, hs] = (y * _silu(g)).astype(o_ref.dtype)
        return carry

    lax.fori_loop(0, n_chunks, fin, 0)


def _state_io(s0, b, packed_shape, final_shape):
    lead = lambda shape: pl.BlockSpec((1,) + shape, lambda i: (i,) + (0,) * len(shape))
    if s0 is None:
        return [], [], [lead(final_shape)], [jax.ShapeDtypeStruct((b,) + final_shape, F32)]
    return [lead(packed_shape)], [s0], [], []


def _retention(ret_in, row0, b, l, cos4, sin4, lgl, lgv, gn, s0):
    is_ctx = s0 is None
    if is_ctx:
        cos4 = jnp.zeros((SUBLANES, LANES), F32)
        sin4 = cos4
        trig_spec = pl.BlockSpec((SUBLANES, LANES), lambda i: (0, 0))
    else:
        trig_spec = pl.BlockSpec((l, LANES), lambda i: (0, 0))
    blk0 = row0 // l
    packed = (2, H_R, 2 * LANES, DV_R)
    s_in, s_args, s_out, s_shapes = _state_io(s0, b, packed, (2, H_R, DK_R, DV_R))
    return pl.pallas_call(
        functools.partial(_ret_kernel, is_ctx, l // CHUNK),
        grid=(b,),
        in_specs=[pl.BlockSpec((l, ret_in.shape[1]), lambda i: (blk0 + i, 0)),
                  trig_spec, trig_spec,
                  _const_spec(lgl.shape), _const_spec(lgv.shape), _const_spec(gn.shape)] + s_in,
        out_specs=[pl.BlockSpec((l, BR_W), lambda i: (i, 0))] + s_out,
        out_shape=[jax.ShapeDtypeStruct((b * l, BR_W), BF16)] + s_shapes,
        scratch_shapes=[pltpu.VMEM((l, 4 * LANES), F32),
                        pltpu.VMEM((l, BR_W), F32),
                        pltpu.VMEM(packed, F32),
                        pltpu.VMEM((H_R, CHUNK, CHUNK), F32)],
        compiler_params=_CP(dimension_semantics=("arbitrary",)),
        name="retention",
    )(ret_in, cos4, sin4, lgl, lgv, gn, *s_args)


def _ssd_kernel(is_ctx, n_chunks, in_ref, dt_ref, cw_ref, cb_ref, dtb_ref, na_ref, dsk_ref, ng_ref, *rest):
    if is_ctx:
        o_ref, sfin_ref, pad_s, xc_s, y_s, dt_s, cumb_s, st = rest
    else:
        s0_ref, o_ref, pad_s, xc_s, y_s, dt_s, cumb_s, st = rest
    l = in_ref.shape[0]
    cw = 2 * BR_W
    halo = SUBLANES
    pad_s[0:halo, :] = jnp.zeros((halo, cw), F32)
    pad_s[l + halo:l + 2 * halo, :] = jnp.zeros((halo, cw), F32)

    def fill(c, carry):
        rows = _rows(c)
        dst = pl.ds(pl.multiple_of(c * CHUNK + halo, SUBLANES), CHUNK)
        pad_s[dst, :] = in_ref[rows, BR_W:BR_W + cw].astype(F32)
        return carry

    lax.fori_loop(0, n_chunks, fill, 0)
    st[...] = jnp.zeros(st.shape, F32) if is_ctx else s0_ref[0]

    ii = _iota2((CHUNK, CHUNK), 0)
    jj = _iota2((CHUNK, CHUNK), 1)
    tril = jnp.where(ii >= jj, 1.0, 0.0).astype(BF16)
    triu = jnp.where(jj >= ii, 1.0, 0.0).astype(BF16)
    lane_lo = jj < P_M
    row_lo = ii < N_M
    blockdiag = lane_lo == row_lo
    half = D_CONV // 2

    def pair_cols(vals, h0, h1):
        return jnp.where(lane_lo, vals[:, h0:h0 + 1], vals[:, h1:h1 + 1])

    def fwd(c, carry):
        rows = _rows(c)
        win = pad_s[pl.ds(pl.multiple_of(c * CHUNK, CHUNK), CHUNK + 2 * halo), :]
        acc = win[halo - half:halo - half + CHUNK, :] * cw_ref[0:1, :] + cb_ref[...]
        for w in range(1, D_CONV):
            acc = acc + win[halo - half + w:halo - half + w + CHUNK, :] * cw_ref[w:w + 1, :]
        xc = _silu(acc)
        xc_s[rows, :] = xc.astype(xc_s.dtype)
        x = xc[:, 0:BR_W]
        dt = _softplus(dt_ref[rows, :] + dtb_ref[...])
        la = dt * na_ref[...]
        cumf = _tri_dot(tril, la)
        cumb = _tri_dot(triu, la)
        dt_s[rows, :] = dt
        cumb_s[rows, :] = cumb
        cumf_t, cumb_t, dt_t = cumf.T, cumb.T, dt.T
        totf = cumf[CHUNK - 1:CHUNK, :]
        etotf = jnp.exp(totf)
        kdec = dt * jnp.exp(totf - cumf)
        qdec = jnp.exp(cumf)
        for g in range(G_M):
            b2 = xc[:, BR_W + g * LANES:BR_W + (g + 1) * LANES]
            c2 = xc[:, BR_W + (G_M + g) * LANES:BR_W + (G_M + g + 1) * LANES]
            cbm = _dot_nt(jnp.where(lane_lo, c2, 0.0).astype(BF16), b2.astype(BF16))
            for pp in range(2):
                p = 2 * g + pp
                h0, h1 = 2 * p, 2 * p + 1
                ps = slice(p * LANES, (p + 1) * LANES)
                ms = []
                for h in (h0, h1):
                    hb = H_M + h
                    mf = jnp.where(ii >= jj, jnp.exp(cumf[:, h:h + 1] - cumf_t[h:h + 1, :]), 0.0) * dt_t[h:h + 1, :]
                    mb = (jnp.where(jj >= ii, jnp.exp(cumb[:, hb:hb + 1] - cumb_t[hb:hb + 1, :]), 0.0)
                          * dt_t[hb:hb + 1, :])
                    ms.append((cbm * (mf + mb)).astype(BF16))
                xp = x[:, ps]
                xbd = jnp.concatenate([jnp.where(lane_lo, xp, 0.0), jnp.where(lane_lo, 0.0, xp)], axis=0)
                intra = _dot(jnp.concatenate(ms, axis=1), xbd.astype(BF16))
                sf = st[0, p]
                inter = _dot((c2 * pair_cols(qdec, h0, h1)).astype(BF16), sf.astype(BF16))
                y_s[rows, ps] = intra + inter
                kv = _dot_tn((b2 * pair_cols(kdec, h0, h1)).astype(BF16), xp.astype(BF16))
                arow = jnp.where(row_lo, etotf[:, h0:h0 + 1], etotf[:, h1:h1 + 1])
                st[0, p] = sf * arow + jnp.where(blockdiag, kv, 0.0)
        return carry

    lax.fori_loop(0, n_chunks, fwd, 0)

    def bwd(t, carry):
        rows = _rows(n_chunks - 1 - t)
        xc = xc_s[rows, :].astype(F32)
        x = xc[:, 0:BR_W]
        dt = dt_s[rows, :]
        cumb = cumb_s[rows, :]
        totb = cumb[0:1, :]
        etotb = jnp.exp(totb)
        kdec = dt * jnp.exp(totb - cumb)
        qdec = jnp.exp(cumb)
        for g in range(G_M):
            b2 = xc[:, BR_W + g * LANES:BR_W + (g + 1) * LANES]
            c2 = xc[:, BR_W + (G_M + g) * LANES:BR_W + (G_M + g + 1) * LANES]
            for pp in range(2):
                p = 2 * g + pp
                h0, h1 = H_M + 2 * p, H_M + 2 * p + 1
                ps = slice(p * LANES, (p + 1) * LANES)
                xp = x[:, ps]
                sb = st[1, p]
                y_s[rows, ps] = y_s[rows, ps] + _dot((c2 * pair_cols(qdec, h0, h1)).astype(BF16), sb.astype(BF16))
                kv = _dot_tn((b2 * pair_cols(kdec, h0, h1)).astype(BF16), xp.astype(BF16))
                arow = jnp.where(row_lo, etotb[:, h0:h0 + 1], etotb[:, h1:h1 + 1])
                st[1, p] = sb * arow + jnp.where(blockdiag, kv, 0.0)
        return carry

    lax.fori_loop(0, n_chunks, bwd, 0)
    if is_ctx:
        for d in range(2):
            for p in range(H_M // 2):
                pair = st[d, p]
                sfin_ref[0, d, 2 * p] = pair[0:N_M, 0:P_M]
                sfin_ref[0, d, 2 * p + 1] = pair[N_M:2 * N_M, P_M:2 * P_M]

    def fin(c, carry):
        rows = _rows(c)
        x = xc_s[rows, 0:BR_W].astype(F32)
        z = in_ref[rows, 0:BR_W].astype(F32)
        u = (y_s[rows, :] + x * dsk_ref[...]) * _silu(z)
        y = u * lax.rsqrt(jnp.mean(u * u, axis=-1, keepdims=True) + EPS) * ng_ref[...]
        o_ref[rows, :] = y.astype(o_ref.dtype)
        return carry

    lax.fori_loop(0, n_chunks, fin, 0)


def _ssd(ssm_in, prec, row0, b, l, cw, cb, dtb, na, dsk, ng, s0):
    blk0 = row0 // l
    dt_col = (prec.shape[1] - LANES) // LANES
    packed = (2, H_M // 2, LANES, LANES)
    s_in, s_args, s_out, s_shapes = _state_io(s0, b, packed, (2, H_M, N_M, P_M))
    return pl.pallas_call(
        functools.partial(_ssd_kernel, s0 is None, l // CHUNK),
        grid=(b,),
        in_specs=[pl.BlockSpec((l, ssm_in.shape[1]), lambda i: (blk0 + i, 0)),
                  pl.BlockSpec((l, LANES), lambda i: (blk0 + i, dt_col)),
                  _const_spec(cw.shape), _const_spec(cb.shape), _const_spec(dtb.shape),
                  _const_spec(na.shape), _const_spec(dsk.shape), _const_spec(ng.shape)] + s_in,
        out_specs=[pl.BlockSpec((l, BR_W), lambda i: (i, 0))] + s_out,
        out_shape=[jax.ShapeDtypeStruct((b * l, BR_W), BF16)] + s_shapes,
        scratch_shapes=[pltpu.VMEM((l + 2 * SUBLANES, 2 * BR_W), F32),
                        pltpu.VMEM((l, 2 * BR_W), BF16),
                        pltpu.VMEM((l, BR_W), F32),
                        pltpu.VMEM((l, LANES), F32),
                        pltpu.VMEM((l, LANES), F32),
                        pltpu.VMEM(packed, F32)],
        compiler_params=_CP(dimension_semantics=("arbitrary",)),
        name="ssd",
    )(ssm_in, prec, cw, cb, dtb, na, dsk, ng, *s_args)


_HG_LEVELS = (64, 32, 16, 8, 4, 2)
LOG2_E = 1.4426950408889634


def _block_ref_rows(x, m, row):
    size = 2 * m
    if size >= 2 * SUBLANES:
        return _bcast_block_row(x, size, row)
    out = _bcast_group_row(x, row)
    sub = _iota2((CHUNK, LANES), 0) % SUBLANES
    for b in range(1, SUBLANES // size):
        out = jnp.where(sub >= b * size, _bcast_group_row(x, b * size + row), out)
    return out


def _bcast_group_row(x, j):
    x3 = x.reshape(CHUNK // SUBLANES, SUBLANES, LANES)
    r = jnp.broadcast_to(x3[:, j:j + 1, :], x3.shape)
    return r.reshape(CHUNK, LANES)


def _bcast_block_row(x, size, j):
    pieces = [jnp.broadcast_to(x[b * size + j:b * size + j + 1, :], (size, LANES)) for b in range(CHUNK // size)]
    return pieces[0] if len(pieces) == 1 else jnp.concatenate(pieces, axis=0)


def _hgrn_kernel(is_ctx, n_chunks, in_ref, f_ref, llb_ref, oml_ref, l1m_ref, ng_ref, *rest):
    if is_ctx:
        o_ref, sfin_ref, oacc, st, cumb_s, keyb_s = rest
    else:
        s0_ref, o_ref, oacc, st, cumb_s, keyb_s = rest
    ii = _iota2((CHUNK, CHUNK), 0)
    jj = _iota2((CHUNK, CHUNK), 1)
    tril = jnp.where(ii >= jj, 1.0, 0.0).astype(BF16)
    triu = jnp.where(jj >= ii, 1.0, 0.0).astype(BF16)
    odd = (ii % 2) == 1
    same_pair = (ii // 2) == (jj // 2)
    ones_sq = jnp.ones((LANES, CHUNK), BF16)
    if is_ctx:
        st[...] = jnp.zeros(st.shape, F32)
    else:
        for d in range(2):
            for h in range(H_C):
                st[d, h] = s0_ref[0, d, h].T

    def gates(rows, d):
        fr = f_ref[rows, d * BR_W:(d + 1) * BR_W]
        ds_ = slice(d * BR_W, (d + 1) * BR_W)
        a = llb_ref[:, ds_]
        e = jnp.exp(-jnp.abs(fr))
        bterm = l1m_ref[:, ds_] + jnp.minimum(fr, 0.0) - jnp.log1p(e)
        logf = jnp.maximum(a, bterm) + jnp.log1p(jnp.exp(-jnp.abs(a - bterm)))
        key = oml_ref[:, ds_] * (jnp.where(fr >= 0.0, e, 1.0) / (1.0 + e))
        return logf * LOG2_E, key

    def fwd(c, carry):
        rows = _rows(c)
        lgf_all, key_f = gates(rows, 0)
        lgb_all, key_b = gates(rows, 1)
        cumf_all = _tri_dot(tril, lgf_all)
        cumb_all = _tri_dot(triu, lgb_all)
        cumb_s[rows, :] = cumb_all
        keyb_s[rows, :] = key_b
        for h in range(H_C):
            hs = slice(h * LANES, (h + 1) * LANES)
            q = in_ref[rows, hs].astype(F32)
            v = in_ref[rows, BR_W + h * LANES:BR_W + (h + 1) * LANES]
            cumf, cumb = cumf_all[:, hs], cumb_all[:, hs]
            kf, kb = key_f[:, hs], key_b[:, hs]
            sc = jnp.zeros((CHUNK, CHUNK), F32)
            for m in _HG_LEVELS:
                upper = (ii % (2 * m)) >= m
                same_block = (ii // (2 * m)) == (jj // (2 * m))
                ref_f = _block_ref_rows(cumf, m, m - 1)
                ref_b = _block_ref_rows(cumb, m, m)
                e_f = jnp.exp2(jnp.where(upper, cumf - ref_f, ref_f - cumf))
                e_b = jnp.exp2(jnp.where(upper, ref_b - cumb, cumb - ref_b))
                qcat = jnp.concatenate([jnp.where(upper, q * e_f, 0.0), jnp.where(upper, 0.0, q * e_b)], axis=1)
                kcat = jnp.concatenate([jnp.where(upper, 0.0, kf * e_f), jnp.where(upper, kb * e_b, 0.0)], axis=1)
                sc = sc + jnp.where(same_block, _dot_nt(qcat.astype(BF16), kcat.astype(BF16)), 0.0)
            qcat = jnp.concatenate([jnp.where(odd, q * jnp.exp2(lgf_all[:, hs]), 0.0),
                                    jnp.where(odd, 0.0, q * jnp.exp2(lgb_all[:, hs]))], axis=1)
            kcat = jnp.concatenate([jnp.where(odd, 0.0, kf), jnp.where(odd, kb, 0.0)], axis=1)
            sc = sc + jnp.where(same_pair, _dot_nt(qcat.astype(BF16), kcat.astype(BF16)), 0.0)
            sc = sc + jnp.where(ii == jj, _dot((q * (kf + kb)).astype(BF16), ones_sq), 0.0)
            intra = _dot(sc.astype(BF16), v)
            stf = st[0, h]
            totf = cumf[CHUNK - 1:CHUNK, :]
            inter = _dot_nt((q * jnp.exp2(cumf)).astype(BF16), stf.astype(BF16))
            oacc[rows, hs] = intra + inter
            st[0, h] = stf * jnp.exp2(totf) + _dot_tn(v, (kf * jnp.exp2(totf - cumf)).astype(BF16))
        return carry

    lax.fori_loop(0, n_chunks, fwd, 0)

    def bwd(t, carry):
        rows = _rows(n_chunks - 1 - t)
        cumb_all, key_b = cumb_s[rows, :], keyb_s[rows, :]
        for h in range(H_C):
            hs = slice(h * LANES, (h + 1) * LANES)
            q = in_ref[rows, hs].astype(F32)
            v = in_ref[rows, BR_W + h * LANES:BR_W + (h + 1) * LANES]
            cumb, kb = cumb_all[:, hs], key_b[:, hs]
            stb = st[1, h]
            totb = cumb[0:1, :]
            oacc[rows, hs] = oacc[rows, hs] + _dot_nt((q * jnp.exp2(cumb)).astype(BF16), stb.astype(BF16))
            st[1, h] = stb * jnp.exp2(totb) + _dot_tn(v, (kb * jnp.exp2(totb - cumb)).astype(BF16))
        return carry

    lax.fori_loop(0, n_chunks, bwd, 0)
    if is_ctx:
        for d in range(2):
            for h in range(H_C):
                sfin_ref[0, d, h] = st[d, h].T

    def fin(c, carry):
        rows = _rows(c)
        for h in range(H_C):
            hs = slice(h * LANES, (h + 1) * LANES)
            o = oacc[rows, hs]
            y = o * lax.rsqrt(jnp.mean(o * o, axis=-1, keepdims=True) + EPS) * ng_ref[:, hs]
            g = in_ref[rows, 2 * BR_W + h * LANES:2 * BR_W + (h + 1) * LANES].astype(F32)
            o_ref[rows, hs] = (y * _silu(g)).astype(o_ref.dtype)
        return carry

    lax.fori_loop(0, n_chunks, fin, 0)


def _hgrn(hg_in, prec, row0, b, l, llb, oml, l1m, ng, s0):
    blk0 = row0 // l
    s_in, s_args, s_out, s_shapes = _state_io(s0, b, (2, H_C, E_C, DV_C), (2, H_C, E_C, DV_C))
    return pl.pallas_call(
        functools.partial(_hgrn_kernel, s0 is None, l // CHUNK),
        grid=(b,),
        in_specs=[pl.BlockSpec((l, hg_in.shape[1]), lambda i: (blk0 + i, 0)),
                  pl.BlockSpec((l, 2 * BR_W), lambda i: (blk0 + i, 0)),
                  _const_spec(llb.shape), _const_spec(oml.shape), _const_spec(l1m.shape),
                  _const_spec(ng.shape)] + s_in,
        out_specs=[pl.BlockSpec((l, BR_W), lambda i: (i, 0))] + s_out,
        out_shape=[jax.ShapeDtypeStruct((b * l, BR_W), BF16)] + s_shapes,
        scratch_shapes=[pltpu.VMEM((l, BR_W), F32),
                        pltpu.VMEM((2, H_C, DV_C, E_C), F32),
                        pltpu.VMEM((l, BR_W), F32),
                        pltpu.VMEM((l, BR_W), F32)],
        compiler_params=_CP(dimension_semantics=("arbitrary",)),
        name="hgrn2",
    )(hg_in, prec, llb, oml, l1m, ng, *s_args)


def _merge_kernel(n_ctx_tiles, xc_ref, xl_ref, mod_ref, n2_ref, o0c, o0l, o1c, o1l, o2c, o2l, gl_ref, wb_ref, wo_ref,
                  xo_ref, h2_ref):
    d = xc_ref.shape[1]
    merged = jnp.zeros(xc_ref.shape, F32)
    for k, (oc, ol) in enumerate(((o0c, o0l), (o1c, o1l), (o2c, o2l))):
        gate = _sigmoid(gl_ref[:, k * d:(k + 1) * d].astype(F32))
        merged = merged + gate * _dot(_group_pick(n_ctx_tiles, oc, ol), wb_ref[k])
    mix = _dot(merged.astype(BF16), wo_ref[...])
    xn = _group_pick(n_ctx_tiles, xc_ref, xl_ref) + mod_ref[0, 2:3, :] * mix
    xo_ref[...] = xn
    h2 = _modulated_norm(xn, n2_ref[...], mod_ref[0, 3:4, :], mod_ref[0, 4:5, :])
    h2_ref[...] = h2.astype(h2_ref.dtype)


def _merge(x_pair, mod, norm_g, o_pairs, gl, wb, wo, mod_index):
    xc, xl = x_pair
    d = xc.shape[1]
    t = xc.shape[0] + xl.shape[0]
    n_ctx_tiles = xc.shape[0] // ROW_TILE
    row = lambda r: (r, 0)
    o_specs, o_args = [], []
    for pair in o_pairs:
        o_specs += _group_specs(BR_W, n_ctx_tiles)
        o_args += list(pair)
    return pl.pallas_call(
        functools.partial(_merge_kernel, n_ctx_tiles),
        grid=(t // ROW_TILE,),
        in_specs=_group_specs(d, n_ctx_tiles) + [pl.BlockSpec((1, 6, d), mod_index),
                                                 pl.BlockSpec((1, d), lambda r: (0, 0))] + o_specs + [
                  pl.BlockSpec((ROW_TILE, N_BRANCH * d), row),
                  pl.BlockSpec(wb.shape, lambda r: (0, 0, 0), pipeline_mode=pl.Buffered(1)),
                  pl.BlockSpec(wo.shape, lambda r: (0, 0), pipeline_mode=pl.Buffered(1))],
        out_specs=[pl.BlockSpec((ROW_TILE, d), row),
                   pl.BlockSpec((ROW_TILE, d), row)],
        out_shape=[jax.ShapeDtypeStruct((t, d), F32),
                   jax.ShapeDtypeStruct((t, d), BF16)],
        compiler_params=_CP(dimension_semantics=("arbitrary",)),
        name="merge_outproj",
    )(xc, xl, mod, norm_g.reshape(1, d), *o_args, gl, wb, wo)


def _router_kernel(x_ref, mod_ref, n2_ref, rwt_ref, rb_ref, rloc_ref, w_ref, cnt_ref):
    tm = x_ref.shape[0]
    h2 = _modulated_norm(x_ref[...], n2_ref[...], mod_ref[0, 3:4, :], mod_ref[0, 4:5, :])
    logits = lax.dot_general(rwt_ref[...], h2, (((1,), (1,)), ((), ())),
                             preferred_element_type=F32, precision=HIGHEST)
    scores = _sigmoid(logits)
    biased = scores + rb_ref[...]
    gsz = N_EXPERTS // N_GROUPS
    neg_inf = -jnp.inf

    b3 = biased.reshape(N_GROUPS, gsz, tm)
    e_in_g = lax.broadcasted_iota(jnp.int32, (N_GROUPS, gsz, tm), 1)
    m1 = jnp.max(b3, axis=1, keepdims=True)
    first = jnp.min(jnp.where(b3 == m1, e_in_g, gsz), axis=1, keepdims=True)
    m2 = jnp.max(jnp.where(e_in_g == first, neg_inf, b3), axis=1, keepdims=True)
    gscore = m1 + m2

    g_iota = lax.broadcasted_iota(jnp.int32, (N_GROUPS, 1, tm), 0)
    chosen = jnp.zeros((N_GROUPS, 1, tm), jnp.int32)
    for _ in range(TOPK_GROUPS):
        m = jnp.max(gscore, axis=0, keepdims=True)
        first = jnp.min(jnp.where(gscore == m, g_iota, N_GROUPS), axis=0, keepdims=True)
        hit = g_iota == first
        chosen = jnp.where(hit, 1, chosen)
        gscore = jnp.where(hit, neg_inf, gscore)
    emask = jnp.broadcast_to(chosen, (N_GROUPS, gsz, tm)).reshape(N_EXPERTS, tm)

    cand = jnp.where(emask > 0, biased, neg_inf)
    e_iota = _iota2((N_EXPERTS, tm), 0)
    hits, ws = [], []
    for _ in range(TOP_K):
        m = jnp.max(cand, axis=0, keepdims=True)
        first = jnp.min(jnp.where(cand == m, e_iota, N_EXPERTS), axis=0, keepdims=True)
        hit = e_iota == first
        hits.append(hit)
        ws.append(jnp.sum(jnp.where(hit, scores, 0.0), axis=0, keepdims=True))
        cand = jnp.where(hit, neg_inf, cand)
    wsum = ws[0]
    for w in ws[1:]:
        wsum = wsum + w
    pad = SUBLANES - TOP_K
    w_ref[...] = jnp.concatenate([ROUTED_SCALE * w / wsum for w in ws] + [jnp.zeros((pad, tm), F32)], axis=0)

    picked = jnp.zeros((N_EXPERTS, tm), F32)
    for hit in hits:
        picked = jnp.where(hit, 1.0, picked)
    picked = picked.astype(BF16)
    earlier_tok = jnp.where(_iota2((tm, tm), 0) < _iota2((tm, tm), 1), 1.0, 0.0).astype(BF16)
    before_in_expert = _dot(picked, earlier_tok)
    count_rep = _dot(picked, jnp.ones((tm, tm), BF16))
    lower_expert = jnp.where(_iota2((N_EXPERTS, N_EXPERTS), 1) < _iota2((N_EXPERTS, N_EXPERTS), 0), 1.0, 0.0)
    pos = _dot(lower_expert.astype(BF16), count_rep.astype(BF16)) + before_in_expert
    rloc = [jnp.sum(jnp.where(hit, pos, 0.0), axis=0, keepdims=True).astype(jnp.int32) for hit in hits]
    rloc_ref[...] = jnp.concatenate(rloc + [jnp.zeros((pad, tm), jnp.int32)], axis=0)
    cnt_ref[0] = _dot_nt(jnp.ones((SUBLANES, tm), BF16), picked)


def _router(x_new, mod, norm_g, rwt, rb, mod_index):
    t, d = x_new.shape
    n_tiles = t // ROW_TILE
    return pl.pallas_call(
        _router_kernel,
        grid=(n_tiles,),
        in_specs=[pl.BlockSpec((ROW_TILE, d), lambda r: (r, 0)),
                  pl.BlockSpec((1, 6, d), mod_index),
                  pl.BlockSpec((1, d), lambda r: (0, 0)),
                  pl.BlockSpec(rwt.shape, lambda r: (0, 0)),
                  pl.BlockSpec(rb.shape, lambda r: (0, 0))],
        out_specs=[pl.BlockSpec((SUBLANES, ROW_TILE), lambda r: (0, r)),
                   pl.BlockSpec((SUBLANES, ROW_TILE), lambda r: (0, r)),
                   pl.BlockSpec((1, SUBLANES, N_EXPERTS), lambda r: (r, 0, 0))],
        out_shape=[jax.ShapeDtypeStruct((SUBLANES, t), jnp.int32),
                   jax.ShapeDtypeStruct((SUBLANES, t), F32),
                   jax.ShapeDtypeStruct((n_tiles, SUBLANES, N_EXPERTS), F32)],
        compiler_params=_CP(dimension_semantics=("arbitrary",)),
        name="router",
    )(x_new, mod, norm_g.reshape(1, d), rwt, rb)


SEG_ROWS = 16
TILE_SLOTS = ROW_TILE * TOP_K
_COPY_CLASSES = ((SEG_ROWS, TILE_SLOTS // SEG_ROWS),) + tuple(
    (1 << s, N_EXPERTS) for s in range(SEG_ROWS.bit_length() - 2, -1, -1))
_COPY_COUNTS_AT = 2 * sum(cap for _, cap in _COPY_CLASSES)
_COPY_LIST_LEN = -(-(_COPY_COUNTS_AT + len(_COPY_CLASSES)) // LANES) * LANES


def _copy_lists(cnt, lstart, goff):
    n_tiles = cnt.shape[0]
    experts = jnp.arange(N_EXPERTS, dtype=jnp.int32)

    def expand(per_e, cap):
        cum = jnp.cumsum(per_e, axis=1)
        j = jnp.arange(cap, dtype=jnp.int32)
        e_of_j = jnp.minimum(jnp.sum((cum[:, None, :] <= j[None, :, None]).astype(jnp.int32), axis=2), N_EXPERTS - 1)
        onehot = (e_of_j[:, :, None] == experts[None, None, :]).astype(jnp.int32)
        take = lambda v: jnp.sum(onehot * v[:, None, :], axis=2)
        return take, j[None, :] - take(cum - per_e), cum[:, -1]

    parts, counts = [], []
    for size, cap in _COPY_CLASSES:
        if size == SEG_ROWS:
            take, k, total = expand(cnt // SEG_ROWS, cap)
            off = k * SEG_ROWS
        else:
            take, _, total = expand((jnp.bitwise_and(cnt, size) > 0).astype(jnp.int32), cap)
            n = take(cnt)
            off = n - n % (2 * size)
        parts += [take(lstart) + off, take(goff) + off]
        counts.append(total)
    row = jnp.concatenate(parts + [jnp.stack(counts, axis=1)], axis=1)
    row = jnp.pad(row, ((0, 0), (0, _COPY_LIST_LEN - row.shape[1])))
    return row.astype(jnp.int32).reshape(n_tiles, 1, _COPY_LIST_LEN)


def _segment_copies(list_ref, make_copy):
    def rows(first, n_rows):
        return pl.ds(pl.multiple_of(first * SLAB, SLAB), n_rows * SLAB)

    base = 0
    for k, (size, cap) in enumerate(_COPY_CLASSES):
        def issue(j, carry, base=base, size=size, cap=cap):
            make_copy(rows(list_ref[0, 0, base + j], size), rows(list_ref[0, 0, base + cap + j], size)).start()
            return carry

        lax.fori_loop(0, list_ref[0, 0, _COPY_COUNTS_AT + k], issue, 0)
        base += 2 * cap


def _dispatch_kernel(h2_ref, rloc_ref, list_ref, xs_hbm, buf, sem):
    i = pl.program_id(0)
    n_tiles = pl.num_programs(0)
    slot = i % 2
    tm = h2_ref.shape[0]

    def all_copies(s):
        return pltpu.make_async_copy(buf.at[s], xs_hbm.at[pl.ds(0, TILE_SLOTS * SLAB)], sem.at[s])

    @pl.when(i >= 2)
    def _():
        all_copies(slot).wait()

    rid = _iota2((TILE_SLOTS, tm), 0)
    perm = jnp.zeros((TILE_SLOTS, tm), F32)
    for k in range(TOP_K):
        perm = jnp.where(rid == rloc_ref[k:k + 1, :], 1.0, perm)
    perm = perm.astype(BF16)
    for s in range(0, SLAB, 2):
        xs = _dot(perm, h2_ref[:, s * LANES:(s + 2) * LANES])
        buf[slot, pl.ds(s, TILE_SLOTS, stride=SLAB), :] = xs[:, :LANES]
        buf[slot, pl.ds(s + 1, TILE_SLOTS, stride=SLAB), :] = xs[:, LANES:]

    _segment_copies(list_ref, lambda loc, glob: pltpu.make_async_copy(buf.at[slot, loc], xs_hbm.at[glob], sem.at[slot]))

    @pl.when(i == n_tiles - 1)
    def _():
        all_copies(slot).wait()

        @pl.when(n_tiles >= 2)
        def _():
            all_copies(1 - slot).wait()


def _dispatch(h2, rloc, lists):
    t, d = h2.shape
    n_tiles = t // ROW_TILE
    return pl.pallas_call(
        _dispatch_kernel,
        grid=(n_tiles,),
        in_specs=[pl.BlockSpec((ROW_TILE, d), lambda i: (i, 0)),
                  pl.BlockSpec((SUBLANES, ROW_TILE), lambda i: (0, i)),
                  pl.BlockSpec((1, 1, _COPY_LIST_LEN), lambda i: (i, 0, 0), memory_space=pltpu.SMEM)],
        out_specs=pl.BlockSpec(memory_space=pl.ANY),
        out_shape=jax.ShapeDtypeStruct((t * TOP_K * SLAB, LANES), F32),
        scratch_shapes=[pltpu.VMEM((2, TILE_SLOTS * SLAB, LANES), F32),
                        pltpu.SemaphoreType.DMA((2,))],
        compiler_params=_CP(dimension_semantics=("arbitrary",)),
        name="moe_dispatch",
    )(h2, rloc, lists)


def _expert_kernel(blk_ref, exp_ref, lo_ref, hi_ref, n_ref, x_ref, wg_ref, wu_ref, wdn_ref, y_ref,
                   wgu_s, wd_s):
    i = pl.program_id(0)
    prev = jnp.maximum(i - 1, 0)
    valid = i < n_ref[0]
    new_expert = jnp.logical_or(i == 0, exp_ref[i] != exp_ref[prev])
    first_of_block = jnp.logical_or(i == 0, blk_ref[i] != blk_ref[prev])

    @pl.when(jnp.logical_and(valid, new_expert))
    def _():
        wgu_s[:, :D_EXPERT] = wg_ref[0, 0].astype(BF16)
        wgu_s[:, D_EXPERT:] = wu_ref[0, 0].astype(BF16)
        wd_s[...] = wdn_ref[0, 0].astype(BF16)

    @pl.when(valid)
    def _():
        lhs = jnp.concatenate([x_ref[pl.ds(k, MOE_ROWS, stride=SLAB), :] for k in range(SLAB)], axis=1)
        gu = _dot(lhs.astype(BF16), wgu_s[...])
        act = _silu(gu[:, :D_EXPERT]) * gu[:, D_EXPERT:]
        y = _dot(act.astype(BF16), wd_s[...])
        rows = _iota2((MOE_ROWS, LANES), 0)
        mine = jnp.logical_and(rows >= lo_ref[i], rows < hi_ref[i])

        @pl.when(first_of_block)
        def _():
            for k in range(SLAB):
                y_ref[pl.ds(k, MOE_ROWS, stride=SLAB), :] = jnp.where(mine, y[:, k * LANES:(k + 1) * LANES], 0.0)

        @pl.when(jnp.logical_not(first_of_block))
        def _():
            for k in range(SLAB):
                rows_k = pl.ds(k, MOE_ROWS, stride=SLAB)
                y_ref[rows_k, :] = jnp.where(mine, y[:, k * LANES:(k + 1) * LANES], y_ref[rows_k, :])


def _experts(xs, items, layer, wg, wu, wdn):
    blk, exp, lo, hi, n_items = items
    by_block = lambda i, blk_r, exp_r, lo_r, hi_r, n_r: (blk_r[i], 0)
    by_expert = lambda i, blk_r, exp_r, lo_r, hi_r, n_r: (layer, exp_r[i], 0, 0)
    grid_spec = pltpu.PrefetchScalarGridSpec(
        num_scalar_prefetch=5,
        grid=(blk.shape[0],),
        in_specs=[pl.BlockSpec((MOE_ROWS * SLAB, LANES), by_block),
                  pl.BlockSpec((1, 1) + wg.shape[2:], by_expert),
                  pl.BlockSpec((1, 1) + wu.shape[2:], by_expert),
                  pl.BlockSpec((1, 1) + wdn.shape[2:], by_expert)],
        out_specs=pl.BlockSpec((MOE_ROWS * SLAB, LANES), by_block),
        scratch_shapes=[pltpu.VMEM((wg.shape[2], 2 * D_EXPERT), BF16),
                        pltpu.VMEM(wdn.shape[2:], BF16)])
    return pl.pallas_call(
        _expert_kernel,
        grid_spec=grid_spec,
        out_shape=jax.ShapeDtypeStruct(xs.shape, F32),
        compiler_params=_CP(dimension_semantics=("arbitrary",)),
        name="routed_experts",
    )(blk, exp, lo, hi, n_items, xs, wg, wu, wdn)


def _moe_plan(cnt_tiles, t):
    p = t * TOP_K
    assert p % MOE_ROWS == 0
    cnt = cnt_tiles[:, 0, :].astype(jnp.int32)
    totals = jnp.sum(cnt, axis=0)
    ends = jnp.cumsum(totals)
    starts = ends - totals
    goff = starts[None, :] + jnp.cumsum(cnt, axis=0) - cnt
    lstart = jnp.cumsum(cnt, axis=1) - cnt
    n_blk = jnp.where(totals > 0, (ends - 1) // MOE_ROWS - starts // MOE_ROWS + 1, 0)
    item_end = jnp.cumsum(n_blk)
    n_items = item_end[-1]
    max_items = p // MOE_ROWS + N_EXPERTS
    it = jnp.minimum(jnp.arange(max_items, dtype=jnp.int32), jnp.maximum(n_items - 1, 0))
    exp = jnp.minimum(jnp.sum((item_end[None, :] <= it[:, None]).astype(jnp.int32), axis=1), N_EXPERTS - 1)
    sel = (exp[:, None] == jnp.arange(N_EXPERTS, dtype=jnp.int32)[None, :]).astype(jnp.int32)
    pick = lambda v: jnp.sum(sel * v[None, :], axis=1)
    blk = jnp.clip(pick(starts) // MOE_ROWS + it - pick(item_end - n_blk), 0, p // MOE_ROWS - 1)
    lo = jnp.maximum(pick(starts), blk * MOE_ROWS) - blk * MOE_ROWS
    hi = jnp.minimum(pick(ends), (blk + 1) * MOE_ROWS) - blk * MOE_ROWS
    items = tuple(a.astype(jnp.int32) for a in (blk, exp, lo, hi, n_items.reshape(1)))
    return _copy_lists(cnt, lstart, goff), items


def _combine_kernel(final, n_ctx_tiles, x_ref, h2_ref, w_ref, rloc_ref, mod_ref, fn_ref, wsgu_ref, wsd_ref,
                    list_ref, list_next_ref, y_hbm, oc_ref, ol_ref, ybuf, sem):
    i = pl.program_id(0)
    n_tiles = pl.num_programs(0)
    slot = i % 2
    tm = x_ref.shape[0]

    def fetch(lists, s):
        _segment_copies(lists, lambda loc, glob: pltpu.make_async_copy(y_hbm.at[glob], ybuf.at[s, loc], sem.at[s]))

    @pl.when(i == 0)
    def _():
        fetch(list_ref, 0)

    @pl.when(i + 1 < n_tiles)
    def _():
        fetch(list_next_ref, 1 - slot)

    gu = _dot(h2_ref[...], wsgu_ref[...])
    shared = _dot((_silu(gu[:, :D_SHARED]) * gu[:, D_SHARED:]).astype(BF16), wsd_ref[...])

    rid = _iota2((tm, TILE_SLOTS), 1)
    w = w_ref[...]
    unsort = jnp.zeros((tm, TILE_SLOTS), F32)
    for k in range(TOP_K):
        unsort = jnp.where(rid == rloc_ref[:, k:k + 1], w[:, k:k + 1], unsort)
    unsort = unsort.astype(BF16)

    pltpu.make_async_copy(y_hbm.at[pl.ds(0, TILE_SLOTS * SLAB)], ybuf.at[slot], sem.at[slot]).wait()
    pieces = [_dot(unsort, ybuf[slot, pl.ds(s, TILE_SLOTS, stride=SLAB), :].astype(BF16)) for s in range(SLAB)]
    routed = jnp.concatenate(pieces, axis=1)
    xo = x_ref[...] + mod_ref[0, 5:6, :] * (routed + shared)
    if final:
        xo = xo * lax.rsqrt(jnp.mean(xo * xo, axis=-1, keepdims=True) + EPS) * fn_ref[...]

    @pl.when(i < n_ctx_tiles)
    def _():
        oc_ref[...] = xo

    @pl.when(i >= n_ctx_tiles)
    def _():
        ol_ref[...] = xo


def _combine(final, n_ctx_tiles, x_new, h2, y, w_tk, rloc_tk, lists, mod, final_norm, wsgu, wsd, mod_index):
    t, d = x_new.shape
    n_tiles = t // ROW_TILE
    row = lambda r: (r, 0)
    cur = lambda r: (r, 0, 0)
    nxt = lambda r: (jnp.minimum(r + 1, n_tiles - 1), 0, 0)
    smem_tile = functools.partial(pl.BlockSpec, (1, 1, _COPY_LIST_LEN), memory_space=pltpu.SMEM)
    t_ctx = n_ctx_tiles * ROW_TILE
    return pl.pallas_call(
        functools.partial(_combine_kernel, final, n_ctx_tiles),
        grid=(n_tiles,),
        in_specs=[pl.BlockSpec((ROW_TILE, d), row),
                  pl.BlockSpec((ROW_TILE, d), row),
                  pl.BlockSpec((ROW_TILE, SUBLANES), row),
                  pl.BlockSpec((ROW_TILE, SUBLANES), row),
                  pl.BlockSpec((1, 6, d), mod_index),
                  pl.BlockSpec((1, d), lambda r: (0, 0)),
                  pl.BlockSpec(wsgu.shape, lambda r: (0, 0)),
                  pl.BlockSpec(wsd.shape, lambda r: (0, 0)),
                  smem_tile(cur), smem_tile(nxt),
                  pl.BlockSpec(memory_space=pl.ANY)],
        out_specs=_group_specs(d, n_ctx_tiles),
        out_shape=[jax.ShapeDtypeStruct((t_ctx, d), F32), jax.ShapeDtypeStruct((t - t_ctx, d), F32)],
        scratch_shapes=[pltpu.VMEM((2, TILE_SLOTS * SLAB, LANES), F32),
                        pltpu.SemaphoreType.DMA((2,))],
        compiler_params=_CP(dimension_semantics=("arbitrary",)),
        name="combine_shared",
    )(x_new, h2, w_tk, rloc_tk, mod, final_norm.reshape(1, d), wsgu, wsd, lists, lists, y)


def _qk_perm():
    half = DK_R // 2
    perm = np.zeros(RET_QK_W, np.int32)
    for part in range(2):
        for h in range(H_R):
            for j in range(half):
                perm[part * LANES + h * half + j] = h * DK_R + part * half + j
    return perm


def _ret_state_rows():
    half = DK_R // 2
    rows = np.zeros((H_R, DK_R), np.int32)
    for h in range(H_R):
        for j in range(DK_R):
            rows[h, j] = (j // half) * LANES + h * half + j % half
    return rows


def _xbc_channels():
    ch = list(range(BR_W))
    for base in (BR_W, BR_W + G_M * N_M):
        for g in range(G_M):
            grp = list(range(base + g * N_M, base + (g + 1) * N_M))
            ch += grp + grp
    return np.asarray(ch, np.int32)


def _pad_lanes(v):
    return jnp.zeros((1, LANES), F32).at[0, :v.shape[0]].set(v.astype(F32))


def _ret_state_pack(s):
    rows = _ret_state_rows()
    src = np.full((H_R, 2 * LANES), DK_R, np.int32)
    for h in range(H_R):
        src[h, rows[h]] = np.arange(DK_R)
    sz = jnp.concatenate([s, jnp.zeros(s.shape[:3] + (1, DV_R), s.dtype)], axis=3)
    return sz[:, :, np.arange(H_R)[:, None], src, :]


def _ssm_state_pack(s):
    b = s.shape[0]
    sr = s.reshape(b, 2, H_M // 2, 2, N_M, 1, P_M)
    eye = jnp.eye(2, dtype=s.dtype).reshape(1, 1, 1, 2, 1, 2, 1)
    return (sr * eye).reshape(b, 2, H_M // 2, 2 * N_M, 2 * P_M)


def _grid_rope(l):
    rows = l // GRID_W
    row = jnp.repeat(jnp.arange(rows), GRID_W).astype(F32)
    col = (jnp.arange(rows * GRID_W) % GRID_W).astype(F32)
    n_freq = DK_R // 4
    freqs = ROPE_BASE ** (-jnp.arange(n_freq, dtype=F32) / n_freq)
    ang = jnp.concatenate([row[:, None] * freqs, col[:, None] * freqs], axis=-1)
    return jnp.tile(jnp.cos(ang), (1, H_R)), jnp.tile(jnp.sin(ang), (1, H_R))


def _layer_weights(i, w_in, ssm_conv_w, ssm_conv_b):
    cuts = np.cumsum(IN_SPLITS)[:-1]
    rq, rk, rv, rg, sz, sxbc, sdt, hq, hf, hi, hg, gl = jnp.split(w_in[i], cuts, axis=1)
    perm = _qk_perm()
    ch = _xbc_channels()
    w_ret = jnp.concatenate([rq[:, perm], rk[:, perm] * (DK_R ** -0.5), rv, rg], axis=1).astype(BF16)
    w_ssm = jnp.concatenate([sz, sxbc[:, ch]], axis=1).astype(BF16)
    w_hg = jnp.concatenate([hq, hi, hg], axis=1).astype(BF16)
    w_prec = jnp.concatenate([hf, sdt, jnp.zeros((w_in.shape[1], LANES - SSM_DT_W), F32)], axis=1).astype(BF16)
    conv_w = ssm_conv_w[i][ch].T
    conv_b = ssm_conv_b[i][ch][None, :]
    return (w_ret, w_ssm, w_hg, gl.astype(BF16), w_prec), conv_w, conv_b


def kernel(x_prompt, x_sample, state_ret, state_ssm, state_hgrn, c, c_ctx, ada_w, ada_b, norm1, norm2, w_in,
           ret_decay_logit, ret_gn, ssm_conv_w, ssm_conv_b, ssm_a_log, ssm_dt_bias, ssm_d, ssm_norm,
           hgrn_lb_logits, hgrn_norm, w_branch, w_out, router_w, router_b, exp_w_gate, exp_w_up,
           exp_w_down, sh_w_gate, sh_w_up, sh_w_down, final_norm):
    bc, lc, d = x_prompt.shape
    bl, ll, _ = x_sample.shape
    tc, tl = bc * lc, bl * ll
    t = tc + tl
    assert tc % ROW_TILE == 0 and ll % ROW_TILE == 0 and lc % CHUNK == 0 and ll % CHUNK == 0
    assert tl % lc == 0 and tc % ll == 0 and 1 + bl <= SUBLANES
    mod_index = _mod_index(tc // ROW_TILE, ll // ROW_TILE)

    n_ctx_tiles = tc // ROW_TILE
    x_pair = (x_prompt.reshape(tc, d), x_sample.reshape(tl, d))
    c_all = jnp.concatenate([c_ctx[None, :], c, jnp.zeros((SUBLANES - 1 - bl, d), F32)], axis=0)
    mods = _ada(c_all, ada_w, ada_b)

    lb_all = jnp.cumsum(jax.nn.softmax(hgrn_lb_logits.astype(F32), axis=1), axis=1)
    lb_all = lb_all - lb_all[:, :1]
    cos4, sin4 = _grid_rope(ll)
    lane_head = (np.arange(2 * LANES) % LANES) // (DK_R // 2)

    new_ret, new_ssm, new_hg = [], [], []
    for i in range(DEPTH):
        mod = mods[i].reshape(SUBLANES, 6, d)
        weights, conv_w, conv_b = _layer_weights(i, w_in, ssm_conv_w, ssm_conv_b)
        ret_in, ssm_in, hg_in, gl, prec = _inproj(x_pair, mod, norm1[i], weights, (BF16, BF16, BF16, BF16, F32),
                                                  mod_index)

        lg = jax.nn.log_sigmoid(ret_decay_logit[i].astype(F32))
        lgl = lg[:, lane_head]
        lgv = jnp.broadcast_to(lg[:, :, None], (2, H_R, LANES))
        gn = ret_gn[i][None, :]
        o_ret_c, s_ret = _retention(ret_in, 0, bc, lc, None, None, lgl, lgv, gn, None)
        o_ret_l, = _retention(ret_in, tc, bl, ll, cos4, sin4, lgl, lgv, gn, _ret_state_pack(state_ret[:, i]))
        new_ret.append(s_ret)

        dtb = _pad_lanes(ssm_dt_bias[i].reshape(-1))
        na = _pad_lanes(-jnp.exp(ssm_a_log[i].astype(F32)).reshape(-1))
        dsk = jnp.repeat(ssm_d[i], P_M)[None, :]
        ng = ssm_norm[i][None, :]
        o_ssm_c, s_ssm = _ssd(ssm_in, prec, 0, bc, lc, conv_w, conv_b, dtb, na, dsk, ng, None)
        o_ssm_l, = _ssd(ssm_in, prec, tc, bl, ll, conv_w, conv_b, dtb, na, dsk, ng,
                        _ssm_state_pack(state_ssm[:, i]))
        new_ssm.append(s_ssm)

        lb = lb_all[:, i]
        llb = jnp.log(lb).reshape(1, 2 * BR_W)
        oml = (1.0 - lb).reshape(1, 2 * BR_W)
        l1m = jnp.log1p(-lb).reshape(1, 2 * BR_W)
        hn = hgrn_norm[i][None, :]
        o_hg_c, s_hg = _hgrn(hg_in, prec, 0, bc, lc, llb, oml, l1m, hn, None)
        o_hg_l, = _hgrn(hg_in, prec, tc, bl, ll, llb, oml, l1m, hn, state_hgrn[:, i])
        new_hg.append(s_hg)

        x_new, h2 = _merge(x_pair, mod, norm2[i], ((o_ret_c, o_ret_l), (o_ssm_c, o_ssm_l), (o_hg_c, o_hg_l)), gl,
                           w_branch[i].astype(BF16), w_out[i].astype(BF16), mod_index)

        rloc, w8, cnt_tiles = _router(x_new, mod, norm2[i], router_w[i].T, router_b[i][:, None], mod_index)
        lists, items = _moe_plan(cnt_tiles, t)
        xs = _dispatch(h2, rloc, lists)
        y = _experts(xs, items, i, exp_w_gate, exp_w_up, exp_w_down)
        wsgu = jnp.concatenate([sh_w_gate[i], sh_w_up[i]], axis=-1).astype(BF16)
        x_pair = _combine(i == DEPTH - 1, n_ctx_tiles, x_new, h2, y, w8.T, rloc.T, lists, mod,
                          final_norm, wsgu, sh_w_down[i].astype(BF16), mod_index)

    y_prompt = x_pair[0].reshape(bc, lc, d)
    y_sample = x_pair[1].reshape(bl, ll, d)
    return (y_prompt, y_sample, jnp.stack(new_ret, axis=1), jnp.stack(new_ssm, axis=1),
            jnp.stack(new_hg, axis=1))
```

```python
import functools

import numpy as np
import jax
import jax.numpy as jnp
from jax import lax
from jax.experimental import pallas as pl
from jax.experimental.pallas import tpu as pltpu

F32 = jnp.float32
BF16 = jnp.bfloat16
HIGHEST = lax.Precision.HIGHEST

D_MODEL = 1024
DEPTH = 2
GRID_W = 64
H_R, DK_R, DV_R = 4, 64, 128
H_M, P_M, N_M, G_M, D_CONV = 8, 64, 64, 2, 5
H_C, E_C, DV_C = 4, 128, 128
BR_W = 512
N_BRANCH = 3
RET_QK_W = H_R * DK_R
CONV_CH = BR_W + 2 * G_M * N_M
SSM_DT_W = 2 * H_M
IN_SPLITS = (RET_QK_W, RET_QK_W, BR_W, BR_W, BR_W, CONV_CH, SSM_DT_W, BR_W, 2 * BR_W, BR_W, BR_W,
             N_BRANCH * D_MODEL)
N_EXPERTS, TOP_K, N_GROUPS, TOPK_GROUPS = 64, 6, 8, 4
D_EXPERT = 256
D_SHARED = 256
ROUTED_SCALE = 2.5
EPS = 1e-6
ROPE_BASE = 10000.0

LANES = 128
SUBLANES = 8
CHUNK = 128
ROW_TILE = 256
MOE_ROWS = 512
VMEM_LIMIT = 56 * 1024 * 1024
NEG_BIG = -1e30

_CP = functools.partial(pltpu.CompilerParams, vmem_limit_bytes=VMEM_LIMIT)


def _sigmoid(x):
    return 1.0 / (1.0 + jnp.exp(-x))


def _silu(x):
    return x * _sigmoid(x)


def _softplus(x):
    return jnp.maximum(x, 0.0) + jnp.log1p(jnp.exp(-jnp.abs(x)))


def _log_sigmoid(x):
    return jnp.minimum(x, 0.0) - jnp.log1p(jnp.exp(-jnp.abs(x)))


def _dot(a, b):
    return jnp.dot(a, b, preferred_element_type=F32)


def _dot_nt(a, b):
    return lax.dot_general(a, b, (((1,), (1,)), ((), ())), preferred_element_type=F32)


def _dot_tn(a, b):
    return lax.dot_general(a, b, (((0,), (0,)), ((), ())), preferred_element_type=F32)


def _tri_dot(tri, x):
    hi = x.astype(BF16)
    r1 = x - hi.astype(F32)
    mid = r1.astype(BF16)
    lo = (r1 - mid.astype(F32)).astype(BF16)
    return _dot(tri, hi) + _dot(tri, mid) + _dot(tri, lo)


def _iota2(shape, dim):
    return lax.broadcasted_iota(jnp.int32, shape, dim)


def _rows(c):
    return pl.ds(pl.multiple_of(c * CHUNK, CHUNK), CHUNK)


def _const_spec(shape):
    nd = len(shape)
    return pl.BlockSpec(shape, lambda *_: (0,) * nd)


def _ada_kernel(c_ref, w_ref, b_ref, o_ref):
    a = _silu(c_ref[...])
    o_ref[0] = jnp.dot(a, w_ref[0], preferred_element_type=F32, precision=HIGHEST) + b_ref[0]


def _ada(c_all, ada_w, ada_b):
    depth, d, n = ada_w.shape
    tn = 1536
    return pl.pallas_call(
        _ada_kernel,
        grid=(depth, n // tn),
        in_specs=[pl.BlockSpec((SUBLANES, d), lambda i, j: (0, 0)),
                  pl.BlockSpec((1, d, tn), lambda i, j: (i, 0, j)),
                  pl.BlockSpec((1, 1, tn), lambda i, j: (i, 0, j))],
        out_specs=pl.BlockSpec((1, SUBLANES, tn), lambda i, j: (i, 0, j)),
        out_shape=jax.ShapeDtypeStruct((depth, SUBLANES, n), F32),
        compiler_params=_CP(dimension_semantics=("arbitrary", "arbitrary")),
        name="ada_mod",
    )(c_all, ada_w, ada_b.reshape(depth, 1, n))


def _mod_index(n_ctx_tiles, tiles_per_seq):
    def index(r):
        return (jnp.where(r < n_ctx_tiles, 0, 1 + (r - n_ctx_tiles) // tiles_per_seq), 0, 0)
    return index


def _modulated_norm(x, g, shift, scale):
    y = x * lax.rsqrt(jnp.mean(x * x, axis=-1, keepdims=True) + EPS) * g
    return y * (1.0 + scale) + shift


def _group_specs(cols, n_ctx_tiles):
    return [pl.BlockSpec((ROW_TILE, cols), lambda r: (jnp.minimum(r, n_ctx_tiles - 1), 0)),
            pl.BlockSpec((ROW_TILE, cols), lambda r: (jnp.maximum(r - n_ctx_tiles, 0), 0))]


def _group_pick(n_ctx_tiles, ctx_ref, lat_ref):
    return jnp.where(pl.program_id(0) < n_ctx_tiles, ctx_ref[...], lat_ref[...])


def _inproj_kernel(n_ctx_tiles, xc_ref, xl_ref, mod_ref, n_ref, w0, w1, w2, w3, w4, o0, o1, o2, o3, o4):
    x = _group_pick(n_ctx_tiles, xc_ref, xl_ref)
    h = _modulated_norm(x, n_ref[...], mod_ref[0, 0:1, :], mod_ref[0, 1:2, :]).astype(BF16)
    for w, o in ((w0, o0), (w1, o1), (w2, o2), (w3, o3), (w4, o4)):
        o[...] = _dot(h, w[...]).astype(o.dtype)


def _inproj(x_pair, mod, norm_g, weights, out_dtypes, mod_index):
    xc, xl = x_pair
    d = xc.shape[1]
    t = xc.shape[0] + xl.shape[0]
    n_ctx_tiles = xc.shape[0] // ROW_TILE
    w_specs = [pl.BlockSpec(w.shape, lambda r: (0, 0), pipeline_mode=pl.Buffered(1)) for w in weights]
    return pl.pallas_call(
        functools.partial(_inproj_kernel, n_ctx_tiles),
        grid=(t // ROW_TILE,),
        in_specs=_group_specs(d, n_ctx_tiles) + [pl.BlockSpec((1, 6, d), mod_index),
                                                 pl.BlockSpec((1, d), lambda r: (0, 0))] + w_specs,
        out_specs=[pl.BlockSpec((ROW_TILE, w.shape[1]), lambda r: (r, 0)) for w in weights],
        out_shape=[jax.ShapeDtypeStruct((t, w.shape[1]), dt) for w, dt in zip(weights, out_dtypes)],
        compiler_params=_CP(dimension_semantics=("arbitrary",)),
        name="norm_inproj",
    )(xc, xl, mod, norm_g.reshape(1, d), *weights)


def _ret_kernel(is_ctx, n_chunks, in_ref, cos_ref, sin_ref, lgl_ref, lgv_ref, gn_ref, *rest):
    if is_ctx:
        o_ref, sfin_ref, qk_s, oacc, st, dm_s = rest
    else:
        s0_ref, o_ref, qk_s, oacc, st, dm_s = rest
    use_rope = not is_ctx
    qw = 2 * LANES
    ii = _iota2((CHUNK, CHUNK), 0)
    jj = _iota2((CHUNK, CHUNK), 1)
    dist = jnp.abs(ii - jj).astype(F32)
    for h in range(H_R):
        dm_s[h] = (jnp.where(ii >= jj, jnp.exp(dist * lgv_ref[0, h:h + 1, :]), 0.0)
                   + jnp.where(jj >= ii, jnp.exp(dist * lgv_ref[1, h:h + 1, :]), 0.0))
    st[...] = jnp.zeros(st.shape, F32) if is_ctx else s0_ref[0]

    lane_head = (_iota2((1, qw), 1) % LANES) // (DK_R // 2)
    rr = _iota2((CHUNK, qw), 0).astype(F32)
    lg_f = lgl_ref[0:1, :]
    lg_b = lgl_ref[1:2, :]

    def rope(x, cs, sn):
        x1, x2 = x[:, :LANES], x[:, LANES:]
        return jnp.concatenate([x1 * cs - x2 * sn, x1 * sn + x2 * cs], axis=1)

    def fwd(c, carry):
        rows = _rows(c)
        q = in_ref[rows, 0:qw].astype(F32)
        k = in_ref[rows, qw:2 * qw].astype(F32)
        if use_rope:
            cs, sn = cos_ref[rows, :], sin_ref[rows, :]
            q, k = rope(q, cs, sn), rope(k, cs, sn)
        qk_s[rows, 0:qw] = q
        qk_s[rows, qw:2 * qw] = k
        kb = k.astype(BF16)
        q_dec = (q * jnp.exp((rr + 1.0) * lg_f)).astype(BF16)
        k_dec = k * jnp.exp((CHUNK - 1.0 - rr) * lg_f)
        for h in range(H_R):
            hs = slice(h * DV_R, (h + 1) * DV_R)
            mh = lane_head == h
            vh = in_ref[rows, 2 * qw + h * DV_R:2 * qw + (h + 1) * DV_R]
            s = _dot_nt(jnp.where(mh, q, 0.0).astype(BF16), kb)
            intra = _dot((s * dm_s[h]).astype(BF16), vh)
            sf = st[0, h]
            oacc[rows, hs] = intra + _dot(q_dec, sf.astype(BF16))
            st[0, h] = (sf * jnp.exp(CHUNK * lgv_ref[0, h:h + 1, :])
                        + _dot_tn(jnp.where(mh, k_dec, 0.0).astype(BF16), vh))
        return carry

    lax.fori_loop(0, n_chunks, fwd, 0)

    def bwd(t, carry):
        rows = _rows(n_chunks - 1 - t)
        q = qk_s[rows, 0:qw]
        k = qk_s[rows, qw:2 * qw]
        q_dec = (q * jnp.exp((CHUNK - rr) * lg_b)).astype(BF16)
        k_dec = k * jnp.exp(rr * lg_b)
        for h in range(H_R):
            hs = slice(h * DV_R, (h + 1) * DV_R)
            mh = lane_head == h
            vh = in_ref[rows, 2 * qw + h * DV_R:2 * qw + (h + 1) * DV_R]
            sb = st[1, h]
            oacc[rows, hs] = oacc[rows, hs] + _dot(q_dec, sb.astype(BF16))
            st[1, h] = (sb * jnp.exp(CHUNK * lgv_ref[1, h:h + 1, :])
                        + _dot_tn(jnp.where(mh, k_dec, 0.0).astype(BF16), vh))
        return carry

    lax.fori_loop(0, n_chunks, bwd, 0)
    if is_ctx:
        half = DK_R // 2
        for d in range(2):
            for h in range(H_R):
                sfin_ref[0, d, h, 0:half, :] = st[d, h, h * half:(h + 1) * half, :]
                sfin_ref[0, d, h, half:DK_R, :] = st[d, h, LANES + h * half:LANES + (h + 1) * half, :]

    def fin(c, carry):
        rows = _rows(c)
        for h in range(H_R):
            hs = slice(h * DV_R, (h + 1) * DV_R)
            o = oacc[rows, hs]
            oc = o - jnp.mean(o, axis=-1, keepdims=True)
            y = oc * lax.rsqrt(jnp.mean(oc * oc, axis=-1, keepdims=True) + EPS) * gn_ref[:, hs]
            g = in_ref[rows, 4 * qw + h * DV_R:4 * qw + (h + 1) * DV_R].astype(F32)
            o_ref[rows, hs] = (y * _silu(g)).astype(o_ref.dtype)
        return carry

    lax.fori_loop(0, n_chunks, fin, 0)


def _state_io(s0, b, packed_shape, final_shape):
    lead = lambda shape: pl.BlockSpec((1,) + shape, lambda i: (i,) + (0,) * len(shape))
    if s0 is None:
        return [], [], [lead(final_shape)], [jax.ShapeDtypeStruct((b,) + final_shape, F32)]
    return [lead(packed_shape)], [s0], [], []


def _retention(ret_in, row0, b, l, cos4, sin4, lgl, lgv, gn, s0):
    is_ctx = s0 is None
    if is_ctx:
        cos4 = jnp.zeros((SUBLANES, LANES), F32)
        sin4 = cos4
        trig_spec = pl.BlockSpec((SUBLANES, LANES), lambda i: (0, 0))
    else:
        trig_spec = pl.BlockSpec((l, LANES), lambda i: (0, 0))
    blk0 = row0 // l
    packed = (2, H_R, 2 * LANES, DV_R)
    s_in, s_args, s_out, s_shapes = _state_io(s0, b, packed, (2, H_R, DK_R, DV_R))
    return pl.pallas_call(
        functools.partial(_ret_kernel, is_ctx, l // CHUNK),
        grid=(b,),
        in_specs=[pl.BlockSpec((l, ret_in.shape[1]), lambda i: (blk0 + i, 0)),
                  trig_spec, trig_spec,
                  _const_spec(lgl.shape), _const_spec(lgv.shape), _const_spec(gn.shape)] + s_in,
        out_specs=[pl.BlockSpec((l, BR_W), lambda i: (i, 0))] + s_out,
        out_shape=[jax.ShapeDtypeStruct((b * l, BR_W), BF16)] + s_shapes,
        scratch_shapes=[pltpu.VMEM((l, 4 * LANES), F32),
                        pltpu.VMEM((l, BR_W), F32),
                        pltpu.VMEM(packed, F32),
                        pltpu.VMEM((H_R, CHUNK, CHUNK), F32)],
        compiler_params=_CP(dimension_semantics=("arbitrary",)),
        name="retention",
    )(ret_in, cos4, sin4, lgl, lgv, gn, *s_args)


def _ssd_kernel(is_ctx, n_chunks, in_ref, dt_ref, cw_ref, cb_ref, dtb_ref, na_ref, dsk_ref, ng_ref, *rest):
    if is_ctx:
        o_ref, sfin_ref, pad_s, xc_s, y_s, dt_s, cumb_s, st = rest
    else:
        s0_ref, o_ref, pad_s, xc_s, y_s, dt_s, cumb_s, st = rest
    l = in_ref.shape[0]
    cw = 2 * BR_W
    halo = SUBLANES
    pad_s[0:halo, :] = jnp.zeros((halo, cw), F32)
    pad_s[l + halo:l + 2 * halo, :] = jnp.zeros((halo, cw), F32)

    def fill(c, carry):
        rows = _rows(c)
        dst = pl.ds(pl.multiple_of(c * CHUNK + halo, SUBLANES), CHUNK)
        pad_s[dst, :] = in_ref[rows, BR_W:BR_W + cw].astype(F32)
        return carry

    lax.fori_loop(0, n_chunks, fill, 0)
    st[...] = jnp.zeros(st.shape, F32) if is_ctx else s0_ref[0]

    ii = _iota2((CHUNK, CHUNK), 0)
    jj = _iota2((CHUNK, CHUNK), 1)
    tril = jnp.where(ii >= jj, 1.0, 0.0).astype(BF16)
    triu = jnp.where(jj >= ii, 1.0, 0.0).astype(BF16)
    lane_lo = jj < P_M
    row_lo = ii < N_M
    blockdiag = lane_lo == row_lo
    half = D_CONV // 2

    def pair_cols(vals, h0, h1):
        return jnp.where(lane_lo, vals[:, h0:h0 + 1], vals[:, h1:h1 + 1])

    def fwd(c, carry):
        rows = _rows(c)
        win = pad_s[pl.ds(pl.multiple_of(c * CHUNK, CHUNK), CHUNK + 2 * halo), :]
        acc = win[halo - half:halo - half + CHUNK, :] * cw_ref[0:1, :] + cb_ref[...]
        for w in range(1, D_CONV):
            acc = acc + win[halo - half + w:halo - half + w + CHUNK, :] * cw_ref[w:w + 1, :]
        xc = _silu(acc)
        xc_s[rows, :] = xc.astype(xc_s.dtype)
        x = xc[:, 0:BR_W]
        dt = _softplus(dt_ref[rows, :] + dtb_ref[...])
        la = dt * na_ref[...]
        cumf = _tri_dot(tril, la)
        cumb = _tri_dot(triu, la)
        dt_s[rows, :] = dt
        cumb_s[rows, :] = cumb
        cumf_t, cumb_t, dt_t = cumf.T, cumb.T, dt.T
        totf = cumf[CHUNK - 1:CHUNK, :]
        etotf = jnp.exp(totf)
        kdec = dt * jnp.exp(totf - cumf)
        qdec = jnp.exp(cumf)
        for g in range(G_M):
            b2 = xc[:, BR_W + g * LANES:BR_W + (g + 1) * LANES]
            c2 = xc[:, BR_W + (G_M + g) * LANES:BR_W + (G_M + g + 1) * LANES]
            cbm = _dot_nt(jnp.where(lane_lo, c2, 0.0).astype(BF16), b2.astype(BF16))
            for pp in range(2):
                p = 2 * g + pp
                h0, h1 = 2 * p, 2 * p + 1
                ps = slice(p * LANES, (p + 1) * LANES)
                ms = []
                for h in (h0, h1):
                    hb = H_M + h
                    mf = jnp.where(ii >= jj, jnp.exp(cumf[:, h:h + 1] - cumf_t[h:h + 1, :]), 0.0) * dt_t[h:h + 1, :]
                    mb = (jnp.where(jj >= ii, jnp.exp(cumb[:, hb:hb + 1] - cumb_t[hb:hb + 1, :]), 0.0)
                          * dt_t[hb:hb + 1, :])
                    ms.append((cbm * (mf + mb)).astype(BF16))
                xp = x[:, ps]
                xbd = jnp.concatenate([jnp.where(lane_lo, xp, 0.0), jnp.where(lane_lo, 0.0, xp)], axis=0)
                intra = _dot(jnp.concatenate(ms, axis=1), xbd.astype(BF16))
                sf = st[0, p]
                inter = _dot((c2 * pair_cols(qdec, h0, h1)).astype(BF16), sf.astype(BF16))
                y_s[rows, ps] = intra + inter
                kv = _dot_tn((b2 * pair_cols(kdec, h0, h1)).astype(BF16), xp.astype(BF16))
                arow = jnp.where(row_lo, etotf[:, h0:h0 + 1], etotf[:, h1:h1 + 1])
                st[0, p] = sf * arow + jnp.where(blockdiag, kv, 0.0)
        return carry

    lax.fori_loop(0, n_chunks, fwd, 0)

    def bwd(t, carry):
        rows = _rows(n_chunks - 1 - t)
        xc = xc_s[rows, :].astype(F32)
        x = xc[:, 0:BR_W]
        dt = dt_s[rows, :]
        cumb = cumb_s[rows, :]
        totb = cumb[0:1, :]
        etotb = jnp.exp(totb)
        kdec = dt * jnp.exp(totb - cumb)
        qdec = jnp.exp(cumb)
        for g in range(G_M):
            b2 = xc[:, BR_W + g * LANES:BR_W + (g + 1) * LANES]
            c2 = xc[:, BR_W + (G_M + g) * LANES:BR_W + (G_M + g + 1) * LANES]
            for pp in range(2):
                p = 2 * g + pp
                h0, h1 = H_M + 2 * p, H_M + 2 * p + 1
                ps = slice(p * LANES, (p + 1) * LANES)
                xp = x[:, ps]
                sb = st[1, p]
                y_s[rows, ps] = y_s[rows, ps] + _dot((c2 * pair_cols(qdec, h0, h1)).astype(BF16), sb.astype(BF16))
                kv = _dot_tn((b2 * pair_cols(kdec, h0, h1)).astype(BF16), xp.astype(BF16))
                arow = jnp.where(row_lo, etotb[:, h0:h0 + 1], etotb[:, h1:h1 + 1])
                st[1, p] = sb * arow + jnp.where(blockdiag, kv, 0.0)
        return carry

    lax.fori_loop(0, n_chunks, bwd, 0)
    if is_ctx:
        for d in range(2):
            for p in range(H_M // 2):
                pair = st[d, p]
                sfin_ref[0, d, 2 * p] = pair[0:N_M, 0:P_M]
                sfin_ref[0, d, 2 * p + 1] = pair[N_M:2 * N_M, P_M:2 * P_M]

    def fin(c, carry):
        rows = _rows(c)
        x = xc_s[rows, 0:BR_W].astype(F32)
        z = in_ref[rows, 0:BR_W].astype(F32)
        u = (y_s[rows, :] + x * dsk_ref[...]) * _silu(z)
        y = u * lax.rsqrt(jnp.mean(u * u, axis=-1, keepdims=True) + EPS) * ng_ref[...]
        o_ref[rows, :] = y.astype(o_ref.dtype)
        return carry

    lax.fori_loop(0, n_chunks, fin, 0)


def _ssd(ssm_in, prec, row0, b, l, cw, cb, dtb, na, dsk, ng, s0):
    blk0 = row0 // l
    dt_col = (prec.shape[1] - LANES) // LANES
    packed = (2, H_M // 2, LANES, LANES)
    s_in, s_args, s_out, s_shapes = _state_io(s0, b, packed, (2, H_M, N_M, P_M))
    return pl.pallas_call(
        functools.partial(_ssd_kernel, s0 is None, l // CHUNK),
        grid=(b,),
        in_specs=[pl.BlockSpec((l, ssm_in.shape[1]), lambda i: (blk0 + i, 0)),
                  pl.BlockSpec((l, LANES), lambda i: (blk0 + i, dt_col)),
                  _const_spec(cw.shape), _const_spec(cb.shape), _const_spec(dtb.shape),
                  _const_spec(na.shape), _const_spec(dsk.shape), _const_spec(ng.shape)] + s_in,
        out_specs=[pl.BlockSpec((l, BR_W), lambda i: (i, 0))] + s_out,
        out_shape=[jax.ShapeDtypeStruct((b * l, BR_W), BF16)] + s_shapes,
        scratch_shapes=[pltpu.VMEM((l + 2 * SUBLANES, 2 * BR_W), F32),
                        pltpu.VMEM((l, 2 * BR_W), BF16),
                        pltpu.VMEM((l, BR_W), F32),
                        pltpu.VMEM((l, LANES), F32),
                        pltpu.VMEM((l, LANES), F32),
                        pltpu.VMEM(packed, F32)],
        compiler_params=_CP(dimension_semantics=("arbitrary",)),
        name="ssd",
    )(ssm_in, prec, cw, cb, dtb, na, dsk, ng, *s_args)


_HG_LEVELS = (64, 32, 16, 8, 4, 2)
LOG2_E = 1.4426950408889634


def _block_ref_rows(x, m, row):
    size = 2 * m
    if size >= 2 * SUBLANES:
        return _bcast_block_row(x, size, row)
    out = _bcast_group_row(x, row)
    sub = _iota2((CHUNK, LANES), 0) % SUBLANES
    for b in range(1, SUBLANES // size):
        out = jnp.where(sub >= b * size, _bcast_group_row(x, b * size + row), out)
    return out


def _bcast_group_row(x, j):
    x3 = x.reshape(CHUNK // SUBLANES, SUBLANES, LANES)
    r = jnp.broadcast_to(x3[:, j:j + 1, :], x3.shape)
    return r.reshape(CHUNK, LANES)


def _bcast_block_row(x, size, j):
    pieces = [jnp.broadcast_to(x[b * size + j:b * size + j + 1, :], (size, LANES)) for b in range(CHUNK // size)]
    return pieces[0] if len(pieces) == 1 else jnp.concatenate(pieces, axis=0)


def _hgrn_kernel(is_ctx, n_chunks, in_ref, f_ref, llb_ref, oml_ref, l1m_ref, ng_ref, *rest):
    if is_ctx:
        o_ref, sfin_ref, oacc, st, cumb_s, keyb_s = rest
    else:
        s0_ref, o_ref, oacc, st, cumb_s, keyb_s = rest
    ii = _iota2((CHUNK, CHUNK), 0)
    jj = _iota2((CHUNK, CHUNK), 1)
    tril = jnp.where(ii >= jj, 1.0, 0.0).astype(BF16)
    triu = jnp.where(jj >= ii, 1.0, 0.0).astype(BF16)
    odd = (ii % 2) == 1
    same_pair = (ii // 2) == (jj // 2)
    ones_sq = jnp.ones((LANES, CHUNK), BF16)
    if is_ctx:
        st[...] = jnp.zeros(st.shape, F32)
    else:
        for d in range(2):
            for h in range(H_C):
                st[d, h] = s0_ref[0, d, h].T

    def gates(rows, d):
        fr = f_ref[rows, d * BR_W:(d + 1) * BR_W]
        ds_ = slice(d * BR_W, (d + 1) * BR_W)
        a = llb_ref[:, ds_]
        e = jnp.exp(-jnp.abs(fr))
        bterm = l1m_ref[:, ds_] + jnp.minimum(fr, 0.0) - jnp.log1p(e)
        logf = jnp.maximum(a, bterm) + jnp.log1p(jnp.exp(-jnp.abs(a - bterm)))
        key = oml_ref[:, ds_] * (jnp.where(fr >= 0.0, e, 1.0) / (1.0 + e))
        return logf * LOG2_E, key

    def fwd(c, carry):
        rows = _rows(c)
        lgf_all, key_f = gates(rows, 0)
        lgb_all, key_b = gates(rows, 1)
        cumf_all = _tri_dot(tril, lgf_all)
        cumb_all = _tri_dot(triu, lgb_all)
        cumb_s[rows, :] = cumb_all
        keyb_s[rows, :] = key_b
        for h in range(H_C):
            hs = slice(h * LANES, (h + 1) * LANES)
            q = in_ref[rows, hs].astype(F32)
            v = in_ref[rows, BR_W + h * LANES:BR_W + (h + 1) * LANES]
            cumf, cumb = cumf_all[:, hs], cumb_all[:, hs]
            kf, kb = key_f[:, hs], key_b[:, hs]
            sc = jnp.zeros((CHUNK, CHUNK), F32)
            for m in _HG_LEVELS:
                upper = (ii % (2 * m)) >= m
                same_block = (ii // (2 * m)) == (jj // (2 * m))
                ref_f = _block_ref_rows(cumf, m, m - 1)
                ref_b = _block_ref_rows(cumb, m, m)
                e_f = jnp.exp2(jnp.where(upper, cumf - ref_f, ref_f - cumf))
                e_b = jnp.exp2(jnp.where(upper, ref_b - cumb, cumb - ref_b))
                qcat = jnp.concatenate([jnp.where(upper, q * e_f, 0.0), jnp.where(upper, 0.0, q * e_b)], axis=1)
                kcat = jnp.concatenate([jnp.where(upper, 0.0, kf * e_f), jnp.where(upper, kb * e_b, 0.0)], axis=1)
                sc = sc + jnp.where(same_block, _dot_nt(qcat.astype(BF16), kcat.astype(BF16)), 0.0)
            qcat = jnp.concatenate([jnp.where(odd, q * jnp.exp2(lgf_all[:, hs]), 0.0),
                                    jnp.where(odd, 0.0, q * jnp.exp2(lgb_all[:, hs]))], axis=1)
            kcat = jnp.concatenate([jnp.where(odd, 0.0, kf), jnp.where(odd, kb, 0.0)], axis=1)
            sc = sc + jnp.where(same_pair, _dot_nt(qcat.astype(BF16), kcat.astype(BF16)), 0.0)
            sc = sc + jnp.where(ii == jj, _dot((q * (kf + kb)).astype(BF16), ones_sq), 0.0)
            intra = _dot(sc.astype(BF16), v)
            stf = st[0, h]
            totf = cumf[CHUNK - 1:CHUNK, :]
            inter = _dot_nt((q * jnp.exp2(cumf)).astype(BF16), stf.astype(BF16))
            oacc[rows, hs] = intra + inter
            st[0, h] = stf * jnp.exp2(totf) + _dot_tn(v, (kf * jnp.exp2(totf - cumf)).astype(BF16))
        return carry

    lax.fori_loop(0, n_chunks, fwd, 0)

    def bwd(t, carry):
        rows = _rows(n_chunks - 1 - t)
        cumb_all, key_b = cumb_s[rows, :], keyb_s[rows, :]
        for h in range(H_C):
            hs = slice(h * LANES, (h + 1) * LANES)
            q = in_ref[rows, hs].astype(F32)
            v = in_ref[rows, BR_W + h * LANES:BR_W + (h + 1) * LANES]
            cumb, kb = cumb_all[:, hs], key_b[:, hs]
            stb = st[1, h]
            totb = cumb[0:1, :]
            oacc[rows, hs] = oacc[rows, hs] + _dot_nt((q * jnp.exp2(cumb)).astype(BF16), stb.astype(BF16))
            st[1, h] = stb * jnp.exp2(totb) + _dot_tn(v, (kb * jnp.exp2(totb - cumb)).astype(BF16))
        return carry

    lax.fori_loop(0, n_chunks, bwd, 0)
    if is_ctx:
        for d in range(2):
            for h in range(H_C):
                sfin_ref[0, d, h] = st[d, h].T

    def fin(c, carry):
        rows = _rows(c)
        for h in range(H_C):
            hs = slice(h * LANES, (h + 1) * LANES)
            o = oacc[rows, hs]
            y = o * lax.rsqrt(jnp.mean(o * o, axis=-1, keepdims=True) + EPS) * ng_ref[:, hs]
            g = in_ref[rows, 2 * BR_W + h * LANES:2 * BR_W + (h + 1) * LANES].astype(F32)
            o_ref[rows, hs] = (y * _silu(g)).astype(o_ref.dtype)
        return carry

    lax.fori_loop(0, n_chunks, fin, 0)


def _hgrn(hg_in, prec, row0, b, l, llb, oml, l1m, ng, s0):
    blk0 = row0 // l
    s_in, s_args, s_out, s_shapes = _state_io(s0, b, (2, H_C, E_C, DV_C), (2, H_C, E_C, DV_C))
    return pl.pallas_call(
        functools.partial(_hgrn_kernel, s0 is None, l // CHUNK),
        grid=(b,),
        in_specs=[pl.BlockSpec((l, hg_in.shape[1]), lambda i: (blk0 + i, 0)),
                  pl.BlockSpec((l, 2 * BR_W), lambda i: (blk0 + i, 0)),
                  _const_spec(llb.shape), _const_spec(oml.shape), _const_spec(l1m.shape),
                  _const_spec(ng.shape)] + s_in,
        out_specs=[pl.BlockSpec((l, BR_W), lambda i: (i, 0))] + s_out,
        out_shape=[jax.ShapeDtypeStruct((b * l, BR_W), BF16)] + s_shapes,
        scratch_shapes=[pltpu.VMEM((l, BR_W), F32),
                        pltpu.VMEM((2, H_C, DV_C, E_C), F32),
                        pltpu.VMEM((l, BR_W), F32),
                        pltpu.VMEM((l, BR_W), F32)],
        compiler_params=_CP(dimension_semantics=("arbitrary",)),
        name="hgrn2",
    )(hg_in, prec, llb, oml, l1m, ng, *s_args)


def _merge_kernel(n_ctx_tiles, xc_ref, xl_ref, mod_ref, n2_ref, o0c, o0l, o1c, o1l, o2c, o2l, gl_ref, wb_ref, wo_ref,
                  xo_ref, h2_ref):
    d = xc_ref.shape[1]
    merged = jnp.zeros(xc_ref.shape, F32)
    for k, (oc, ol) in enumerate(((o0c, o0l), (o1c, o1l), (o2c, o2l))):
        gate = _sigmoid(gl_ref[:, k * d:(k + 1) * d].astype(F32))
        merged = merged + gate * _dot(_group_pick(n_ctx_tiles, oc, ol), wb_ref[k])
    mix = _dot(merged.astype(BF16), wo_ref[...])
    xn = _group_pick(n_ctx_tiles, xc_ref, xl_ref) + mod_ref[0, 2:3, :] * mix
    xo_ref[...] = xn
    h2 = _modulated_norm(xn, n2_ref[...], mod_ref[0, 3:4, :], mod_ref[0, 4:5, :])
    h2_ref[...] = h2.astype(h2_ref.dtype)


def _merge(x_pair, mod, norm_g, o_pairs, gl, wb, wo, mod_index):
    xc, xl = x_pair
    d = xc.shape[1]
    t = xc.shape[0] + xl.shape[0]
    n_ctx_tiles = xc.shape[0] // ROW_TILE
    row = lambda r: (r, 0)
    o_specs, o_args = [], []
    for pair in o_pairs:
        o_specs += _group_specs(BR_W, n_ctx_tiles)
        o_args += list(pair)
    return pl.pallas_call(
        functools.partial(_merge_kernel, n_ctx_tiles),
        grid=(t // ROW_TILE,),
        in_specs=_group_specs(d, n_ctx_tiles) + [pl.BlockSpec((1, 6, d), mod_index),
                                                 pl.BlockSpec((1, d), lambda r: (0, 0))] + o_specs + [
                  pl.BlockSpec((ROW_TILE, N_BRANCH * d), row),
                  pl.BlockSpec(wb.shape, lambda r: (0, 0, 0), pipeline_mode=pl.Buffered(1)),
                  pl.BlockSpec(wo.shape, lambda r: (0, 0), pipeline_mode=pl.Buffered(1))],
        out_specs=[pl.BlockSpec((ROW_TILE, d), row),
                   pl.BlockSpec((ROW_TILE, d), row)],
        out_shape=[jax.ShapeDtypeStruct((t, d), F32),
                   jax.ShapeDtypeStruct((t, d), BF16)],
        compiler_params=_CP(dimension_semantics=("arbitrary",)),
        name="merge_outproj",
    )(xc, xl, mod, norm_g.reshape(1, d), *o_args, gl, wb, wo)


def _router_kernel(x_ref, mod_ref, n2_ref, rwt_ref, rb_ref, rloc_ref, w_ref, cnt_ref):
    tm = x_ref.shape[0]
    h2 = _modulated_norm(x_ref[...], n2_ref[...], mod_ref[0, 3:4, :], mod_ref[0, 4:5, :])
    logits = lax.dot_general(rwt_ref[...], h2, (((1,), (1,)), ((), ())),
                             preferred_element_type=F32, precision=HIGHEST)
    scores = _sigmoid(logits)
    biased = scores + rb_ref[...]
    gsz = N_EXPERTS // N_GROUPS
    neg_inf = -jnp.inf

    b3 = biased.reshape(N_GROUPS, gsz, tm)
    e_in_g = lax.broadcasted_iota(jnp.int32, (N_GROUPS, gsz, tm), 1)
    m1 = jnp.max(b3, axis=1, keepdims=True)
    first = jnp.min(jnp.where(b3 == m1, e_in_g, gsz), axis=1, keepdims=True)
    m2 = jnp.max(jnp.where(e_in_g == first, neg_inf, b3), axis=1, keepdims=True)
    gscore = m1 + m2

    g_iota = lax.broadcasted_iota(jnp.int32, (N_GROUPS, 1, tm), 0)
    chosen = jnp.zeros((N_GROUPS, 1, tm), jnp.int32)
    for _ in range(TOPK_GROUPS):
        m = jnp.max(gscore, axis=0, keepdims=True)
        first = jnp.min(jnp.where(gscore == m, g_iota, N_GROUPS), axis=0, keepdims=True)
        hit = g_iota == first
        chosen = jnp.where(hit, 1, chosen)
        gscore = jnp.where(hit, neg_inf, gscore)
    emask = jnp.broadcast_to(chosen, (N_GROUPS, gsz, tm)).reshape(N_EXPERTS, tm)

    cand = jnp.where(emask > 0, biased, neg_inf)
    e_iota = _iota2((N_EXPERTS, tm), 0)
    hits, ws = [], []
    for _ in range(TOP_K):
        m = jnp.max(cand, axis=0, keepdims=True)
        first = jnp.min(jnp.where(cand == m, e_iota, N_EXPERTS), axis=0, keepdims=True)
        hit = e_iota == first
        hits.append(hit)
        ws.append(jnp.sum(jnp.where(hit, scores, 0.0), axis=0, keepdims=True))
        cand = jnp.where(hit, neg_inf, cand)
    wsum = ws[0]
    for w in ws[1:]:
        wsum = wsum + w
    pad = SUBLANES - TOP_K
    w_ref[...] = jnp.concatenate([ROUTED_SCALE * w / wsum for w in ws] + [jnp.zeros((pad, tm), F32)], axis=0)

    picked = jnp.zeros((N_EXPERTS, tm), F32)
    for hit in hits:
        picked = jnp.where(hit, 1.0, picked)
    picked = picked.astype(BF16)
    earlier_tok = jnp.where(_iota2((tm, tm), 0) < _iota2((tm, tm), 1), 1.0, 0.0).astype(BF16)
    before_in_expert = _dot(picked, earlier_tok)
    count_rep = _dot(picked, jnp.ones((tm, tm), BF16))
    seg_rows = jnp.floor((count_rep + (BF16_ROWS - 1.0)) * (1.0 / BF16_ROWS)) * BF16_ROWS
    lower_expert = jnp.where(_iota2((N_EXPERTS, N_EXPERTS), 1) < _iota2((N_EXPERTS, N_EXPERTS), 0), 1.0, 0.0)
    pos = _dot(lower_expert.astype(BF16), seg_rows.astype(BF16)) + before_in_expert
    rloc = [jnp.sum(jnp.where(hit, pos, 0.0), axis=0, keepdims=True).astype(jnp.int32) for hit in hits]
    rloc_ref[...] = jnp.concatenate(rloc + [jnp.zeros((pad, tm), jnp.int32)], axis=0)
    cnt_ref[0] = _dot_nt(jnp.ones((SUBLANES, tm), BF16), picked)


def _router(x_new, mod, norm_g, rwt, rb, mod_index):
    t, d = x_new.shape
    n_tiles = t // ROW_TILE
    return pl.pallas_call(
        _router_kernel,
        grid=(n_tiles,),
        in_specs=[pl.BlockSpec((ROW_TILE, d), lambda r: (r, 0)),
                  pl.BlockSpec((1, 6, d), mod_index),
                  pl.BlockSpec((1, d), lambda r: (0, 0)),
                  pl.BlockSpec(rwt.shape, lambda r: (0, 0)),
                  pl.BlockSpec(rb.shape, lambda r: (0, 0))],
        out_specs=[pl.BlockSpec((SUBLANES, ROW_TILE), lambda r: (0, r)),
                   pl.BlockSpec((SUBLANES, ROW_TILE), lambda r: (0, r)),
                   pl.BlockSpec((1, SUBLANES, N_EXPERTS), lambda r: (r, 0, 0))],
        out_shape=[jax.ShapeDtypeStruct((SUBLANES, t), jnp.int32),
                   jax.ShapeDtypeStruct((SUBLANES, t), F32),
                   jax.ShapeDtypeStruct((n_tiles, SUBLANES, N_EXPERTS), F32)],
        compiler_params=_CP(dimension_semantics=("arbitrary",)),
        name="router",
    )(x_new, mod, norm_g.reshape(1, d), rwt, rb)


BF16_ROWS = 2 * SUBLANES
SEG_ROWS = 2 * BF16_ROWS
TILE_SLOTS = -(-(ROW_TILE * TOP_K + N_EXPERTS * (BF16_ROWS - 1)) // ROW_TILE) * ROW_TILE
_COPY_CLASSES = ((SEG_ROWS, TILE_SLOTS // SEG_ROWS), (BF16_ROWS, N_EXPERTS + 1))
_COPY_COUNTS_AT = 2 * sum(cap for _, cap in _COPY_CLASSES)
_COPY_LIST_LEN = -(-(_COPY_COUNTS_AT + len(_COPY_CLASSES)) // LANES) * LANES


def _copy_lists(cnt, lstart, goff):
    n_tiles, n_seg = cnt.shape
    segments = jnp.arange(n_seg, dtype=jnp.int32)

    def expand(per_e, cap):
        cum = jnp.cumsum(per_e, axis=1)
        j = jnp.arange(cap, dtype=jnp.int32)
        e_of_j = jnp.minimum(jnp.sum((cum[:, None, :] <= j[None, :, None]).astype(jnp.int32), axis=2), n_seg - 1)
        onehot = (e_of_j[:, :, None] == segments[None, None, :]).astype(jnp.int32)
        take = lambda v: jnp.sum(onehot * v[:, None, :], axis=2)
        return take, j[None, :] - take(cum - per_e), cum[:, -1]

    parts, counts = [], []
    for size, cap in _COPY_CLASSES:
        if size == SEG_ROWS:
            take, k, total = expand(cnt // SEG_ROWS, cap)
            off = k * SEG_ROWS
        else:
            take, _, total = expand((jnp.bitwise_and(cnt, size) > 0).astype(jnp.int32), cap)
            n = take(cnt)
            off = n - n % (2 * size)
        parts += [take(lstart) + off, take(goff) + off]
        counts.append(total)
    row = jnp.concatenate(parts + [jnp.stack(counts, axis=1)], axis=1)
    row = jnp.pad(row, ((0, 0), (0, _COPY_LIST_LEN - row.shape[1])))
    return row.astype(jnp.int32).reshape(n_tiles, 1, _COPY_LIST_LEN)


def _segment_copies(list_ref, make_copy):
    def rows(first, n_rows):
        return pl.ds(pl.multiple_of(first, BF16_ROWS), n_rows)

    base = 0
    for k, (size, cap) in enumerate(_COPY_CLASSES):
        def issue(j, carry, base=base, size=size, cap=cap):
            make_copy(rows(list_ref[0, 0, base + j], size), rows(list_ref[0, 0, base + cap + j], size)).start()
            return carry

        lax.fori_loop(0, list_ref[0, 0, _COPY_COUNTS_AT + k], issue, 0)
        base += 2 * cap


def _dispatch_kernel(h2_ref, rloc_ref, list_ref, xs_hbm, buf, sem):
    i = pl.program_id(0)
    n_tiles = pl.num_programs(0)
    slot = i % 2
    tm, d = h2_ref.shape

    def all_copies(s):
        return pltpu.make_async_copy(buf.at[s], xs_hbm.at[pl.ds(0, TILE_SLOTS)], sem.at[s])

    @pl.when(i >= 2)
    def _():
        all_copies(slot).wait()

    rid = _iota2((TILE_SLOTS, tm), 0)
    perm = jnp.zeros((TILE_SLOTS, tm), F32)
    for k in range(TOP_K):
        perm = jnp.where(rid == rloc_ref[k:k + 1, :], 1.0, perm)
    perm = perm.astype(BF16)
    for c in range(0, d, 2 * LANES):
        buf[slot, :, c:c + 2 * LANES] = _dot(perm, h2_ref[:, c:c + 2 * LANES]).astype(BF16)

    _segment_copies(list_ref, lambda loc, glob: pltpu.make_async_copy(buf.at[slot, loc], xs_hbm.at[glob], sem.at[slot]))

    @pl.when(i == n_tiles - 1)
    def _():
        all_copies(slot).wait()

        @pl.when(n_tiles >= 2)
        def _():
            all_copies(1 - slot).wait()


def _dispatch(h2, rloc, lists):
    t, d = h2.shape
    n_tiles = t // ROW_TILE
    return pl.pallas_call(
        _dispatch_kernel,
        grid=(n_tiles,),
        in_specs=[pl.BlockSpec((ROW_TILE, d), lambda i: (i, 0)),
                  pl.BlockSpec((SUBLANES, ROW_TILE), lambda i: (0, i)),
                  pl.BlockSpec((1, 1, _COPY_LIST_LEN), lambda i: (i, 0, 0), memory_space=pltpu.SMEM)],
        out_specs=pl.BlockSpec(memory_space=pl.ANY),
        out_shape=jax.ShapeDtypeStruct((n_tiles * TILE_SLOTS, d), BF16),
        scratch_shapes=[pltpu.VMEM((2, TILE_SLOTS, d), BF16),
                        pltpu.SemaphoreType.DMA((2,))],
        compiler_params=_CP(dimension_semantics=("arbitrary",)),
        name="moe_dispatch",
    )(h2, rloc, lists)


def _expert_kernel(blk_ref, exp_ref, lo_ref, hi_ref, n_ref, x_ref, wg_ref, wu_ref, wdn_ref, y_ref,
                   wgu_s, wd_s):
    i = pl.program_id(0)
    prev = jnp.maximum(i - 1, 0)
    valid = i < n_ref[0]
    is_expert = exp_ref[i] < N_EXPERTS
    new_expert = jnp.logical_or(i == 0, exp_ref[i] != exp_ref[prev])
    first_of_block = jnp.logical_or(i == 0, blk_ref[i] != blk_ref[prev])
    rows = _iota2(y_ref.shape, 0)
    mine = jnp.logical_and(rows >= lo_ref[i], rows < hi_ref[i])

    def put(y):
        @pl.when(first_of_block)
        def _():
            y_ref[...] = jnp.where(mine, y, 0.0).astype(y_ref.dtype)

        @pl.when(jnp.logical_not(first_of_block))
        def _():
            y_ref[...] = jnp.where(mine, y, y_ref[...].astype(F32)).astype(y_ref.dtype)

    @pl.when(jnp.logical_and(jnp.logical_and(valid, is_expert), new_expert))
    def _():
        wgu_s[:, :D_EXPERT] = wg_ref[0, 0].astype(BF16)
        wgu_s[:, D_EXPERT:] = wu_ref[0, 0].astype(BF16)
        wd_s[...] = wdn_ref[0, 0].astype(BF16)

    @pl.when(jnp.logical_and(valid, is_expert))
    def _():
        gu = _dot(x_ref[...], wgu_s[...])
        act = _silu(gu[:, :D_EXPERT]) * gu[:, D_EXPERT:]
        put(_dot(act.astype(BF16), wd_s[...]))

    @pl.when(jnp.logical_and(valid, jnp.logical_not(is_expert)))
    def _():
        put(jnp.zeros(y_ref.shape, F32))


def _experts(xs, items, layer, wg, wu, wdn):
    blk, exp, lo, hi, n_items = items
    d = xs.shape[1]
    by_block = lambda i, blk_r, exp_r, lo_r, hi_r, n_r: (blk_r[i], 0)
    by_expert = lambda i, blk_r, exp_r, lo_r, hi_r, n_r: (layer, jnp.minimum(exp_r[i], N_EXPERTS - 1), 0, 0)
    grid_spec = pltpu.PrefetchScalarGridSpec(
        num_scalar_prefetch=5,
        grid=(blk.shape[0],),
        in_specs=[pl.BlockSpec((MOE_ROWS, d), by_block),
                  pl.BlockSpec((1, 1) + wg.shape[2:], by_expert),
                  pl.BlockSpec((1, 1) + wu.shape[2:], by_expert),
                  pl.BlockSpec((1, 1) + wdn.shape[2:], by_expert)],
        out_specs=pl.BlockSpec((MOE_ROWS, d), by_block),
        scratch_shapes=[pltpu.VMEM((wg.shape[2], 2 * D_EXPERT), BF16),
                        pltpu.VMEM(wdn.shape[2:], BF16)])
    return pl.pallas_call(
        _expert_kernel,
        grid_spec=grid_spec,
        out_shape=jax.ShapeDtypeStruct(xs.shape, BF16),
        compiler_params=_CP(dimension_semantics=("arbitrary",)),
        name="routed_experts",
    )(blk, exp, lo, hi, n_items, xs, wg, wu, wdn)


def _moe_plan(cnt_tiles, t):
    p = t // ROW_TILE * TILE_SLOTS
    assert p % MOE_ROWS == 0
    n_seg = N_EXPERTS + 1
    cnt = cnt_tiles[:, 0, :].astype(jnp.int32)
    cnt = (cnt + BF16_ROWS - 1) // BF16_ROWS * BF16_ROWS
    cnt = jnp.concatenate([cnt, TILE_SLOTS - jnp.sum(cnt, axis=1, keepdims=True)], axis=1)
    totals = jnp.sum(cnt, axis=0)
    ends = jnp.cumsum(totals)
    starts = ends - totals
    goff = starts[None, :] + jnp.cumsum(cnt, axis=0) - cnt
    lstart = jnp.cumsum(cnt, axis=1) - cnt
    n_blk = jnp.where(totals > 0, (ends - 1) // MOE_ROWS - starts // MOE_ROWS + 1, 0)
    item_end = jnp.cumsum(n_blk)
    n_items = item_end[-1]
    max_items = p // MOE_ROWS + n_seg
    it = jnp.minimum(jnp.arange(max_items, dtype=jnp.int32), jnp.maximum(n_items - 1, 0))
    exp = jnp.minimum(jnp.sum((item_end[None, :] <= it[:, None]).astype(jnp.int32), axis=1), n_seg - 1)
    sel = (exp[:, None] == jnp.arange(n_seg, dtype=jnp.int32)[None, :]).astype(jnp.int32)
    pick = lambda v: jnp.sum(sel * v[None, :], axis=1)
    blk = jnp.clip(pick(starts) // MOE_ROWS + it - pick(item_end - n_blk), 0, p // MOE_ROWS - 1)
    lo = jnp.maximum(pick(starts), blk * MOE_ROWS) - blk * MOE_ROWS
    hi = jnp.minimum(pick(ends), (blk + 1) * MOE_ROWS) - blk * MOE_ROWS
    items = tuple(a.astype(jnp.int32) for a in (blk, exp, lo, hi, n_items.reshape(1)))
    return _copy_lists(cnt, lstart, goff), items


def _combine_kernel(final, n_ctx_tiles, x_ref, h2_ref, w_ref, rloc_ref, mod_ref, fn_ref, wsgu_ref, wsd_ref,
                    list_ref, list_next_ref, y_hbm, oc_ref, ol_ref, ybuf, sem):
    i = pl.program_id(0)
    n_tiles = pl.num_programs(0)
    slot = i % 2
    tm = x_ref.shape[0]

    def fetch(lists, s):
        _segment_copies(lists, lambda loc, glob: pltpu.make_async_copy(y_hbm.at[glob], ybuf.at[s, loc], sem.at[s]))

    @pl.when(i == 0)
    def _():
        fetch(list_ref, 0)

    @pl.when(i + 1 < n_tiles)
    def _():
        fetch(list_next_ref, 1 - slot)

    gu = _dot(h2_ref[...], wsgu_ref[...])
    shared = _dot((_silu(gu[:, :D_SHARED]) * gu[:, D_SHARED:]).astype(BF16), wsd_ref[...])

    rid = _iota2((tm, TILE_SLOTS), 1)
    w = w_ref[...]
    unsort = jnp.zeros((tm, TILE_SLOTS), F32)
    for k in range(TOP_K):
        unsort = jnp.where(rid == rloc_ref[:, k:k + 1], w[:, k:k + 1], unsort)
    unsort = unsort.astype(BF16)

    pltpu.make_async_copy(y_hbm.at[pl.ds(0, TILE_SLOTS)], ybuf.at[slot], sem.at[slot]).wait()
    routed = _dot(unsort, ybuf[slot])
    xo = x_ref[...] + mod_ref[0, 5:6, :] * (routed + shared)
    if final:
        xo = xo * lax.rsqrt(jnp.mean(xo * xo, axis=-1, keepdims=True) + EPS) * fn_ref[...]

    @pl.when(i < n_ctx_tiles)
    def _():
        oc_ref[...] = xo

    @pl.when(i >= n_ctx_tiles)
    def _():
        ol_ref[...] = xo


def _combine(final, n_ctx_tiles, x_new, h2, y, w_tk, rloc_tk, lists, mod, final_norm, wsgu, wsd, mod_index):
    t, d = x_new.shape
    n_tiles = t // ROW_TILE
    row = lambda r: (r, 0)
    cur = lambda r: (r, 0, 0)
    nxt = lambda r: (jnp.minimum(r + 1, n_tiles - 1), 0, 0)
    smem_tile = functools.partial(pl.BlockSpec, (1, 1, _COPY_LIST_LEN), memory_space=pltpu.SMEM)
    t_ctx = n_ctx_tiles * ROW_TILE
    return pl.pallas_call(
        functools.partial(_combine_kernel, final, n_ctx_tiles),
        grid=(n_tiles,),
        in_specs=[pl.BlockSpec((ROW_TILE, d), row),
                  pl.BlockSpec((ROW_TILE, d), row),
                  pl.BlockSpec((ROW_TILE, SUBLANES), row),
                  pl.BlockSpec((ROW_TILE, SUBLANES), row),
                  pl.BlockSpec((1, 6, d), mod_index),
                  pl.BlockSpec((1, d), lambda r: (0, 0)),
                  pl.BlockSpec(wsgu.shape, lambda r: (0, 0)),
                  pl.BlockSpec(wsd.shape, lambda r: (0, 0)),
                  smem_tile(cur), smem_tile(nxt),
                  pl.BlockSpec(memory_space=pl.ANY)],
        out_specs=_group_specs(d, n_ctx_tiles),
        out_shape=[jax.ShapeDtypeStruct((t_ctx, d), F32), jax.ShapeDtypeStruct((t - t_ctx, d), F32)],
        scratch_shapes=[pltpu.VMEM((2, TILE_SLOTS, d), BF16),
                        pltpu.SemaphoreType.DMA((2,))],
        compiler_params=_CP(dimension_semantics=("arbitrary",)),
        name="combine_shared",
    )(x_new, h2, w_tk, rloc_tk, mod, final_norm.reshape(1, d), wsgu, wsd, lists, lists, y)


def _qk_perm():
    half = DK_R // 2
    perm = np.zeros(RET_QK_W, np.int32)
    for part in range(2):
        for h in range(H_R):
            for j in range(half):
                perm[part * LANES + h * half + j] = h * DK_R + part * half + j
    return perm


def _ret_state_rows():
    half = DK_R // 2
    rows = np.zeros((H_R, DK_R), np.int32)
    for h in range(H_R):
        for j in range(DK_R):
            rows[h, j] = (j // half) * LANES + h * half + j % half
    return rows


def _xbc_channels():
    ch = list(range(BR_W))
    for base in (BR_W, BR_W + G_M * N_M):
        for g in range(G_M):
            grp = list(range(base + g * N_M, base + (g + 1) * N_M))
            ch += grp + grp
    return np.asarray(ch, np.int32)


def _pad_lanes(v):
    return jnp.zeros((1, LANES), F32).at[0, :v.shape[0]].set(v.astype(F32))


def _ret_state_pack(s):
    rows = _ret_state_rows()
    src = np.full((H_R, 2 * LANES), DK_R, np.int32)
    for h in range(H_R):
        src[h, rows[h]] = np.arange(DK_R)
    sz = jnp.concatenate([s, jnp.zeros(s.shape[:3] + (1, DV_R), s.dtype)], axis=3)
    return sz[:, :, np.arange(H_R)[:, None], src, :]


def _ssm_state_pack(s):
    b = s.shape[0]
    sr = s.reshape(b, 2, H_M // 2, 2, N_M, 1, P_M)
    eye = jnp.eye(2, dtype=s.dtype).reshape(1, 1, 1, 2, 1, 2, 1)
    return (sr * eye).reshape(b, 2, H_M // 2, 2 * N_M, 2 * P_M)


def _grid_rope(l):
    rows = l // GRID_W
    row = jnp.repeat(jnp.arange(rows), GRID_W).astype(F32)
    col = (jnp.arange(rows * GRID_W) % GRID_W).astype(F32)
    n_freq = DK_R // 4
    freqs = ROPE_BASE ** (-jnp.arange(n_freq, dtype=F32) / n_freq)
    ang = jnp.concatenate([row[:, None] * freqs, col[:, None] * freqs], axis=-1)
    return jnp.tile(jnp.cos(ang), (1, H_R)), jnp.tile(jnp.sin(ang), (1, H_R))


def _layer_weights(i, w_in, ssm_conv_w, ssm_conv_b):
    cuts = np.cumsum(IN_SPLITS)[:-1]
    rq, rk, rv, rg, sz, sxbc, sdt, hq, hf, hi, hg, gl = jnp.split(w_in[i], cuts, axis=1)
    perm = _qk_perm()
    ch = _xbc_channels()
    w_ret = jnp.concatenate([rq[:, perm], rk[:, perm] * (DK_R ** -0.5), rv, rg], axis=1).astype(BF16)
    w_ssm = jnp.concatenate([sz, sxbc[:, ch]], axis=1).astype(BF16)
    w_hg = jnp.concatenate([hq, hi, hg], axis=1).astype(BF16)
    w_prec = jnp.concatenate([hf, sdt, jnp.zeros((w_in.shape[1], LANES - SSM_DT_W), F32)], axis=1).astype(BF16)
    conv_w = ssm_conv_w[i][ch].T
    conv_b = ssm_conv_b[i][ch][None, :]
    return (w_ret, w_ssm, w_hg, gl.astype(BF16), w_prec), conv_w, conv_b


def kernel(x_prompt, x_sample, state_ret, state_ssm, state_hgrn, c, c_ctx, ada_w, ada_b, norm1, norm2, w_in,
           ret_decay_logit, ret_gn, ssm_conv_w, ssm_conv_b, ssm_a_log, ssm_dt_bias, ssm_d, ssm_norm,
           hgrn_lb_logits, hgrn_norm, w_branch, w_out, router_w, router_b, exp_w_gate, exp_w_up,
           exp_w_down, sh_w_gate, sh_w_up, sh_w_down, final_norm):
    bc, lc, d = x_prompt.shape
    bl, ll, _ = x_sample.shape
    tc, tl = bc * lc, bl * ll
    t = tc + tl
    assert tc % ROW_TILE == 0 and ll % ROW_TILE == 0 and lc % CHUNK == 0 and ll % CHUNK == 0
    assert tl % lc == 0 and tc % ll == 0 and 1 + bl <= SUBLANES
    mod_index = _mod_index(tc // ROW_TILE, ll // ROW_TILE)

    n_ctx_tiles = tc // ROW_TILE
    x_pair = (x_prompt.reshape(tc, d), x_sample.reshape(tl, d))
    c_all = jnp.concatenate([c_ctx[None, :], c, jnp.zeros((SUBLANES - 1 - bl, d), F32)], axis=0)
    mods = _ada(c_all, ada_w, ada_b)

    lb_all = jnp.cumsum(jax.nn.softmax(hgrn_lb_logits.astype(F32), axis=1), axis=1)
    lb_all = lb_all - lb_all[:, :1]
    cos4, sin4 = _grid_rope(ll)
    lane_head = (np.arange(2 * LANES) % LANES) // (DK_R // 2)

    new_ret, new_ssm, new_hg = [], [], []
    for i in range(DEPTH):
        mod = mods[i].reshape(SUBLANES, 6, d)
        weights, conv_w, conv_b = _layer_weights(i, w_in, ssm_conv_w, ssm_conv_b)
        ret_in, ssm_in, hg_in, gl, prec = _inproj(x_pair, mod, norm1[i], weights, (BF16, BF16, BF16, BF16, F32),
                                                  mod_index)

        lg = jax.nn.log_sigmoid(ret_decay_logit[i].astype(F32))
        lgl = lg[:, lane_head]
        lgv = jnp.broadcast_to(lg[:, :, None], (2, H_R, LANES))
        gn = ret_gn[i][None, :]
        o_ret_c, s_ret = _retention(ret_in, 0, bc, lc, None, None, lgl, lgv, gn, None)
        o_ret_l, = _retention(ret_in, tc, bl, ll, cos4, sin4, lgl, lgv, gn, _ret_state_pack(state_ret[:, i]))
        new_ret.append(s_ret)

        dtb = _pad_lanes(ssm_dt_bias[i].reshape(-1))
        na = _pad_lanes(-jnp.exp(ssm_a_log[i].astype(F32)).reshape(-1))
        dsk = jnp.repeat(ssm_d[i], P_M)[None, :]
        ng = ssm_norm[i][None, :]
        o_ssm_c, s_ssm = _ssd(ssm_in, prec, 0, bc, lc, conv_w, conv_b, dtb, na, dsk, ng, None)
        o_ssm_l, = _ssd(ssm_in, prec, tc, bl, ll, conv_w, conv_b, dtb, na, dsk, ng,
                        _ssm_state_pack(state_ssm[:, i]))
        new_ssm.append(s_ssm)

        lb = lb_all[:, i]
        llb = jnp.log(lb).reshape(1, 2 * BR_W)
        oml = (1.0 - lb).reshape(1, 2 * BR_W)
        l1m = jnp.log1p(-lb).reshape(1, 2 * BR_W)
        hn = hgrn_norm[i][None, :]
        o_hg_c, s_hg = _hgrn(hg_in, prec, 0, bc, lc, llb, oml, l1m, hn, None)
        o_hg_l, = _hgrn(hg_in, prec, tc, bl, ll, llb, oml, l1m, hn, state_hgrn[:, i])
        new_hg.append(s_hg)

        x_new, h2 = _merge(x_pair, mod, norm2[i], ((o_ret_c, o_ret_l), (o_ssm_c, o_ssm_l), (o_hg_c, o_hg_l)), gl,
                           w_branch[i].astype(BF16), w_out[i].astype(BF16), mod_index)

        rloc, w8, cnt_tiles = _router(x_new, mod, norm2[i], router_w[i].T, router_b[i][:, None], mod_index)
        lists, items = _moe_plan(cnt_tiles, t)
        xs = _dispatch(h2, rloc, lists)
        y = _experts(xs, items, i, exp_w_gate, exp_w_up, exp_w_down)
        wsgu = jnp.concatenate([sh_w_gate[i], sh_w_up[i]], axis=-1).astype(BF16)
        x_pair = _combine(i == DEPTH - 1, n_ctx_tiles, x_new, h2, y, w8.T, rloc.T, lists, mod,
                          final_norm, wsgu, sh_w_down[i].astype(BF16), mod_index)

    y_prompt = x_pair[0].reshape(bc, lc, d)
    y_sample = x_pair[1].reshape(bl, ll, d)
    return (y_prompt, y_sample, jnp.stack(new_ret, axis=1), jnp.stack(new_ssm, axis=1),
            jnp.stack(new_hg, axis=1))
```

```python
import functools

import numpy as np
import jax
import jax.numpy as jnp
from jax import lax
from jax.experimental import pallas as pl
from jax.experimental.pallas import tpu as pltpu

F32 = jnp.float32
BF16 = jnp.bfloat16
HIGHEST = lax.Precision.HIGHEST

D_MODEL = 1024
DEPTH = 2
GRID_W = 64
H_R, DK_R, DV_R = 4, 64, 128
H_M, P_M, N_M, G_M, D_CONV = 8, 64, 64, 2, 5
H_C, E_C, DV_C = 4, 128, 128
BR_W = 512
N_BRANCH = 3
RET_QK_W = H_R * DK_R
CONV_CH = BR_W + 2 * G_M * N_M
SSM_DT_W = 2 * H_M
IN_SPLITS = (RET_QK_W, RET_QK_W, BR_W, BR_W, BR_W, CONV_CH, SSM_DT_W, BR_W, 2 * BR_W, BR_W, BR_W,
             N_BRANCH * D_MODEL)
N_EXPERTS, TOP_K, N_GROUPS, TOPK_GROUPS = 64, 6, 8, 4
D_EXPERT = 256
D_SHARED = 256
ROUTED_SCALE = 2.5
EPS = 1e-6
ROPE_BASE = 10000.0

LANES = 128
SUBLANES = 8
CHUNK = 128
ROW_TILE = 256
PROJ_TILE = 512
MOE_ROWS = 512
VMEM_LIMIT = 56 * 1024 * 1024
NEG_BIG = -1e30

_CP = functools.partial(pltpu.CompilerParams, vmem_limit_bytes=VMEM_LIMIT)


def _sigmoid(x):
    return 1.0 / (1.0 + jnp.exp(-x))


def _silu(x):
    return x * _sigmoid(x)


def _softplus(x):
    return jnp.maximum(x, 0.0) + jnp.log1p(jnp.exp(-jnp.abs(x)))


def _log_sigmoid(x):
    return jnp.minimum(x, 0.0) - jnp.log1p(jnp.exp(-jnp.abs(x)))


def _dot(a, b):
    return jnp.dot(a, b, preferred_element_type=F32)


def _dot_nt(a, b):
    return lax.dot_general(a, b, (((1,), (1,)), ((), ())), preferred_element_type=F32)


def _dot_tn(a, b):
    return lax.dot_general(a, b, (((0,), (0,)), ((), ())), preferred_element_type=F32)


def _tri_dot(tri, x):
    hi = x.astype(BF16)
    r1 = x - hi.astype(F32)
    mid = r1.astype(BF16)
    lo = (r1 - mid.astype(F32)).astype(BF16)
    return _dot(tri, hi) + _dot(tri, mid) + _dot(tri, lo)


def _iota2(shape, dim):
    return lax.broadcasted_iota(jnp.int32, shape, dim)


def _rows(c):
    return pl.ds(pl.multiple_of(c * CHUNK, CHUNK), CHUNK)


def _const_spec(shape):
    nd = len(shape)
    return pl.BlockSpec(shape, lambda *_: (0,) * nd)


def _ada_kernel(c_ref, w_ref, b_ref, o_ref):
    a = _silu(c_ref[...])
    o_ref[0] = jnp.dot(a, w_ref[0], preferred_element_type=F32, precision=HIGHEST) + b_ref[0]


def _ada(c_all, ada_w, ada_b):
    depth, d, n = ada_w.shape
    tn = 1536
    return pl.pallas_call(
        _ada_kernel,
        grid=(depth, n // tn),
        in_specs=[pl.BlockSpec((SUBLANES, d), lambda i, j: (0, 0)),
                  pl.BlockSpec((1, d, tn), lambda i, j: (i, 0, j)),
                  pl.BlockSpec((1, 1, tn), lambda i, j: (i, 0, j))],
        out_specs=pl.BlockSpec((1, SUBLANES, tn), lambda i, j: (i, 0, j)),
        out_shape=jax.ShapeDtypeStruct((depth, SUBLANES, n), F32),
        compiler_params=_CP(dimension_semantics=("arbitrary", "arbitrary")),
        name="ada_mod",
    )(c_all, ada_w, ada_b.reshape(depth, 1, n))


def _mod_index(n_ctx_tiles, tiles_per_seq):
    def index(r):
        return (jnp.where(r < n_ctx_tiles, 0, 1 + (r - n_ctx_tiles) // tiles_per_seq), 0, 0)
    return index


def _modulated_norm(x, g, shift, scale):
    y = x * lax.rsqrt(jnp.mean(x * x, axis=-1, keepdims=True) + EPS) * g
    return y * (1.0 + scale) + shift


def _group_specs(cols, n_ctx_tiles, tile=ROW_TILE):
    return [pl.BlockSpec((tile, cols), lambda r: (jnp.minimum(r, n_ctx_tiles - 1), 0)),
            pl.BlockSpec((tile, cols), lambda r: (jnp.maximum(r - n_ctx_tiles, 0), 0))]


def _group_pick(n_ctx_tiles, ctx_ref, lat_ref):
    return jnp.where(pl.program_id(0) < n_ctx_tiles, ctx_ref[...], lat_ref[...])


def _inproj_kernel(n_ctx_tiles, xc_ref, xl_ref, mod_ref, n_ref, w0, w1, w2, w3, w4, o0, o1, o2, o3, o4):
    x = _group_pick(n_ctx_tiles, xc_ref, xl_ref)
    h = _modulated_norm(x, n_ref[...], mod_ref[0, 0:1, :], mod_ref[0, 1:2, :]).astype(BF16)
    for w, o in ((w0, o0), (w1, o1), (w2, o2), (w3, o3), (w4, o4)):
        o[...] = _dot(h, w[...]).astype(o.dtype)


def _inproj(x_pair, mod, norm_g, weights, out_dtypes, mod_index):
    xc, xl = x_pair
    d = xc.shape[1]
    t = xc.shape[0] + xl.shape[0]
    n_ctx_tiles = xc.shape[0] // PROJ_TILE
    w_specs = [pl.BlockSpec(w.shape, lambda r: (0, 0), pipeline_mode=pl.Buffered(1)) for w in weights]
    return pl.pallas_call(
        functools.partial(_inproj_kernel, n_ctx_tiles),
        grid=(t // PROJ_TILE,),
        in_specs=_group_specs(d, n_ctx_tiles, PROJ_TILE) + [pl.BlockSpec((1, 6, d), mod_index),
                                                            pl.BlockSpec((1, d), lambda r: (0, 0))] + w_specs,
        out_specs=[pl.BlockSpec((PROJ_TILE, w.shape[1]), lambda r: (r, 0)) for w in weights],
        out_shape=[jax.ShapeDtypeStruct((t, w.shape[1]), dt) for w, dt in zip(weights, out_dtypes)],
        compiler_params=_CP(dimension_semantics=("arbitrary",)),
        name="norm_inproj",
    )(xc, xl, mod, norm_g.reshape(1, d), *weights)


def _ret_kernel(is_ctx, n_chunks, in_ref, cos_ref, sin_ref, lgl_ref, lgv_ref, gn_ref, *rest):
    if is_ctx:
        o_ref, sfin_ref, qk_s, oacc, st, dm_s = rest
    else:
        s0_ref, o_ref, qk_s, oacc, st, dm_s = rest
    use_rope = not is_ctx
    qw = 2 * LANES
    ii = _iota2((CHUNK, CHUNK), 0)
    jj = _iota2((CHUNK, CHUNK), 1)
    dist = jnp.abs(ii - jj).astype(F32)
    for h in range(H_R):
        dm_s[h] = (jnp.where(ii >= jj, jnp.exp(dist * lgv_ref[0, h:h + 1, :]), 0.0)
                   + jnp.where(jj >= ii, jnp.exp(dist * lgv_ref[1, h:h + 1, :]), 0.0))
    st[...] = jnp.zeros(st.shape, F32) if is_ctx else s0_ref[0]

    lane_head = (_iota2((1, qw), 1) % LANES) // (DK_R // 2)
    rr = _iota2((CHUNK, qw), 0).astype(F32)
    lg_f = lgl_ref[0:1, :]
    lg_b = lgl_ref[1:2, :]

    def rope(x, cs, sn):
        x1, x2 = x[:, :LANES], x[:, LANES:]
        return jnp.concatenate([x1 * cs - x2 * sn, x1 * sn + x2 * cs], axis=1)

    def fwd(c, carry):
        rows = _rows(c)
        q = in_ref[rows, 0:qw].astype(F32)
        k = in_ref[rows, qw:2 * qw].astype(F32)
        if use_rope:
            cs, sn = cos_ref[rows, :], sin_ref[rows, :]
            q, k = rope(q, cs, sn), rope(k, cs, sn)
        qk_s[rows, 0:qw] = q
        qk_s[rows, qw:2 * qw] = k
        kb = k.astype(BF16)
        q_dec = (q * jnp.exp((rr + 1.0) * lg_f)).astype(BF16)
        k_dec = k * jnp.exp((CHUNK - 1.0 - rr) * lg_f)
        for h in range(H_R):
            hs = slice(h * DV_R, (h + 1) * DV_R)
            mh = lane_head == h
            vh = in_ref[rows, 2 * qw + h * DV_R:2 * qw + (h + 1) * DV_R]
            s = _dot_nt(jnp.where(mh, q, 0.0).astype(BF16), kb)
            intra = _dot((s * dm_s[h]).astype(BF16), vh)
            sf = st[0, h]
            oacc[rows, hs] = intra + _dot(q_dec, sf.astype(BF16))
            st[0, h] = (sf * jnp.exp(CHUNK * lgv_ref[0, h:h + 1, :])
                        + _dot_tn(jnp.where(mh, k_dec, 0.0).astype(BF16), vh))
        return carry

    lax.fori_loop(0, n_chunks, fwd, 0)

    def bwd(t, carry):
        rows = _rows(n_chunks - 1 - t)
        q = qk_s[rows, 0:qw]
        k = qk_s[rows, qw:2 * qw]
        q_dec = (q * jnp.exp((CHUNK - rr) * lg_b)).astype(BF16)
        k_dec = k * jnp.exp(rr * lg_b)
        for h in range(H_R):
            hs = slice(h * DV_R, (h + 1) * DV_R)
            mh = lane_head == h
            vh = in_ref[rows, 2 * qw + h * DV_R:2 * qw + (h + 1) * DV_R]
            sb = st[1, h]
            oacc[rows, hs] = oacc[rows, hs] + _dot(q_dec, sb.astype(BF16))
            st[1, h] = (sb * jnp.exp(CHUNK * lgv_ref[1, h:h + 1, :])
                        + _dot_tn(jnp.where(mh, k_dec, 0.0).astype(BF16), vh))
        return carry

    lax.fori_loop(0, n_chunks, bwd, 0)
    if is_ctx:
        half = DK_R // 2
        for d in range(2):
            for h in range(H_R):
                sfin_ref[0, d, h, 0:half, :] = st[d, h, h * half:(h + 1) * half, :]
                sfin_ref[0, d, h, half:DK_R, :] = st[d, h, LANES + h * half:LANES + (h + 1) * half, :]

    def fin(c, carry):
        rows = _rows(c)
        for h in range(H_R):
            hs = slice(h * DV_R, (h + 1) * DV_R)
            o = oacc[rows, hs]
            oc = o - jnp.mean(o, axis=-1, keepdims=True)
            y = oc * lax.rsqrt(jnp.mean(oc * oc, axis=-1, keepdims=True) + EPS) * gn_ref[:, hs]
            g = in_ref[rows, 4 * qw + h * DV_R:4 * qw + (h + 1) * DV_R].astype(F32)
            o_ref[rows, hs] = (y * _silu(g)).astype(o_ref.dtype)
        return carry

    lax.fori_loop(0, n_chunks, fin, 0)


def _state_io(s0, b, packed_shape, final_shape):
    lead = lambda shape: pl.BlockSpec((1,) + shape, lambda i: (i,) + (0,) * len(shape))
    if s0 is None:
        return [], [], [lead(final_shape)], [jax.ShapeDtypeStruct((b,) + final_shape, F32)]
    return [lead(packed_shape)], [s0], [], []


def _retention(ret_in, row0, b, l, cos4, sin4, lgl, lgv, gn, s0):
    is_ctx = s0 is None
    if is_ctx:
        cos4 = jnp.zeros((SUBLANES, LANES), F32)
        sin4 = cos4
        trig_spec = pl.BlockSpec((SUBLANES, LANES), lambda i: (0, 0))
    else:
        trig_spec = pl.BlockSpec((l, LANES), lambda i: (0, 0))
    blk0 = row0 // l
    packed = (2, H_R, 2 * LANES, DV_R)
    s_in, s_args, s_out, s_shapes = _state_io(s0, b, packed, (2, H_R, DK_R, DV_R))
    return pl.pallas_call(
        functools.partial(_ret_kernel, is_ctx, l // CHUNK),
        grid=(b,),
        in_specs=[pl.BlockSpec((l, ret_in.shape[1]), lambda i: (blk0 + i, 0)),
                  trig_spec, trig_spec,
                  _const_spec(lgl.shape), _const_spec(lgv.shape), _const_spec(gn.shape)] + s_in,
        out_specs=[pl.BlockSpec((l, BR_W), lambda i: (i, 0))] + s_out,
        out_shape=[jax.ShapeDtypeStruct((b * l, BR_W), BF16)] + s_shapes,
        scratch_shapes=[pltpu.VMEM((l, 4 * LANES), F32),
                        pltpu.VMEM((l, BR_W), F32),
                        pltpu.VMEM(packed, F32),
                        pltpu.VMEM((H_R, CHUNK, CHUNK), F32)],
        compiler_params=_CP(dimension_semantics=("arbitrary",)),
        name="retention",
    )(ret_in, cos4, sin4, lgl, lgv, gn, *s_args)


def _ssd_kernel(is_ctx, n_chunks, in_ref, dt_ref, cw_ref, cb_ref, dtb_ref, na_ref, dsk_ref, ng_ref, *rest):
    if is_ctx:
        o_ref, sfin_ref, pad_s, xc_s, y_s, dt_s, cumb_s, st = rest
    else:
        s0_ref, o_ref, pad_s, xc_s, y_s, dt_s, cumb_s, st = rest
    l = in_ref.shape[0]
    cw = 2 * BR_W
    halo = SUBLANES
    pad_s[0:halo, :] = jnp.zeros((halo, cw), F32)
    pad_s[l + halo:l + 2 * halo, :] = jnp.zeros((halo, cw), F32)

    def fill(c, carry):
        rows = _rows(c)
        dst = pl.ds(pl.multiple_of(c * CHUNK + halo, SUBLANES), CHUNK)
        pad_s[dst, :] = in_ref[rows, BR_W:BR_W + cw].astype(F32)
        return carry

    lax.fori_loop(0, n_chunks, fill, 0)
    st[...] = jnp.zeros(st.shape, F32) if is_ctx else s0_ref[0]

    ii = _iota2((CHUNK, CHUNK), 0)
    jj = _iota2((CHUNK, CHUNK), 1)
    tril = jnp.where(ii >= jj, 1.0, 0.0).astype(BF16)
    triu = jnp.where(jj >= ii, 1.0, 0.0).astype(BF16)
    lane_lo = jj < P_M
    row_lo = ii < N_M
    blockdiag = lane_lo == row_lo
    half = D_CONV // 2

    def pair_cols(vals, h0, h1):
        return jnp.where(lane_lo, vals[:, h0:h0 + 1], vals[:, h1:h1 + 1])

    def fwd(c, carry):
        rows = _rows(c)
        win = pad_s[pl.ds(pl.multiple_of(c * CHUNK, CHUNK), CHUNK + 2 * halo), :]
        acc = win[halo - half:halo - half + CHUNK, :] * cw_ref[0:1, :] + cb_ref[...]
        for w in range(1, D_CONV):
            acc = acc + win[halo - half + w:halo - half + w + CHUNK, :] * cw_ref[w:w + 1, :]
        xc = _silu(acc)
        xc_s[rows, :] = xc.astype(xc_s.dtype)
        x = xc[:, 0:BR_W]
        dt = _softplus(dt_ref[rows, :] + dtb_ref[...])
        la = dt * na_ref[...]
        cumf = _tri_dot(tril, la)
        cumb = _tri_dot(triu, la)
        dt_s[rows, :] = dt
        cumb_s[rows, :] = cumb
        cumf_t, cumb_t, dt_t = cumf.T, cumb.T, dt.T
        totf = cumf[CHUNK - 1:CHUNK, :]
        etotf = jnp.exp(totf)
        kdec = dt * jnp.exp(totf - cumf)
        qdec = jnp.exp(cumf)
        for g in range(G_M):
            b2 = xc[:, BR_W + g * LANES:BR_W + (g + 1) * LANES]
            c2 = xc[:, BR_W + (G_M + g) * LANES:BR_W + (G_M + g + 1) * LANES]
            cbm = _dot_nt(jnp.where(lane_lo, c2, 0.0).astype(BF16), b2.astype(BF16))
            for pp in range(2):
                p = 2 * g + pp
                h0, h1 = 2 * p, 2 * p + 1
                ps = slice(p * LANES, (p + 1) * LANES)
                ms = []
                for h in (h0, h1):
                    hb = H_M + h
                    mf = jnp.where(ii >= jj, jnp.exp(cumf[:, h:h + 1] - cumf_t[h:h + 1, :]), 0.0) * dt_t[h:h + 1, :]
                    mb = (jnp.where(jj >= ii, jnp.exp(cumb[:, hb:hb + 1] - cumb_t[hb:hb + 1, :]), 0.0)
                          * dt_t[hb:hb + 1, :])
                    ms.append((cbm * (mf + mb)).astype(BF16))
                xp = x[:, ps]
                xbd = jnp.concatenate([jnp.where(lane_lo, xp, 0.0), jnp.where(lane_lo, 0.0, xp)], axis=0)
                intra = _dot(jnp.concatenate(ms, axis=1), xbd.astype(BF16))
                sf = st[0, p]
                inter = _dot((c2 * pair_cols(qdec, h0, h1)).astype(BF16), sf.astype(BF16))
                y_s[rows, ps] = intra + inter
                kv = _dot_tn((b2 * pair_cols(kdec, h0, h1)).astype(BF16), xp.astype(BF16))
                arow = jnp.where(row_lo, etotf[:, h0:h0 + 1], etotf[:, h1:h1 + 1])
                st[0, p] = sf * arow + jnp.where(blockdiag, kv, 0.0)
        return carry

    lax.fori_loop(0, n_chunks, fwd, 0)

    def bwd(t, carry):
        rows = _rows(n_chunks - 1 - t)
        xc = xc_s[rows, :].astype(F32)
        x = xc[:, 0:BR_W]
        dt = dt_s[rows, :]
        cumb = cumb_s[rows, :]
        totb = cumb[0:1, :]
        etotb = jnp.exp(totb)
        kdec = dt * jnp.exp(totb - cumb)
        qdec = jnp.exp(cumb)
        for g in range(G_M):
            b2 = xc[:, BR_W + g * LANES:BR_W + (g + 1) * LANES]
            c2 = xc[:, BR_W + (G_M + g) * LANES:BR_W + (G_M + g + 1) * LANES]
            for pp in range(2):
                p = 2 * g + pp
                h0, h1 = H_M + 2 * p, H_M + 2 * p + 1
                ps = slice(p * LANES, (p + 1) * LANES)
                xp = x[:, ps]
                sb = st[1, p]
                y_s[rows, ps] = y_s[rows, ps] + _dot((c2 * pair_cols(qdec, h0, h1)).astype(BF16), sb.astype(BF16))
                kv = _dot_tn((b2 * pair_cols(kdec, h0, h1)).astype(BF16), xp.astype(BF16))
                arow = jnp.where(row_lo, etotb[:, h0:h0 + 1], etotb[:, h1:h1 + 1])
                st[1, p] = sb * arow + jnp.where(blockdiag, kv, 0.0)
        return carry

    lax.fori_loop(0, n_chunks, bwd, 0)
    if is_ctx:
        for d in range(2):
            for p in range(H_M // 2):
                pair = st[d, p]
                sfin_ref[0, d, 2 * p] = pair[0:N_M, 0:P_M]
                sfin_ref[0, d, 2 * p + 1] = pair[N_M:2 * N_M, P_M:2 * P_M]

    def fin(c, carry):
        rows = _rows(c)
        x = xc_s[rows, 0:BR_W].astype(F32)
        z = in_ref[rows, 0:BR_W].astype(F32)
        u = (y_s[rows, :] + x * dsk_ref[...]) * _silu(z)
        y = u * lax.rsqrt(jnp.mean(u * u, axis=-1, keepdims=True) + EPS) * ng_ref[...]
        o_ref[rows, :] = y.astype(o_ref.dtype)
        return carry

    lax.fori_loop(0, n_chunks, fin, 0)


def _ssd(ssm_in, prec, row0, b, l, cw, cb, dtb, na, dsk, ng, s0):
    blk0 = row0 // l
    dt_col = (prec.shape[1] - LANES) // LANES
    packed = (2, H_M // 2, LANES, LANES)
    s_in, s_args, s_out, s_shapes = _state_io(s0, b, packed, (2, H_M, N_M, P_M))
    return pl.pallas_call(
        functools.partial(_ssd_kernel, s0 is None, l // CHUNK),
        grid=(b,),
        in_specs=[pl.BlockSpec((l, ssm_in.shape[1]), lambda i: (blk0 + i, 0)),
                  pl.BlockSpec((l, LANES), lambda i: (blk0 + i, dt_col)),
                  _const_spec(cw.shape), _const_spec(cb.shape), _const_spec(dtb.shape),
                  _const_spec(na.shape), _const_spec(dsk.shape), _const_spec(ng.shape)] + s_in,
        out_specs=[pl.BlockSpec((l, BR_W), lambda i: (i, 0))] + s_out,
        out_shape=[jax.ShapeDtypeStruct((b * l, BR_W), BF16)] + s_shapes,
        scratch_shapes=[pltpu.VMEM((l + 2 * SUBLANES, 2 * BR_W), F32),
                        pltpu.VMEM((l, 2 * BR_W), BF16),
                        pltpu.VMEM((l, BR_W), F32),
                        pltpu.VMEM((l, LANES), F32),
                        pltpu.VMEM((l, LANES), F32),
                        pltpu.VMEM(packed, F32)],
        compiler_params=_CP(dimension_semantics=("arbitrary",)),
        name="ssd",
    )(ssm_in, prec, cw, cb, dtb, na, dsk, ng, *s_args)


_HG_LEVELS = (64, 32, 16, 8, 4, 2)
LOG2_E = 1.4426950408889634


def _block_ref_rows(x, m, row):
    size = 2 * m
    if size >= 2 * SUBLANES:
        return _bcast_block_row(x, size, row)
    out = _bcast_group_row(x, row)
    sub = _iota2((CHUNK, LANES), 0) % SUBLANES
    for b in range(1, SUBLANES // size):
        out = jnp.where(sub >= b * size, _bcast_group_row(x, b * size + row), out)
    return out


def _bcast_group_row(x, j):
    x3 = x.reshape(CHUNK // SUBLANES, SUBLANES, LANES)
    r = jnp.broadcast_to(x3[:, j:j + 1, :], x3.shape)
    return r.reshape(CHUNK, LANES)


def _bcast_block_row(x, size, j):
    pieces = [jnp.broadcast_to(x[b * size + j:b * size + j + 1, :], (size, LANES)) for b in range(CHUNK // size)]
    return pieces[0] if len(pieces) == 1 else jnp.concatenate(pieces, axis=0)


def _hgrn_kernel(is_ctx, n_chunks, in_ref, f_ref, llb_ref, oml_ref, l1m_ref, ng_ref, *rest):
    if is_ctx:
        o_ref, sfin_ref, oacc, st, cumb_s, keyb_s = rest
    else:
        s0_ref, o_ref, oacc, st, cumb_s, keyb_s = rest
    ii = _iota2((CHUNK, CHUNK), 0)
    jj = _iota2((CHUNK, CHUNK), 1)
    tril = jnp.where(ii >= jj, 1.0, 0.0).astype(BF16)
    triu = jnp.where(jj >= ii, 1.0, 0.0).astype(BF16)
    odd = (ii % 2) == 1
    same_pair = (ii // 2) == (jj // 2)
    ones_sq = jnp.ones((LANES, CHUNK), BF16)
    if is_ctx:
        st[...] = jnp.zeros(st.shape, F32)
    else:
        for d in range(2):
            for h in range(H_C):
                st[d, h] = s0_ref[0, d, h].T

    def gates(rows, d):
        fr = f_ref[rows, d * BR_W:(d + 1) * BR_W]
        ds_ = slice(d * BR_W, (d + 1) * BR_W)
        a = llb_ref[:, ds_]
        e = jnp.exp(-jnp.abs(fr))
        bterm = l1m_ref[:, ds_] + jnp.minimum(fr, 0.0) - jnp.log1p(e)
        logf = jnp.maximum(a, bterm) + jnp.log1p(jnp.exp(-jnp.abs(a - bterm)))
        key = oml_ref[:, ds_] * (jnp.where(fr >= 0.0, e, 1.0) / (1.0 + e))
        return logf * LOG2_E, key

    def fwd(c, carry):
        rows = _rows(c)
        lgf_all, key_f = gates(rows, 0)
        lgb_all, key_b = gates(rows, 1)
        cumf_all = _tri_dot(tril, lgf_all)
        cumb_all = _tri_dot(triu, lgb_all)
        cumb_s[rows, :] = cumb_all
        keyb_s[rows, :] = key_b
        for h in range(H_C):
            hs = slice(h * LANES, (h + 1) * LANES)
            q = in_ref[rows, hs].astype(F32)
            v = in_ref[rows, BR_W + h * LANES:BR_W + (h + 1) * LANES]
            cumf, cumb = cumf_all[:, hs], cumb_all[:, hs]
            kf, kb = key_f[:, hs], key_b[:, hs]
            sc = jnp.zeros((CHUNK, CHUNK), F32)
            for m in _HG_LEVELS:
                upper = (ii % (2 * m)) >= m
                same_block = (ii // (2 * m)) == (jj // (2 * m))
                ref_f = _block_ref_rows(cumf, m, m - 1)
                ref_b = _block_ref_rows(cumb, m, m)
                e_f = jnp.exp2(jnp.where(upper, cumf - ref_f, ref_f - cumf))
                e_b = jnp.exp2(jnp.where(upper, ref_b - cumb, cumb - ref_b))
                qcat = jnp.concatenate([jnp.where(upper, q * e_f, 0.0), jnp.where(upper, 0.0, q * e_b)], axis=1)
                kcat = jnp.concatenate([jnp.where(upper, 0.0, kf * e_f), jnp.where(upper, kb * e_b, 0.0)], axis=1)
                sc = sc + jnp.where(same_block, _dot_nt(qcat.astype(BF16), kcat.astype(BF16)), 0.0)
            qcat = jnp.concatenate([jnp.where(odd, q * jnp.exp2(lgf_all[:, hs]), 0.0),
                                    jnp.where(odd, 0.0, q * jnp.exp2(lgb_all[:, hs]))], axis=1)
            kcat = jnp.concatenate([jnp.where(odd, 0.0, kf), jnp.where(odd, kb, 0.0)], axis=1)
            sc = sc + jnp.where(same_pair, _dot_nt(qcat.astype(BF16), kcat.astype(BF16)), 0.0)
            sc = sc + jnp.where(ii == jj, _dot((q * (kf + kb)).astype(BF16), ones_sq), 0.0)
            intra = _dot(sc.astype(BF16), v)
            stf = st[0, h]
            totf = cumf[CHUNK - 1:CHUNK, :]
            inter = _dot_nt((q * jnp.exp2(cumf)).astype(BF16), stf.astype(BF16))
            oacc[rows, hs] = intra + inter
            st[0, h] = stf * jnp.exp2(totf) + _dot_tn(v, (kf * jnp.exp2(totf - cumf)).astype(BF16))
        return carry

    lax.fori_loop(0, n_chunks, fwd, 0)

    def bwd(t, carry):
        rows = _rows(n_chunks - 1 - t)
        cumb_all, key_b = cumb_s[rows, :], keyb_s[rows, :]
        for h in range(H_C):
            hs = slice(h * LANES, (h + 1) * LANES)
            q = in_ref[rows, hs].astype(F32)
            v = in_ref[rows, BR_W + h * LANES:BR_W + (h + 1) * LANES]
            cumb, kb = cumb_all[:, hs], key_b[:, hs]
            stb = st[1, h]
            totb = cumb[0:1, :]
            oacc[rows, hs] = oacc[rows, hs] + _dot_nt((q * jnp.exp2(cumb)).astype(BF16), stb.astype(BF16))
            st[1, h] = stb * jnp.exp2(totb) + _dot_tn(v, (kb * jnp.exp2(totb - cumb)).astype(BF16))
        return carry

    lax.fori_loop(0, n_chunks, bwd, 0)
    if is_ctx:
        for d in range(2):
            for h in range(H_C):
                sfin_ref[0, d, h] = st[d, h].T

    def fin(c, carry):
        rows = _rows(c)
        for h in range(H_C):
            hs = slice(h * LANES, (h + 1) * LANES)
            o = oacc[rows, hs]
            y = o * lax.rsqrt(jnp.mean(o * o, axis=-1, keepdims=True) + EPS) * ng_ref[:, hs]
            g = in_ref[rows, 2 * BR_W + h * LANES:2 * BR_W + (h + 1) * LANES].astype(F32)
            o_ref[rows, hs] = (y * _silu(g)).astype(o_ref.dtype)
        return carry

    lax.fori_loop(0, n_chunks, fin, 0)


def _hgrn(hg_in, prec, row0, b, l, llb, oml, l1m, ng, s0):
    blk0 = row0 // l
    s_in, s_args, s_out, s_shapes = _state_io(s0, b, (2, H_C, E_C, DV_C), (2, H_C, E_C, DV_C))
    return pl.pallas_call(
        functools.partial(_hgrn_kernel, s0 is None, l // CHUNK),
        grid=(b,),
        in_specs=[pl.BlockSpec((l, hg_in.shape[1]), lambda i: (blk0 + i, 0)),
                  pl.BlockSpec((l, 2 * BR_W), lambda i: (blk0 + i, 0)),
                  _const_spec(llb.shape), _const_spec(oml.shape), _const_spec(l1m.shape),
                  _const_spec(ng.shape)] + s_in,
        out_specs=[pl.BlockSpec((l, BR_W), lambda i: (i, 0))] + s_out,
        out_shape=[jax.ShapeDtypeStruct((b * l, BR_W), BF16)] + s_shapes,
        scratch_shapes=[pltpu.VMEM((l, BR_W), F32),
                        pltpu.VMEM((2, H_C, DV_C, E_C), F32),
                        pltpu.VMEM((l, BR_W), F32),
                        pltpu.VMEM((l, BR_W), F32)],
        compiler_params=_CP(dimension_semantics=("arbitrary",)),
        name="hgrn2",
    )(hg_in, prec, llb, oml, l1m, ng, *s_args)


def _merge_kernel(n_ctx_tiles, xc_ref, xl_ref, mod_ref, n2_ref, o0c, o0l, o1c, o1l, o2c, o2l, gl_ref, wb_ref, wo_ref,
                  xo_ref, h2_ref):
    d = xc_ref.shape[1]
    merged = jnp.zeros(xc_ref.shape, F32)
    for k, (oc, ol) in enumerate(((o0c, o0l), (o1c, o1l), (o2c, o2l))):
        gate = _sigmoid(gl_ref[:, k * d:(k + 1) * d].astype(F32))
        merged = merged + gate * _dot(_group_pick(n_ctx_tiles, oc, ol), wb_ref[k])
    mix = _dot(merged.astype(BF16), wo_ref[...])
    xn = _group_pick(n_ctx_tiles, xc_ref, xl_ref) + mod_ref[0, 2:3, :] * mix
    xo_ref[...] = xn
    h2 = _modulated_norm(xn, n2_ref[...], mod_ref[0, 3:4, :], mod_ref[0, 4:5, :])
    h2_ref[...] = h2.astype(h2_ref.dtype)


def _merge(x_pair, mod, norm_g, o_pairs, gl, wb, wo, mod_index):
    xc, xl = x_pair
    d = xc.shape[1]
    t = xc.shape[0] + xl.shape[0]
    n_ctx_tiles = xc.shape[0] // PROJ_TILE
    row = lambda r: (r, 0)
    o_specs, o_args = [], []
    for pair in o_pairs:
        o_specs += _group_specs(BR_W, n_ctx_tiles, PROJ_TILE)
        o_args += list(pair)
    return pl.pallas_call(
        functools.partial(_merge_kernel, n_ctx_tiles),
        grid=(t // PROJ_TILE,),
        in_specs=_group_specs(d, n_ctx_tiles, PROJ_TILE) + [pl.BlockSpec((1, 6, d), mod_index),
                                                            pl.BlockSpec((1, d), lambda r: (0, 0))] + o_specs + [
                  pl.BlockSpec((PROJ_TILE, N_BRANCH * d), row),
                  pl.BlockSpec(wb.shape, lambda r: (0, 0, 0), pipeline_mode=pl.Buffered(1)),
                  pl.BlockSpec(wo.shape, lambda r: (0, 0), pipeline_mode=pl.Buffered(1))],
        out_specs=[pl.BlockSpec((PROJ_TILE, d), row),
                   pl.BlockSpec((PROJ_TILE, d), row)],
        out_shape=[jax.ShapeDtypeStruct((t, d), F32),
                   jax.ShapeDtypeStruct((t, d), BF16)],
        compiler_params=_CP(dimension_semantics=("arbitrary",)),
        name="merge_outproj",
    )(xc, xl, mod, norm_g.reshape(1, d), *o_args, gl, wb, wo)


def _router_kernel(x_ref, mod_ref, n2_ref, rwt_ref, rb_ref, rloc_ref, w_ref, cnt_ref):
    tm = x_ref.shape[0]
    h2 = _modulated_norm(x_ref[...], n2_ref[...], mod_ref[0, 3:4, :], mod_ref[0, 4:5, :])
    logits = lax.dot_general(rwt_ref[...], h2, (((1,), (1,)), ((), ())),
                             preferred_element_type=F32, precision=HIGHEST)
    scores = _sigmoid(logits)
    biased = scores + rb_ref[...]
    gsz = N_EXPERTS // N_GROUPS
    neg_inf = -jnp.inf

    b3 = biased.reshape(N_GROUPS, gsz, tm)
    e_in_g = lax.broadcasted_iota(jnp.int32, (N_GROUPS, gsz, tm), 1)
    m1 = jnp.max(b3, axis=1, keepdims=True)
    first = jnp.min(jnp.where(b3 == m1, e_in_g, gsz), axis=1, keepdims=True)
    m2 = jnp.max(jnp.where(e_in_g == first, neg_inf, b3), axis=1, keepdims=True)
    gscore = m1 + m2

    g_iota = lax.broadcasted_iota(jnp.int32, (N_GROUPS, 1, tm), 0)
    chosen = jnp.zeros((N_GROUPS, 1, tm), jnp.int32)
    for _ in range(TOPK_GROUPS):
        m = jnp.max(gscore, axis=0, keepdims=True)
        first = jnp.min(jnp.where(gscore == m, g_iota, N_GROUPS), axis=0, keepdims=True)
        hit = g_iota == first
        chosen = jnp.where(hit, 1, chosen)
        gscore = jnp.where(hit, neg_inf, gscore)
    emask = jnp.broadcast_to(chosen, (N_GROUPS, gsz, tm)).reshape(N_EXPERTS, tm)

    cand = jnp.where(emask > 0, biased, neg_inf)
    e_iota = _iota2((N_EXPERTS, tm), 0)
    hits, ws = [], []
    for _ in range(TOP_K):
        m = jnp.max(cand, axis=0, keepdims=True)
        first = jnp.min(jnp.where(cand == m, e_iota, N_EXPERTS), axis=0, keepdims=True)
        hit = e_iota == first
        hits.append(hit)
        ws.append(jnp.sum(jnp.where(hit, scores, 0.0), axis=0, keepdims=True))
        cand = jnp.where(hit, neg_inf, cand)
    wsum = ws[0]
    for w in ws[1:]:
        wsum = wsum + w
    pad = SUBLANES - TOP_K
    w_ref[...] = jnp.concatenate([ROUTED_SCALE * w / wsum for w in ws] + [jnp.zeros((pad, tm), F32)], axis=0)

    picked = jnp.zeros((N_EXPERTS, tm), F32)
    for hit in hits:
        picked = jnp.where(hit, 1.0, picked)
    picked = picked.astype(BF16)
    earlier_tok = jnp.where(_iota2((tm, tm), 0) < _iota2((tm, tm), 1), 1.0, 0.0).astype(BF16)
    before_in_expert = _dot(picked, earlier_tok)
    count_rep = _dot(picked, jnp.ones((tm, tm), BF16))
    seg_rows = jnp.floor((count_rep + (BF16_ROWS - 1.0)) * (1.0 / BF16_ROWS)) * BF16_ROWS
    lower_expert = jnp.where(_iota2((N_EXPERTS, N_EXPERTS), 1) < _iota2((N_EXPERTS, N_EXPERTS), 0), 1.0, 0.0)
    pos = _dot(lower_expert.astype(BF16), seg_rows.astype(BF16)) + before_in_expert
    rloc = [jnp.sum(jnp.where(hit, pos, 0.0), axis=0, keepdims=True).astype(jnp.int32) for hit in hits]
    rloc_ref[...] = jnp.concatenate(rloc + [jnp.zeros((pad, tm), jnp.int32)], axis=0)
    cnt_ref[0] = _dot_nt(jnp.ones((SUBLANES, tm), BF16), picked)


def _router(x_new, mod, norm_g, rwt, rb, mod_index):
    t, d = x_new.shape
    n_tiles = t // ROW_TILE
    return pl.pallas_call(
        _router_kernel,
        grid=(n_tiles,),
        in_specs=[pl.BlockSpec((ROW_TILE, d), lambda r: (r, 0)),
                  pl.BlockSpec((1, 6, d), mod_index),
                  pl.BlockSpec((1, d), lambda r: (0, 0)),
                  pl.BlockSpec(rwt.shape, lambda r: (0, 0)),
                  pl.BlockSpec(rb.shape, lambda r: (0, 0))],
        out_specs=[pl.BlockSpec((SUBLANES, ROW_TILE), lambda r: (0, r)),
                   pl.BlockSpec((SUBLANES, ROW_TILE), lambda r: (0, r)),
                   pl.BlockSpec((1, SUBLANES, N_EXPERTS), lambda r: (r, 0, 0))],
        out_shape=[jax.ShapeDtypeStruct((SUBLANES, t), jnp.int32),
                   jax.ShapeDtypeStruct((SUBLANES, t), F32),
                   jax.ShapeDtypeStruct((n_tiles, SUBLANES, N_EXPERTS), F32)],
        compiler_params=_CP(dimension_semantics=("arbitrary",)),
        name="router",
    )(x_new, mod, norm_g.reshape(1, d), rwt, rb)


BF16_ROWS = 2 * SUBLANES
SEG_ROWS = 2 * BF16_ROWS
TILE_SLOTS = -(-(ROW_TILE * TOP_K + N_EXPERTS * (BF16_ROWS - 1)) // ROW_TILE) * ROW_TILE
_COPY_CLASSES = ((SEG_ROWS, TILE_SLOTS // SEG_ROWS), (BF16_ROWS, N_EXPERTS + 1))
_COPY_COUNTS_AT = 2 * sum(cap for _, cap in _COPY_CLASSES)
_COPY_LIST_LEN = -(-(_COPY_COUNTS_AT + len(_COPY_CLASSES)) // LANES) * LANES


def _copy_lists(cnt, lstart, goff):
    n_tiles, n_seg = cnt.shape
    segments = jnp.arange(n_seg, dtype=jnp.int32)

    def expand(per_e, cap):
        cum = jnp.cumsum(per_e, axis=1)
        j = jnp.arange(cap, dtype=jnp.int32)
        e_of_j = jnp.minimum(jnp.sum((cum[:, None, :] <= j[None, :, None]).astype(jnp.int32), axis=2), n_seg - 1)
        onehot = (e_of_j[:, :, None] == segments[None, None, :]).astype(jnp.int32)
        take = lambda v: jnp.sum(onehot * v[:, None, :], axis=2)
        return take, j[None, :] - take(cum - per_e), cum[:, -1]

    parts, counts = [], []
    for size, cap in _COPY_CLASSES:
        if size == SEG_ROWS:
            take, k, total = expand(cnt // SEG_ROWS, cap)
            off = k * SEG_ROWS
        else:
            take, _, total = expand((jnp.bitwise_and(cnt, size) > 0).astype(jnp.int32), cap)
            n = take(cnt)
            off = n - n % (2 * size)
        parts += [take(lstart) + off, take(goff) + off]
        counts.append(total)
    row = jnp.concatenate(parts + [jnp.stack(counts, axis=1)], axis=1)
    row = jnp.pad(row, ((0, 0), (0, _COPY_LIST_LEN - row.shape[1])))
    return row.astype(jnp.int32).reshape(n_tiles, 1, _COPY_LIST_LEN)


def _segment_copies(list_ref, make_copy):
    def rows(first, n_rows):
        return pl.ds(pl.multiple_of(first, BF16_ROWS), n_rows)

    base = 0
    for k, (size, cap) in enumerate(_COPY_CLASSES):
        def issue(j, carry, base=base, size=size, cap=cap):
            make_copy(rows(list_ref[0, 0, base + j], size), rows(list_ref[0, 0, base + cap + j], size)).start()
            return carry

        lax.fori_loop(0, list_ref[0, 0, _COPY_COUNTS_AT + k], issue, 0)
        base += 2 * cap


def _dispatch_kernel(h2_ref, rloc_ref, list_ref, xs_hbm, buf, sem):
    i = pl.program_id(0)
    n_tiles = pl.num_programs(0)
    slot = i % 2
    tm, d = h2_ref.shape

    def all_copies(s):
        return pltpu.make_async_copy(buf.at[s], xs_hbm.at[pl.ds(0, TILE_SLOTS)], sem.at[s])

    @pl.when(i >= 2)
    def _():
        all_copies(slot).wait()

    rid = _iota2((TILE_SLOTS, tm), 0)
    perm = jnp.zeros((TILE_SLOTS, tm), F32)
    for k in range(TOP_K):
        perm = jnp.where(rid == rloc_ref[k:k + 1, :], 1.0, perm)
    perm = perm.astype(BF16)
    for c in range(0, d, 2 * LANES):
        buf[slot, :, c:c + 2 * LANES] = _dot(perm, h2_ref[:, c:c + 2 * LANES]).astype(BF16)

    _segment_copies(list_ref, lambda loc, glob: pltpu.make_async_copy(buf.at[slot, loc], xs_hbm.at[glob], sem.at[slot]))

    @pl.when(i == n_tiles - 1)
    def _():
        all_copies(slot).wait()

        @pl.when(n_tiles >= 2)
        def _():
            all_copies(1 - slot).wait()


def _dispatch(h2, rloc, lists):
    t, d = h2.shape
    n_tiles = t // ROW_TILE
    return pl.pallas_call(
        _dispatch_kernel,
        grid=(n_tiles,),
        in_specs=[pl.BlockSpec((ROW_TILE, d), lambda i: (i, 0)),
                  pl.BlockSpec((SUBLANES, ROW_TILE), lambda i: (0, i)),
                  pl.BlockSpec((1, 1, _COPY_LIST_LEN), lambda i: (i, 0, 0), memory_space=pltpu.SMEM)],
        out_specs=pl.BlockSpec(memory_space=pl.ANY),
        out_shape=jax.ShapeDtypeStruct((n_tiles * TILE_SLOTS, d), BF16),
        scratch_shapes=[pltpu.VMEM((2, TILE_SLOTS, d), BF16),
                        pltpu.SemaphoreType.DMA((2,))],
        compiler_params=_CP(dimension_semantics=("arbitrary",)),
        name="moe_dispatch",
    )(h2, rloc, lists)


def _expert_kernel(blk_ref, exp_ref, lo_ref, hi_ref, n_ref, x_ref, wg_ref, wu_ref, wdn_ref, y_ref,
                   wgu_s, wd_s):
    i = pl.program_id(0)
    prev = jnp.maximum(i - 1, 0)
    valid = i < n_ref[0]
    is_expert = exp_ref[i] < N_EXPERTS
    new_expert = jnp.logical_or(i == 0, exp_ref[i] != exp_ref[prev])
    first_of_block = jnp.logical_or(i == 0, blk_ref[i] != blk_ref[prev])
    rows = _iota2(y_ref.shape, 0)
    mine = jnp.logical_and(rows >= lo_ref[i], rows < hi_ref[i])

    def put(y):
        @pl.when(first_of_block)
        def _():
            y_ref[...] = jnp.where(mine, y, 0.0).astype(y_ref.dtype)

        @pl.when(jnp.logical_not(first_of_block))
        def _():
            y_ref[...] = jnp.where(mine, y, y_ref[...].astype(F32)).astype(y_ref.dtype)

    @pl.when(jnp.logical_and(jnp.logical_and(valid, is_expert), new_expert))
    def _():
        wgu_s[:, :D_EXPERT] = wg_ref[0, 0].astype(BF16)
        wgu_s[:, D_EXPERT:] = wu_ref[0, 0].astype(BF16)
        wd_s[...] = wdn_ref[0, 0].astype(BF16)

    @pl.when(jnp.logical_and(valid, is_expert))
    def _():
        gu = _dot(x_ref[...], wgu_s[...])
        act = _silu(gu[:, :D_EXPERT]) * gu[:, D_EXPERT:]
        put(_dot(act.astype(BF16), wd_s[...]))

    @pl.when(jnp.logical_and(valid, jnp.logical_not(is_expert)))
    def _():
        put(jnp.zeros(y_ref.shape, F32))


def _experts(xs, items, layer, wg, wu, wdn):
    blk, exp, lo, hi, n_items = items
    d = xs.shape[1]
    by_block = lambda i, blk_r, exp_r, lo_r, hi_r, n_r: (blk_r[i], 0)
    by_expert = lambda i, blk_r, exp_r, lo_r, hi_r, n_r: (layer, jnp.minimum(exp_r[i], N_EXPERTS - 1), 0, 0)
    grid_spec = pltpu.PrefetchScalarGridSpec(
        num_scalar_prefetch=5,
        grid=(blk.shape[0],),
        in_specs=[pl.BlockSpec((MOE_ROWS, d), by_block),
                  pl.BlockSpec((1, 1) + wg.shape[2:], by_expert),
                  pl.BlockSpec((1, 1) + wu.shape[2:], by_expert),
                  pl.BlockSpec((1, 1) + wdn.shape[2:], by_expert)],
        out_specs=pl.BlockSpec((MOE_ROWS, d), by_block),
        scratch_shapes=[pltpu.VMEM((wg.shape[2], 2 * D_EXPERT), BF16),
                        pltpu.VMEM(wdn.shape[2:], BF16)])
    return pl.pallas_call(
        _expert_kernel,
        grid_spec=grid_spec,
        out_shape=jax.ShapeDtypeStruct(xs.shape, BF16),
        compiler_params=_CP(dimension_semantics=("arbitrary",)),
        name="routed_experts",
    )(blk, exp, lo, hi, n_items, xs, wg, wu, wdn)


def _moe_plan(cnt_tiles, t):
    p = t // ROW_TILE * TILE_SLOTS
    assert p % MOE_ROWS == 0
    n_seg = N_EXPERTS + 1
    cnt = cnt_tiles[:, 0, :].astype(jnp.int32)
    cnt = (cnt + BF16_ROWS - 1) // BF16_ROWS * BF16_ROWS
    cnt = jnp.concatenate([cnt, TILE_SLOTS - jnp.sum(cnt, axis=1, keepdims=True)], axis=1)
    totals = jnp.sum(cnt, axis=0)
    ends = jnp.cumsum(totals)
    starts = ends - totals
    goff = starts[None, :] + jnp.cumsum(cnt, axis=0) - cnt
    lstart = jnp.cumsum(cnt, axis=1) - cnt
    n_blk = jnp.where(totals > 0, (ends - 1) // MOE_ROWS - starts // MOE_ROWS + 1, 0)
    item_end = jnp.cumsum(n_blk)
    n_items = item_end[-1]
    max_items = p // MOE_ROWS + n_seg
    it = jnp.minimum(jnp.arange(max_items, dtype=jnp.int32), jnp.maximum(n_items - 1, 0))
    exp = jnp.minimum(jnp.sum((item_end[None, :] <= it[:, None]).astype(jnp.int32), axis=1), n_seg - 1)
    sel = (exp[:, None] == jnp.arange(n_seg, dtype=jnp.int32)[None, :]).astype(jnp.int32)
    pick = lambda v: jnp.sum(sel * v[None, :], axis=1)
    blk = jnp.clip(pick(starts) // MOE_ROWS + it - pick(item_end - n_blk), 0, p // MOE_ROWS - 1)
    lo = jnp.maximum(pick(starts), blk * MOE_ROWS) - blk * MOE_ROWS
    hi = jnp.minimum(pick(ends), (blk + 1) * MOE_ROWS) - blk * MOE_ROWS
    items = tuple(a.astype(jnp.int32) for a in (blk, exp, lo, hi, n_items.reshape(1)))
    return _copy_lists(cnt, lstart, goff), items


def _combine_kernel(final, n_ctx_tiles, x_ref, h2_ref, w_ref, rloc_ref, mod_ref, fn_ref, wsgu_ref, wsd_ref,
                    list_ref, list_next_ref, y_hbm, oc_ref, ol_ref, ybuf, sem):
    i = pl.program_id(0)
    n_tiles = pl.num_programs(0)
    slot = i % 2
    tm = x_ref.shape[0]

    def fetch(lists, s):
        _segment_copies(lists, lambda loc, glob: pltpu.make_async_copy(y_hbm.at[glob], ybuf.at[s, loc], sem.at[s]))

    @pl.when(i == 0)
    def _():
        fetch(list_ref, 0)

    @pl.when(i + 1 < n_tiles)
    def _():
        fetch(list_next_ref, 1 - slot)

    gu = _dot(h2_ref[...], wsgu_ref[...])
    shared = _dot((_silu(gu[:, :D_SHARED]) * gu[:, D_SHARED:]).astype(BF16), wsd_ref[...])

    rid = _iota2((tm, TILE_SLOTS), 1)
    w = w_ref[...]
    unsort = jnp.zeros((tm, TILE_SLOTS), F32)
    for k in range(TOP_K):
        unsort = jnp.where(rid == rloc_ref[:, k:k + 1], w[:, k:k + 1], unsort)
    unsort = unsort.astype(BF16)

    pltpu.make_async_copy(y_hbm.at[pl.ds(0, TILE_SLOTS)], ybuf.at[slot], sem.at[slot]).wait()
    routed = _dot(unsort, ybuf[slot])
    xo = x_ref[...] + mod_ref[0, 5:6, :] * (routed + shared)
    if final:
        xo = xo * lax.rsqrt(jnp.mean(xo * xo, axis=-1, keepdims=True) + EPS) * fn_ref[...]

    @pl.when(i < n_ctx_tiles)
    def _():
        oc_ref[...] = xo

    @pl.when(i >= n_ctx_tiles)
    def _():
        ol_ref[...] = xo


def _combine(final, n_ctx_tiles, x_new, h2, y, w_tk, rloc_tk, lists, mod, final_norm, wsgu, wsd, mod_index):
    t, d = x_new.shape
    n_tiles = t // ROW_TILE
    row = lambda r: (r, 0)
    cur = lambda r: (r, 0, 0)
    nxt = lambda r: (jnp.minimum(r + 1, n_tiles - 1), 0, 0)
    smem_tile = functools.partial(pl.BlockSpec, (1, 1, _COPY_LIST_LEN), memory_space=pltpu.SMEM)
    t_ctx = n_ctx_tiles * ROW_TILE
    return pl.pallas_call(
        functools.partial(_combine_kernel, final, n_ctx_tiles),
        grid=(n_tiles,),
        in_specs=[pl.BlockSpec((ROW_TILE, d), row),
                  pl.BlockSpec((ROW_TILE, d), row),
                  pl.BlockSpec((ROW_TILE, SUBLANES), row),
                  pl.BlockSpec((ROW_TILE, SUBLANES), row),
                  pl.BlockSpec((1, 6, d), mod_index),
                  pl.BlockSpec((1, d), lambda r: (0, 0)),
                  pl.BlockSpec(wsgu.shape, lambda r: (0, 0)),
                  pl.BlockSpec(wsd.shape, lambda r: (0, 0)),
                  smem_tile(cur), smem_tile(nxt),
                  pl.BlockSpec(memory_space=pl.ANY)],
        out_specs=_group_specs(d, n_ctx_tiles),
        out_shape=[jax.ShapeDtypeStruct((t_ctx, d), F32), jax.ShapeDtypeStruct((t - t_ctx, d), F32)],
        scratch_shapes=[pltpu.VMEM((2, TILE_SLOTS, d), BF16),
                        pltpu.SemaphoreType.DMA((2,))],
        compiler_params=_CP(dimension_semantics=("arbitrary",)),
        name="combine_shared",
    )(x_new, h2, w_tk, rloc_tk, mod, final_norm.reshape(1, d), wsgu, wsd, lists, lists, y)


def _split_rotary_halves(w):
    rows = w.shape[0]
    return w.reshape(rows, H_R, 2, DK_R // 2).transpose(0, 2, 1, 3).reshape(rows, RET_QK_W)


def _ret_state_rows():
    half = DK_R // 2
    rows = np.zeros((H_R, DK_R), np.int32)
    for h in range(H_R):
        for j in range(DK_R):
            rows[h, j] = (j // half) * LANES + h * half + j % half
    return rows


def _repeat_bc_groups(a, axis):
    take = lambda lo, hi: lax.slice_in_dim(a, lo, hi, axis=axis)
    parts = [take(0, BR_W)]
    for base in (BR_W, BR_W + G_M * N_M):
        for g in range(G_M):
            grp = take(base + g * N_M, base + (g + 1) * N_M)
            parts += [grp, grp]
    return jnp.concatenate(parts, axis=axis)


def _pad_lanes(v):
    return jnp.zeros((1, LANES), F32).at[0, :v.shape[0]].set(v.astype(F32))


def _ret_state_pack(s):
    rows = _ret_state_rows()
    src = np.full((H_R, 2 * LANES), DK_R, np.int32)
    for h in range(H_R):
        src[h, rows[h]] = np.arange(DK_R)
    sz = jnp.concatenate([s, jnp.zeros(s.shape[:3] + (1, DV_R), s.dtype)], axis=3)
    return sz[:, :, np.arange(H_R)[:, None], src, :]


def _ssm_state_pack(s):
    b = s.shape[0]
    sr = s.reshape(b, 2, H_M // 2, 2, N_M, 1, P_M)
    eye = jnp.eye(2, dtype=s.dtype).reshape(1, 1, 1, 2, 1, 2, 1)
    return (sr * eye).reshape(b, 2, H_M // 2, 2 * N_M, 2 * P_M)


def _grid_rope(l):
    rows = l // GRID_W
    row = jnp.repeat(jnp.arange(rows), GRID_W).astype(F32)
    col = (jnp.arange(rows * GRID_W) % GRID_W).astype(F32)
    n_freq = DK_R // 4
    freqs = ROPE_BASE ** (-jnp.arange(n_freq, dtype=F32) / n_freq)
    ang = jnp.concatenate([row[:, None] * freqs, col[:, None] * freqs], axis=-1)
    return jnp.tile(jnp.cos(ang), (1, H_R)), jnp.tile(jnp.sin(ang), (1, H_R))


def _layer_weights(i, w_in, ssm_conv_w, ssm_conv_b):
    cuts = np.cumsum(IN_SPLITS)[:-1]
    rq, rk, rv, rg, sz, sxbc, sdt, hq, hf, hi, hg, gl = jnp.split(w_in[i], cuts, axis=1)
    w_ret = jnp.concatenate([_split_rotary_halves(rq), _split_rotary_halves(rk) * (DK_R ** -0.5), rv, rg],
                            axis=1).astype(BF16)
    w_ssm = jnp.concatenate([sz, _repeat_bc_groups(sxbc, 1)], axis=1).astype(BF16)
    w_hg = jnp.concatenate([hq, hi, hg], axis=1).astype(BF16)
    w_prec = jnp.concatenate([hf, sdt, jnp.zeros((w_in.shape[1], LANES - SSM_DT_W), F32)], axis=1).astype(BF16)
    conv_w = _repeat_bc_groups(ssm_conv_w[i], 0).T
    conv_b = _repeat_bc_groups(ssm_conv_b[i], 0)[None, :]
    return (w_ret, w_ssm, w_hg, gl.astype(BF16), w_prec), conv_w, conv_b


def kernel(x_prompt, x_sample, state_ret, state_ssm, state_hgrn, c, c_ctx, ada_w, ada_b, norm1, norm2, w_in,
           ret_decay_logit, ret_gn, ssm_conv_w, ssm_conv_b, ssm_a_log, ssm_dt_bias, ssm_d, ssm_norm,
           hgrn_lb_logits, hgrn_norm, w_branch, w_out, router_w, router_b, exp_w_gate, exp_w_up,
           exp_w_down, sh_w_gate, sh_w_up, sh_w_down, final_norm):
    bc, lc, d = x_prompt.shape
    bl, ll, _ = x_sample.shape
    tc, tl = bc * lc, bl * ll
    t = tc + tl
    assert tc % PROJ_TILE == 0 and ll % PROJ_TILE == 0 and PROJ_TILE % ROW_TILE == 0
    assert lc % CHUNK == 0 and ll % CHUNK == 0 and tl % lc == 0 and tc % ll == 0 and 1 + bl <= SUBLANES
    mod_index = _mod_index(tc // ROW_TILE, ll // ROW_TILE)
    proj_mod_index = _mod_index(tc // PROJ_TILE, ll // PROJ_TILE)

    n_ctx_tiles = tc // ROW_TILE
    x_pair = (x_prompt.reshape(tc, d), x_sample.reshape(tl, d))
    c_all = jnp.concatenate([c_ctx[None, :], c, jnp.zeros((SUBLANES - 1 - bl, d), F32)], axis=0)
    mods = _ada(c_all, ada_w, ada_b)

    lb_all = jnp.cumsum(jax.nn.softmax(hgrn_lb_logits.astype(F32), axis=1), axis=1)
    lb_all = lb_all - lb_all[:, :1]
    cos4, sin4 = _grid_rope(ll)
    lane_head = (np.arange(2 * LANES) % LANES) // (DK_R // 2)

    new_ret, new_ssm, new_hg = [], [], []
    for i in range(DEPTH):
        mod = mods[i].reshape(SUBLANES, 6, d)
        weights, conv_w, conv_b = _layer_weights(i, w_in, ssm_conv_w, ssm_conv_b)
        ret_in, ssm_in, hg_in, gl, prec = _inproj(x_pair, mod, norm1[i], weights, (BF16, BF16, BF16, BF16, F32),
                                                  proj_mod_index)

        lg = jax.nn.log_sigmoid(ret_decay_logit[i].astype(F32))
        lgl = lg[:, lane_head]
        lgv = jnp.broadcast_to(lg[:, :, None], (2, H_R, LANES))
        gn = ret_gn[i][None, :]
        o_ret_c, s_ret = _retention(ret_in, 0, bc, lc, None, None, lgl, lgv, gn, None)
        o_ret_l, = _retention(ret_in, tc, bl, ll, cos4, sin4, lgl, lgv, gn, _ret_state_pack(state_ret[:, i]))
        new_ret.append(s_ret)

        dtb = _pad_lanes(ssm_dt_bias[i].reshape(-1))
        na = _pad_lanes(-jnp.exp(ssm_a_log[i].astype(F32)).reshape(-1))
        dsk = jnp.repeat(ssm_d[i], P_M)[None, :]
        ng = ssm_norm[i][None, :]
        o_ssm_c, s_ssm = _ssd(ssm_in, prec, 0, bc, lc, conv_w, conv_b, dtb, na, dsk, ng, None)
        o_ssm_l, = _ssd(ssm_in, prec, tc, bl, ll, conv_w, conv_b, dtb, na, dsk, ng,
                        _ssm_state_pack(state_ssm[:, i]))
        new_ssm.append(s_ssm)

        lb = lb_all[:, i]
        llb = jnp.log(lb).reshape(1, 2 * BR_W)
        oml = (1.0 - lb).reshape(1, 2 * BR_W)
        l1m = jnp.log1p(-lb).reshape(1, 2 * BR_W)
        hn = hgrn_norm[i][None, :]
        o_hg_c, s_hg = _hgrn(hg_in, prec, 0, bc, lc, llb, oml, l1m, hn, None)
        o_hg_l, = _hgrn(hg_in, prec, tc, bl, ll, llb, oml, l1m, hn, state_hgrn[:, i])
        new_hg.append(s_hg)

        x_new, h2 = _merge(x_pair, mod, norm2[i], ((o_ret_c, o_ret_l), (o_ssm_c, o_ssm_l), (o_hg_c, o_hg_l)), gl,
                           w_branch[i].astype(BF16), w_out[i].astype(BF16), proj_mod_index)

        rloc, w8, cnt_tiles = _router(x_new, mod, norm2[i], router_w[i].T, router_b[i][:, None], mod_index)
        lists, items = _moe_plan(cnt_tiles, t)
        xs = _dispatch(h2, rloc, lists)
        y = _experts(xs, items, i, exp_w_gate, exp_w_up, exp_w_down)
        wsgu = jnp.concatenate([sh_w_gate[i], sh_w_up[i]], axis=-1).astype(BF16)
        x_pair = _combine(i == DEPTH - 1, n_ctx_tiles, x_new, h2, y, w8.T, rloc.T, lists, mod,
                          final_norm, wsgu, sh_w_down[i].astype(BF16), mod_index)

    y_prompt = x_pair[0].reshape(bc, lc, d)
    y_sample = x_pair[1].reshape(bl, ll, d)
    return (y_prompt, y_sample, jnp.stack(new_ret, axis=1), jnp.stack(new_ssm, axis=1),
            jnp.stack(new_hg, axis=1))
```

```python
import functools

import numpy as np
import jax
import jax.numpy as jnp
from jax import lax
from jax.experimental import pallas as pl
from jax.experimental.pallas import tpu as pltpu

F32 = jnp.float32
BF16 = jnp.bfloat16
HIGHEST = lax.Precision.HIGHEST

D_MODEL = 1024
DEPTH = 2
GRID_W = 64
H_R, DK_R, DV_R = 4, 64, 128
H_M, P_M, N_M, G_M, D_CONV = 8, 64, 64, 2, 5
H_C, E_C, DV_C = 4, 128, 128
BR_W = 512
N_BRANCH = 3
RET_QK_W = H_R * DK_R
CONV_CH = BR_W + 2 * G_M * N_M
SSM_DT_W = 2 * H_M
IN_SPLITS = (RET_QK_W, RET_QK_W, BR_W, BR_W, BR_W, CONV_CH, SSM_DT_W, BR_W, 2 * BR_W, BR_W, BR_W,
             N_BRANCH * D_MODEL)
N_EXPERTS, TOP_K, N_GROUPS, TOPK_GROUPS = 64, 6, 8, 4
D_EXPERT = 256
D_SHARED = 256
ROUTED_SCALE = 2.5
EPS = 1e-6
ROPE_BASE = 10000.0

LANES = 128
SUBLANES = 8
CHUNK = 128
ROW_TILE = 256
PROJ_TILE = 512
MOE_ROWS = 512
VMEM_LIMIT = 56 * 1024 * 1024

_CP = functools.partial(pltpu.CompilerParams, vmem_limit_bytes=VMEM_LIMIT)


def _sigmoid(x):
    return 1.0 / (1.0 + jnp.exp(-x))


def _silu(x):
    return x * _sigmoid(x)


def _softplus(x):
    return jnp.maximum(x, 0.0) + jnp.log1p(jnp.exp(-jnp.abs(x)))


def _dot(a, b):
    return jnp.dot(a, b, preferred_element_type=F32)


def _dot_nt(a, b):
    return lax.dot_general(a, b, (((1,), (1,)), ((), ())), preferred_element_type=F32)


def _dot_tn(a, b):
    return lax.dot_general(a, b, (((0,), (0,)), ((), ())), preferred_element_type=F32)


def _tri_dot(tri, x):
    hi = x.astype(BF16)
    r1 = x - hi.astype(F32)
    mid = r1.astype(BF16)
    lo = (r1 - mid.astype(F32)).astype(BF16)
    return _dot(tri, hi) + _dot(tri, mid) + _dot(tri, lo)


def _iota2(shape, dim):
    return lax.broadcasted_iota(jnp.int32, shape, dim)


def _rows(c):
    return pl.ds(pl.multiple_of(c * CHUNK, CHUNK), CHUNK)


def _const_spec(shape):
    nd = len(shape)
    return pl.BlockSpec(shape, lambda *_: (0,) * nd)


def _ada_kernel(n_rows, ct_ref, w_ref, b_ref, o_ref):
    a = _silu(ct_ref[...])
    w = w_ref[0]
    rows = [jnp.sum(a[:, r:r + 1] * w, axis=0, keepdims=True) + b_ref[0] for r in range(n_rows)]
    rows.append(jnp.zeros((SUBLANES - n_rows, w.shape[1]), F32))
    o_ref[0] = jnp.concatenate(rows, axis=0)


def _ada(c_all, n_rows, ada_w, ada_b):
    depth, d, n = ada_w.shape
    tn = 1536
    return pl.pallas_call(
        functools.partial(_ada_kernel, n_rows),
        grid=(depth, n // tn),
        in_specs=[pl.BlockSpec((d, SUBLANES), lambda i, j: (0, 0)),
                  pl.BlockSpec((1, d, tn), lambda i, j: (i, 0, j)),
                  pl.BlockSpec((1, 1, tn), lambda i, j: (i, 0, j))],
        out_specs=pl.BlockSpec((1, SUBLANES, tn), lambda i, j: (i, 0, j)),
        out_shape=jax.ShapeDtypeStruct((depth, SUBLANES, n), F32),
        compiler_params=_CP(dimension_semantics=("arbitrary", "arbitrary")),
        name="ada_mod",
    )(c_all.T, ada_w, ada_b.reshape(depth, 1, n))


def _mod_index(n_ctx_tiles, tiles_per_seq):
    def index(r):
        return (jnp.where(r < n_ctx_tiles, 0, 1 + (r - n_ctx_tiles) // tiles_per_seq), 0, 0)
    return index


def _modulated_norm(x, g, shift, scale):
    y = x * lax.rsqrt(jnp.mean(x * x, axis=-1, keepdims=True) + EPS) * g
    return y * (1.0 + scale) + shift


def _group_specs(cols, n_ctx_tiles, tile=ROW_TILE):
    return [pl.BlockSpec((tile, cols), lambda r: (jnp.minimum(r, n_ctx_tiles - 1), 0)),
            pl.BlockSpec((tile, cols), lambda r: (jnp.maximum(r - n_ctx_tiles, 0), 0))]


def _group_pick(n_ctx_tiles, ctx_ref, lat_ref):
    return jnp.where(pl.program_id(0) < n_ctx_tiles, ctx_ref[...], lat_ref[...])


def _inproj_kernel(n_ctx_tiles, xc_ref, xl_ref, mod_ref, n_ref, w0, w1, w2, w3, w4, o0, o1, o2, o3, o4):
    x = _group_pick(n_ctx_tiles, xc_ref, xl_ref)
    h = _modulated_norm(x, n_ref[...], mod_ref[0, 0:1, :], mod_ref[0, 1:2, :]).astype(BF16)
    for w, o in ((w0, o0), (w1, o1), (w2, o2), (w3, o3), (w4, o4)):
        o[...] = _dot(h, w[...]).astype(o.dtype)


def _inproj(x_pair, mod, norm_g, weights, out_dtypes, mod_index):
    xc, xl = x_pair
    d = xc.shape[1]
    t = xc.shape[0] + xl.shape[0]
    n_ctx_tiles = xc.shape[0] // PROJ_TILE
    w_specs = [pl.BlockSpec(w.shape, lambda r: (0, 0), pipeline_mode=pl.Buffered(1)) for w in weights]
    return pl.pallas_call(
        functools.partial(_inproj_kernel, n_ctx_tiles),
        grid=(t // PROJ_TILE,),
        in_specs=_group_specs(d, n_ctx_tiles, PROJ_TILE) + [pl.BlockSpec((1, 6, d), mod_index),
                                                            pl.BlockSpec((1, d), lambda r: (0, 0))] + w_specs,
        out_specs=[pl.BlockSpec((PROJ_TILE, w.shape[1]), lambda r: (r, 0)) for w in weights],
        out_shape=[jax.ShapeDtypeStruct((t, w.shape[1]), dt) for w, dt in zip(weights, out_dtypes)],
        compiler_params=_CP(dimension_semantics=("arbitrary",)),
        name="norm_inproj",
    )(xc, xl, mod, norm_g.reshape(1, d), *weights)


def _ret_kernel(is_ctx, n_chunks, in_ref, cos_ref, sin_ref, lgl_ref, lgv_ref, gn_ref, *rest):
    if is_ctx:
        o_ref, sfin_ref, qk_s, oacc, st, dm_s = rest
    else:
        s0_ref, o_ref, qk_s, oacc, st, dm_s = rest
    use_rope = not is_ctx
    qw = 2 * LANES
    ii = _iota2((CHUNK, CHUNK), 0)
    jj = _iota2((CHUNK, CHUNK), 1)
    dist = jnp.abs(ii - jj).astype(F32)
    for h in range(H_R):
        dm_s[h] = (jnp.where(ii >= jj, jnp.exp(dist * lgv_ref[0, h:h + 1, :]), 0.0)
                   + jnp.where(jj >= ii, jnp.exp(dist * lgv_ref[1, h:h + 1, :]), 0.0))
    st[...] = jnp.zeros(st.shape, F32) if is_ctx else s0_ref[0]

    lane_head = (_iota2((1, qw), 1) % LANES) // (DK_R // 2)
    rr = _iota2((CHUNK, qw), 0).astype(F32)
    lg_f = lgl_ref[0:1, :]
    lg_b = lgl_ref[1:2, :]

    def rope(x, cs, sn):
        x1, x2 = x[:, :LANES], x[:, LANES:]
        return jnp.concatenate([x1 * cs - x2 * sn, x1 * sn + x2 * cs], axis=1)

    def fwd(c, carry):
        rows = _rows(c)
        q = in_ref[rows, 0:qw].astype(F32)
        k = in_ref[rows, qw:2 * qw].astype(F32)
        if use_rope:
            cs, sn = cos_ref[rows, :], sin_ref[rows, :]
            q, k = rope(q, cs, sn), rope(k, cs, sn)
        qk_s[rows, 0:qw] = q
        qk_s[rows, qw:2 * qw] = k
        kb = k.astype(BF16)
        q_dec = (q * jnp.exp((rr + 1.0) * lg_f)).astype(BF16)
        k_dec = k * jnp.exp((CHUNK - 1.0 - rr) * lg_f)
        for h in range(H_R):
            hs = slice(h * DV_R, (h + 1) * DV_R)
            mh = lane_head == h
            vh = in_ref[rows, 2 * qw + h * DV_R:2 * qw + (h + 1) * DV_R]
            s = _dot_nt(jnp.where(mh, q, 0.0).astype(BF16), kb)
            intra = _dot((s * dm_s[h]).astype(BF16), vh)
            sf = st[0, h]
            oacc[rows, hs] = intra + _dot(q_dec, sf.astype(BF16))
            st[0, h] = (sf * jnp.exp(CHUNK * lgv_ref[0, h:h + 1, :])
                        + _dot_tn(jnp.where(mh, k_dec, 0.0).astype(BF16), vh))
        return carry

    lax.fori_loop(0, n_chunks, fwd, 0)

    def bwd(t, carry):
        rows = _rows(n_chunks - 1 - t)
        q = qk_s[rows, 0:qw]
        k = qk_s[rows, qw:2 * qw]
        q_dec = (q * jnp.exp((CHUNK - rr) * lg_b)).astype(BF16)
        k_dec = k * jnp.exp(rr * lg_b)
        for h in range(H_R):
            hs = slice(h * DV_R, (h + 1) * DV_R)
            mh = lane_head == h
            vh = in_ref[rows, 2 * qw + h * DV_R:2 * qw + (h + 1) * DV_R]
            sb = st[1, h]
            oacc[rows, hs] = oacc[rows, hs] + _dot(q_dec, sb.astype(BF16))
            st[1, h] = (sb * jnp.exp(CHUNK * lgv_ref[1, h:h + 1, :])
                        + _dot_tn(jnp.where(mh, k_dec, 0.0).astype(BF16), vh))
        return carry

    lax.fori_loop(0, n_chunks, bwd, 0)
    if is_ctx:
        half = DK_R // 2
        for d in range(2):
            for h in range(H_R):
                sfin_ref[0, d, h, 0:half, :] = st[d, h, h * half:(h + 1) * half, :]
                sfin_ref[0, d, h, half:DK_R, :] = st[d, h, LANES + h * half:LANES + (h + 1) * half, :]

    def fin(c, carry):
        rows = _rows(c)
        for h in range(H_R):
            hs = slice(h * DV_R, (h + 1) * DV_R)
            o = oacc[rows, hs]
            oc = o - jnp.mean(o, axis=-1, keepdims=True)
            y = oc * lax.rsqrt(jnp.mean(oc * oc, axis=-1, keepdims=True) + EPS) * gn_ref[:, hs]
            g = in_ref[rows, 4 * qw + h * DV_R:4 * qw + (h + 1) * DV_R].astype(F32)
            o_ref[rows, hs] = (y * _silu(g)).astype(o_ref.dtype)
        return carry

    lax.fori_loop(0, n_chunks, fin, 0)


def _state_io(s0, b, packed_shape, final_shape):
    lead = lambda shape: pl.BlockSpec((1,) + shape, lambda i: (i,) + (0,) * len(shape))
    if s0 is None:
        return [], [], [lead(final_shape)], [jax.ShapeDtypeStruct((b,) + final_shape, F32)]
    return [lead(packed_shape)], [s0], [], []


def _retention(ret_in, row0, b, l, cos4, sin4, lgl, lgv, gn, s0):
    is_ctx = s0 is None
    if is_ctx:
        cos4 = jnp.zeros((SUBLANES, LANES), F32)
        sin4 = cos4
        trig_spec = pl.BlockSpec((SUBLANES, LANES), lambda i: (0, 0))
    else:
        trig_spec = pl.BlockSpec((l, LANES), lambda i: (0, 0))
    blk0 = row0 // l
    packed = (2, H_R, 2 * LANES, DV_R)
    s_in, s_args, s_out, s_shapes = _state_io(s0, b, packed, (2, H_R, DK_R, DV_R))
    return pl.pallas_call(
        functools.partial(_ret_kernel, is_ctx, l // CHUNK),
        grid=(b,),
        in_specs=[pl.BlockSpec((l, ret_in.shape[1]), lambda i: (blk0 + i, 0)),
                  trig_spec, trig_spec,
                  _const_spec(lgl.shape), _const_spec(lgv.shape), _const_spec(gn.shape)] + s_in,
        out_specs=[pl.BlockSpec((l, BR_W), lambda i: (i, 0))] + s_out,
        out_shape=[jax.ShapeDtypeStruct((b * l, BR_W), BF16)] + s_shapes,
        scratch_shapes=[pltpu.VMEM((l, 4 * LANES), F32),
                        pltpu.VMEM((l, BR_W), F32),
                        pltpu.VMEM(packed, F32),
                        pltpu.VMEM((H_R, CHUNK, CHUNK), F32)],
        compiler_params=_CP(dimension_semantics=("arbitrary",)),
        name="retention",
    )(ret_in, cos4, sin4, lgl, lgv, gn, *s_args)


def _ssd_kernel(is_ctx, n_chunks, in_ref, dt_ref, cw_ref, cb_ref, dtb_ref, na_ref, dsk_ref, ng_ref, *rest):
    if is_ctx:
        o_ref, sfin_ref, pad_s, xc_s, y_s, dt_s, cumb_s, st = rest
    else:
        s0_ref, o_ref, pad_s, xc_s, y_s, dt_s, cumb_s, st = rest
    l = in_ref.shape[0]
    cw = 2 * BR_W
    halo = SUBLANES
    pad_s[0:halo, :] = jnp.zeros((halo, cw), F32)
    pad_s[l + halo:l + 2 * halo, :] = jnp.zeros((halo, cw), F32)

    def fill(c, carry):
        rows = _rows(c)
        dst = pl.ds(pl.multiple_of(c * CHUNK + halo, SUBLANES), CHUNK)
        pad_s[dst, :] = in_ref[rows, BR_W:BR_W + cw].astype(F32)
        return carry

    lax.fori_loop(0, n_chunks, fill, 0)
    st[...] = jnp.zeros(st.shape, F32) if is_ctx else s0_ref[0]

    ii = _iota2((CHUNK, CHUNK), 0)
    jj = _iota2((CHUNK, CHUNK), 1)
    tril = jnp.where(ii >= jj, 1.0, 0.0).astype(BF16)
    triu = jnp.where(jj >= ii, 1.0, 0.0).astype(BF16)
    lane_lo = jj < P_M
    row_lo = ii < N_M
    blockdiag = lane_lo == row_lo
    half = D_CONV // 2

    def pair_cols(vals, h0, h1):
        return jnp.where(lane_lo, vals[:, h0:h0 + 1], vals[:, h1:h1 + 1])

    def fwd(c, carry):
        rows = _rows(c)
        win = pad_s[pl.ds(pl.multiple_of(c * CHUNK, CHUNK), CHUNK + 2 * halo), :]
        acc = win[halo - half:halo - half + CHUNK, :] * cw_ref[0:1, :] + cb_ref[...]
        for w in range(1, D_CONV):
            acc = acc + win[halo - half + w:halo - half + w + CHUNK, :] * cw_ref[w:w + 1, :]
        xc = _silu(acc)
        xc_s[rows, :] = xc.astype(xc_s.dtype)
        x = xc[:, 0:BR_W]
        dt = _softplus(dt_ref[rows, :] + dtb_ref[...])
        la = dt * na_ref[...]
        cumf = _tri_dot(tril, la)
        cumb = _tri_dot(triu, la)
        dt_s[rows, :] = dt
        cumb_s[rows, :] = cumb
        cumf_t, cumb_t, dt_t = cumf.T, cumb.T, dt.T
        totf = cumf[CHUNK - 1:CHUNK, :]
        etotf = jnp.exp(totf)
        kdec = dt * jnp.exp(totf - cumf)
        qdec = jnp.exp(cumf)
        for g in range(G_M):
            b2 = xc[:, BR_W + g * LANES:BR_W + (g + 1) * LANES]
            c2 = xc[:, BR_W + (G_M + g) * LANES:BR_W + (G_M + g + 1) * LANES]
            cbm = _dot_nt(jnp.where(lane_lo, c2, 0.0).astype(BF16), b2.astype(BF16))
            for pp in range(2):
                p = 2 * g + pp
                h0, h1 = 2 * p, 2 * p + 1
                ps = slice(p * LANES, (p + 1) * LANES)
                ms = []
                for h in (h0, h1):
                    hb = H_M + h
                    mf = jnp.where(ii >= jj, jnp.exp(cumf[:, h:h + 1] - cumf_t[h:h + 1, :]), 0.0) * dt_t[h:h + 1, :]
                    mb = (jnp.where(jj >= ii, jnp.exp(cumb[:, hb:hb + 1] - cumb_t[hb:hb + 1, :]), 0.0)
                          * dt_t[hb:hb + 1, :])
                    ms.append((cbm * (mf + mb)).astype(BF16))
                xp = x[:, ps]
                xbd = jnp.concatenate([jnp.where(lane_lo, xp, 0.0), jnp.where(lane_lo, 0.0, xp)], axis=0)
                intra = _dot(jnp.concatenate(ms, axis=1), xbd.astype(BF16))
                sf = st[0, p]
                inter = _dot((c2 * pair_cols(qdec, h0, h1)).astype(BF16), sf.astype(BF16))
                y_s[rows, ps] = intra + inter
                kv = _dot_tn((b2 * pair_cols(kdec, h0, h1)).astype(BF16), xp.astype(BF16))
                arow = jnp.where(row_lo, etotf[:, h0:h0 + 1], etotf[:, h1:h1 + 1])
                st[0, p] = sf * arow + jnp.where(blockdiag, kv, 0.0)
        return carry

    lax.fori_loop(0, n_chunks, fwd, 0)

    def bwd(t, carry):
        rows = _rows(n_chunks - 1 - t)
        xc = xc_s[rows, :].astype(F32)
        x = xc[:, 0:BR_W]
        dt = dt_s[rows, :]
        cumb = cumb_s[rows, :]
        totb = cumb[0:1, :]
        etotb = jnp.exp(totb)
        kdec = dt * jnp.exp(totb - cumb)
        qdec = jnp.exp(cumb)
        for g in range(G_M):
            b2 = xc[:, BR_W + g * LANES:BR_W + (g + 1) * LANES]
            c2 = xc[:, BR_W + (G_M + g) * LANES:BR_W + (G_M + g + 1) * LANES]
            for pp in range(2):
                p = 2 * g + pp
                h0, h1 = H_M + 2 * p, H_M + 2 * p + 1
                ps = slice(p * LANES, (p + 1) * LANES)
                xp = x[:, ps]
                sb = st[1, p]
                y_s[rows, ps] = y_s[rows, ps] + _dot((c2 * pair_cols(qdec, h0, h1)).astype(BF16), sb.astype(BF16))
                kv = _dot_tn((b2 * pair_cols(kdec, h0, h1)).astype(BF16), xp.astype(BF16))
                arow = jnp.where(row_lo, etotb[:, h0:h0 + 1], etotb[:, h1:h1 + 1])
                st[1, p] = sb * arow + jnp.where(blockdiag, kv, 0.0)
        return carry

    lax.fori_loop(0, n_chunks, bwd, 0)
    if is_ctx:
        for d in range(2):
            for p in range(H_M // 2):
                pair = st[d, p]
                sfin_ref[0, d, 2 * p] = pair[0:N_M, 0:P_M]
                sfin_ref[0, d, 2 * p + 1] = pair[N_M:2 * N_M, P_M:2 * P_M]

    def fin(c, carry):
        rows = _rows(c)
        x = xc_s[rows, 0:BR_W].astype(F32)
        z = in_ref[rows, 0:BR_W].astype(F32)
        u = (y_s[rows, :] + x * dsk_ref[...]) * _silu(z)
        y = u * lax.rsqrt(jnp.mean(u * u, axis=-1, keepdims=True) + EPS) * ng_ref[...]
        o_ref[rows, :] = y.astype(o_ref.dtype)
        return carry

    lax.fori_loop(0, n_chunks, fin, 0)


def _ssd(ssm_in, prec, row0, b, l, cw, cb, dtb, na, dsk, ng, s0):
    blk0 = row0 // l
    dt_col = (prec.shape[1] - LANES) // LANES
    packed = (2, H_M // 2, LANES, LANES)
    s_in, s_args, s_out, s_shapes = _state_io(s0, b, packed, (2, H_M, N_M, P_M))
    return pl.pallas_call(
        functools.partial(_ssd_kernel, s0 is None, l // CHUNK),
        grid=(b,),
        in_specs=[pl.BlockSpec((l, ssm_in.shape[1]), lambda i: (blk0 + i, 0)),
                  pl.BlockSpec((l, LANES), lambda i: (blk0 + i, dt_col)),
                  _const_spec(cw.shape), _const_spec(cb.shape), _const_spec(dtb.shape),
                  _const_spec(na.shape), _const_spec(dsk.shape), _const_spec(ng.shape)] + s_in,
        out_specs=[pl.BlockSpec((l, BR_W), lambda i: (i, 0))] + s_out,
        out_shape=[jax.ShapeDtypeStruct((b * l, BR_W), BF16)] + s_shapes,
        scratch_shapes=[pltpu.VMEM((l + 2 * SUBLANES, 2 * BR_W), F32),
                        pltpu.VMEM((l, 2 * BR_W), BF16),
                        pltpu.VMEM((l, BR_W), F32),
                        pltpu.VMEM((l, LANES), F32),
                        pltpu.VMEM((l, LANES), F32),
                        pltpu.VMEM(packed, F32)],
        compiler_params=_CP(dimension_semantics=("arbitrary",)),
        name="ssd",
    )(ssm_in, prec, cw, cb, dtb, na, dsk, ng, *s_args)


_HG_LEVELS = (64, 32, 16, 8, 4, 2)
LOG2_E = 1.4426950408889634


def _block_ref_rows(x, m, row):
    size = 2 * m
    if size >= 2 * SUBLANES:
        return _bcast_block_row(x, size, row)
    out = _bcast_group_row(x, row)
    sub = _iota2((CHUNK, LANES), 0) % SUBLANES
    for b in range(1, SUBLANES // size):
        out = jnp.where(sub >= b * size, _bcast_group_row(x, b * size + row), out)
    return out


def _bcast_group_row(x, j):
    x3 = x.reshape(CHUNK // SUBLANES, SUBLANES, LANES)
    r = jnp.broadcast_to(x3[:, j:j + 1, :], x3.shape)
    return r.reshape(CHUNK, LANES)


def _bcast_block_row(x, size, j):
    pieces = [jnp.broadcast_to(x[b * size + j:b * size + j + 1, :], (size, LANES)) for b in range(CHUNK // size)]
    return pieces[0] if len(pieces) == 1 else jnp.concatenate(pieces, axis=0)


def _hgrn_kernel(is_ctx, n_chunks, in_ref, f_ref, llb_ref, oml_ref, l1m_ref, ng_ref, *rest):
    if is_ctx:
        o_ref, sfin_ref, oacc, st, cumb_s, keyb_s = rest
    else:
        s0_ref, o_ref, oacc, st, cumb_s, keyb_s = rest
    ii = _iota2((CHUNK, CHUNK), 0)
    jj = _iota2((CHUNK, CHUNK), 1)
    tril = jnp.where(ii >= jj, 1.0, 0.0).astype(BF16)
    triu = jnp.where(jj >= ii, 1.0, 0.0).astype(BF16)
    odd = (ii % 2) == 1
    same_pair = (ii // 2) == (jj // 2)
    ones_sq = jnp.ones((LANES, CHUNK), BF16)
    if is_ctx:
        st[...] = jnp.zeros(st.shape, F32)
    else:
        for d in range(2):
            for h in range(H_C):
                st[d, h] = s0_ref[0, d, h].T

    def gates(rows, d):
        fr = f_ref[rows, d * BR_W:(d + 1) * BR_W]
        ds_ = slice(d * BR_W, (d + 1) * BR_W)
        a = llb_ref[:, ds_]
        e = jnp.exp(-jnp.abs(fr))
        bterm = l1m_ref[:, ds_] + jnp.minimum(fr, 0.0) - jnp.log1p(e)
        logf = jnp.maximum(a, bterm) + jnp.log1p(jnp.exp(-jnp.abs(a - bterm)))
        key = oml_ref[:, ds_] * (jnp.where(fr >= 0.0, e, 1.0) / (1.0 + e))
        return logf * LOG2_E, key

    def fwd(c, carry):
        rows = _rows(c)
        lgf_all, key_f = gates(rows, 0)
        lgb_all, key_b = gates(rows, 1)
        cumf_all = _tri_dot(tril, lgf_all)
        cumb_all = _tri_dot(triu, lgb_all)
        cumb_s[rows, :] = cumb_all
        keyb_s[rows, :] = key_b
        for h in range(H_C):
            hs = slice(h * LANES, (h + 1) * LANES)
            q = in_ref[rows, hs].astype(F32)
            v = in_ref[rows, BR_W + h * LANES:BR_W + (h + 1) * LANES]
            cumf, cumb = cumf_all[:, hs], cumb_all[:, hs]
            kf, kb = key_f[:, hs], key_b[:, hs]
            sc = jnp.zeros((CHUNK, CHUNK), F32)
            for m in _HG_LEVELS:
                upper = (ii % (2 * m)) >= m
                same_block = (ii // (2 * m)) == (jj // (2 * m))
                ref_f = _block_ref_rows(cumf, m, m - 1)
                ref_b = _block_ref_rows(cumb, m, m)
                e_f = jnp.exp2(jnp.where(upper, cumf - ref_f, ref_f - cumf))
                e_b = jnp.exp2(jnp.where(upper, ref_b - cumb, cumb - ref_b))
                qcat = jnp.concatenate([jnp.where(upper, q * e_f, 0.0), jnp.where(upper, 0.0, q * e_b)], axis=1)
                kcat = jnp.concatenate([jnp.where(upper, 0.0, kf * e_f), jnp.where(upper, kb * e_b, 0.0)], axis=1)
                sc = sc + jnp.where(same_block, _dot_nt(qcat.astype(BF16), kcat.astype(BF16)), 0.0)
            qcat = jnp.concatenate([jnp.where(odd, q * jnp.exp2(lgf_all[:, hs]), 0.0),
                                    jnp.where(odd, 0.0, q * jnp.exp2(lgb_all[:, hs]))], axis=1)
            kcat = jnp.concatenate([jnp.where(odd, 0.0, kf), jnp.where(odd, kb, 0.0)], axis=1)
            sc = sc + jnp.where(same_pair, _dot_nt(qcat.astype(BF16), kcat.astype(BF16)), 0.0)
            sc = sc + jnp.where(ii == jj, _dot((q * (kf + kb)).astype(BF16), ones_sq), 0.0)
            intra = _dot(sc.astype(BF16), v)
            stf = st[0, h]
            totf = cumf[CHUNK - 1:CHUNK, :]
            inter = _dot_nt((q * jnp.exp2(cumf)).astype(BF16), stf.astype(BF16))
            oacc[rows, hs] = intra + inter
            st[0, h] = stf * jnp.exp2(totf) + _dot_tn(v, (kf * jnp.exp2(totf - cumf)).astype(BF16))
        return carry

    lax.fori_loop(0, n_chunks, fwd, 0)

    def bwd(t, carry):
        rows = _rows(n_chunks - 1 - t)
        cumb_all, key_b = cumb_s[rows, :], keyb_s[rows, :]
        for h in range(H_C):
            hs = slice(h * LANES, (h + 1) * LANES)
            q = in_ref[rows, hs].astype(F32)
            v = in_ref[rows, BR_W + h * LANES:BR_W + (h + 1) * LANES]
            cumb, kb = cumb_all[:, hs], key_b[:, hs]
            stb = st[1, h]
            totb = cumb[0:1, :]
            oacc[rows, hs] = oacc[rows, hs] + _dot_nt((q * jnp.exp2(cumb)).astype(BF16), stb.astype(BF16))
            st[1, h] = stb * jnp.exp2(totb) + _dot_tn(v, (kb * jnp.exp2(totb - cumb)).astype(BF16))
        return carry

    lax.fori_loop(0, n_chunks, bwd, 0)
    if is_ctx:
        for d in range(2):
            for h in range(H_C):
                sfin_ref[0, d, h] = st[d, h].T

    def fin(c, carry):
        rows = _rows(c)
        for h in range(H_C):
            hs = slice(h * LANES, (h + 1) * LANES)
            o = oacc[rows, hs]
            y = o * lax.rsqrt(jnp.mean(o * o, axis=-1, keepdims=True) + EPS) * ng_ref[:, hs]
            g = in_ref[rows, 2 * BR_W + h * LANES:2 * BR_W + (h + 1) * LANES].astype(F32)
            o_ref[rows, hs] = (y * _silu(g)).astype(o_ref.dtype)
        return carry

    lax.fori_loop(0, n_chunks, fin, 0)


def _hgrn(hg_in, prec, row0, b, l, llb, oml, l1m, ng, s0):
    blk0 = row0 // l
    s_in, s_args, s_out, s_shapes = _state_io(s0, b, (2, H_C, E_C, DV_C), (2, H_C, E_C, DV_C))
    return pl.pallas_call(
        functools.partial(_hgrn_kernel, s0 is None, l // CHUNK),
        grid=(b,),
        in_specs=[pl.BlockSpec((l, hg_in.shape[1]), lambda i: (blk0 + i, 0)),
                  pl.BlockSpec((l, 2 * BR_W), lambda i: (blk0 + i, 0)),
                  _const_spec(llb.shape), _const_spec(oml.shape), _const_spec(l1m.shape),
                  _const_spec(ng.shape)] + s_in,
        out_specs=[pl.BlockSpec((l, BR_W), lambda i: (i, 0))] + s_out,
        out_shape=[jax.ShapeDtypeStruct((b * l, BR_W), BF16)] + s_shapes,
        scratch_shapes=[pltpu.VMEM((l, BR_W), F32),
                        pltpu.VMEM((2, H_C, DV_C, E_C), F32),
                        pltpu.VMEM((l, BR_W), F32),
                        pltpu.VMEM((l, BR_W), F32)],
        compiler_params=_CP(dimension_semantics=("arbitrary",)),
        name="hgrn2",
    )(hg_in, prec, llb, oml, l1m, ng, *s_args)


def _merge_kernel(n_ctx_tiles, xc_ref, xl_ref, mod_ref, n2_ref, o0c, o0l, o1c, o1l, o2c, o2l, gl_ref, wb_ref, wo_ref,
                  xo_ref, h2_ref):
    d = xc_ref.shape[1]
    merged = jnp.zeros(xc_ref.shape, F32)
    for k, (oc, ol) in enumerate(((o0c, o0l), (o1c, o1l), (o2c, o2l))):
        gate = _sigmoid(gl_ref[:, k * d:(k + 1) * d].astype(F32))
        merged = merged + gate * _dot(_group_pick(n_ctx_tiles, oc, ol), wb_ref[k])
    mix = _dot(merged.astype(BF16), wo_ref[...])
    xn = _group_pick(n_ctx_tiles, xc_ref, xl_ref) + mod_ref[0, 2:3, :] * mix
    xo_ref[...] = xn
    h2 = _modulated_norm(xn, n2_ref[...], mod_ref[0, 3:4, :], mod_ref[0, 4:5, :])
    h2_ref[...] = h2.astype(h2_ref.dtype)


def _merge(x_pair, mod, norm_g, o_pairs, gl, wb, wo, mod_index):
    xc, xl = x_pair
    d = xc.shape[1]
    t = xc.shape[0] + xl.shape[0]
    n_ctx_tiles = xc.shape[0] // PROJ_TILE
    row = lambda r: (r, 0)
    o_specs, o_args = [], []
    for pair in o_pairs:
        o_specs += _group_specs(BR_W, n_ctx_tiles, PROJ_TILE)
        o_args += list(pair)
    return pl.pallas_call(
        functools.partial(_merge_kernel, n_ctx_tiles),
        grid=(t // PROJ_TILE,),
        in_specs=_group_specs(d, n_ctx_tiles, PROJ_TILE) + [pl.BlockSpec((1, 6, d), mod_index),
                                                            pl.BlockSpec((1, d), lambda r: (0, 0))] + o_specs + [
                  pl.BlockSpec((PROJ_TILE, N_BRANCH * d), row),
                  pl.BlockSpec(wb.shape, lambda r: (0, 0, 0), pipeline_mode=pl.Buffered(1)),
                  pl.BlockSpec(wo.shape, lambda r: (0, 0), pipeline_mode=pl.Buffered(1))],
        out_specs=[pl.BlockSpec((PROJ_TILE, d), row),
                   pl.BlockSpec((PROJ_TILE, d), row)],
        out_shape=[jax.ShapeDtypeStruct((t, d), F32),
                   jax.ShapeDtypeStruct((t, d), BF16)],
        compiler_params=_CP(dimension_semantics=("arbitrary",)),
        name="merge_outproj",
    )(xc, xl, mod, norm_g.reshape(1, d), *o_args, gl, wb, wo)


def _router_kernel(x_ref, mod_ref, n2_ref, rwt_ref, rb_ref, rloc_ref, w_ref, cnt_ref):
    tm = x_ref.shape[0]
    h2 = _modulated_norm(x_ref[...], n2_ref[...], mod_ref[0, 3:4, :], mod_ref[0, 4:5, :])
    logits = lax.dot_general(rwt_ref[...], h2, (((1,), (1,)), ((), ())),
                             preferred_element_type=F32, precision=HIGHEST)
    scores = _sigmoid(logits)
    biased = scores + rb_ref[...]
    gsz = N_EXPERTS // N_GROUPS
    neg_inf = -jnp.inf

    b3 = biased.reshape(N_GROUPS, gsz, tm)
    e_in_g = lax.broadcasted_iota(jnp.int32, (N_GROUPS, gsz, tm), 1)
    m1 = jnp.max(b3, axis=1, keepdims=True)
    first = jnp.min(jnp.where(b3 == m1, e_in_g, gsz), axis=1, keepdims=True)
    m2 = jnp.max(jnp.where(e_in_g == first, neg_inf, b3), axis=1, keepdims=True)
    gscore = m1 + m2

    g_iota = lax.broadcasted_iota(jnp.int32, (N_GROUPS, 1, tm), 0)
    chosen = jnp.zeros((N_GROUPS, 1, tm), jnp.int32)
    for _ in range(TOPK_GROUPS):
        m = jnp.max(gscore, axis=0, keepdims=True)
        first = jnp.min(jnp.where(gscore == m, g_iota, N_GROUPS), axis=0, keepdims=True)
        hit = g_iota == first
        chosen = jnp.where(hit, 1, chosen)
        gscore = jnp.where(hit, neg_inf, gscore)
    emask = jnp.broadcast_to(chosen, (N_GROUPS, gsz, tm)).reshape(N_EXPERTS, tm)

    cand = jnp.where(emask > 0, biased, neg_inf)
    e_iota = _iota2((N_EXPERTS, tm), 0)
    hits, ws = [], []
    for _ in range(TOP_K):
        m = jnp.max(cand, axis=0, keepdims=True)
        first = jnp.min(jnp.where(cand == m, e_iota, N_EXPERTS), axis=0, keepdims=True)
        hit = e_iota == first
        hits.append(hit)
        ws.append(jnp.sum(jnp.where(hit, scores, 0.0), axis=0, keepdims=True))
        cand = jnp.where(hit, neg_inf, cand)
    wsum = ws[0]
    for w in ws[1:]:
        wsum = wsum + w
    pad = SUBLANES - TOP_K
    w_ref[...] = jnp.concatenate([ROUTED_SCALE * w / wsum for w in ws] + [jnp.zeros((pad, tm), F32)], axis=0)

    picked = jnp.zeros((N_EXPERTS, tm), F32)
    for hit in hits:
        picked = jnp.where(hit, 1.0, picked)
    picked = picked.astype(BF16)
    earlier_tok = jnp.where(_iota2((tm, tm), 0) < _iota2((tm, tm), 1), 1.0, 0.0).astype(BF16)
    before_in_expert = _dot(picked, earlier_tok)
    count_rep = _dot(picked, jnp.ones((tm, tm), BF16))
    seg_rows = jnp.floor((count_rep + (BF16_ROWS - 1.0)) * (1.0 / BF16_ROWS)) * BF16_ROWS
    lower_expert = jnp.where(_iota2((N_EXPERTS, N_EXPERTS), 1) < _iota2((N_EXPERTS, N_EXPERTS), 0), 1.0, 0.0)
    pos = _dot(lower_expert.astype(BF16), seg_rows.astype(BF16)) + before_in_expert
    rloc = [jnp.sum(jnp.where(hit, pos, 0.0), axis=0, keepdims=True).astype(jnp.int32) for hit in hits]
    rloc_ref[...] = jnp.concatenate(rloc + [jnp.zeros((pad, tm), jnp.int32)], axis=0)
    cnt_ref[0] = _dot_nt(jnp.ones((SUBLANES, tm), BF16), picked)


def _router(x_new, mod, norm_g, rwt, rb, mod_index):
    t, d = x_new.shape
    n_tiles = t // ROW_TILE
    return pl.pallas_call(
        _router_kernel,
        grid=(n_tiles,),
        in_specs=[pl.BlockSpec((ROW_TILE, d), lambda r: (r, 0)),
                  pl.BlockSpec((1, 6, d), mod_index),
                  pl.BlockSpec((1, d), lambda r: (0, 0)),
                  pl.BlockSpec(rwt.shape, lambda r: (0, 0)),
                  pl.BlockSpec(rb.shape, lambda r: (0, 0))],
        out_specs=[pl.BlockSpec((SUBLANES, ROW_TILE), lambda r: (0, r)),
                   pl.BlockSpec((SUBLANES, ROW_TILE), lambda r: (0, r)),
                   pl.BlockSpec((1, SUBLANES, N_EXPERTS), lambda r: (r, 0, 0))],
        out_shape=[jax.ShapeDtypeStruct((SUBLANES, t), jnp.int32),
                   jax.ShapeDtypeStruct((SUBLANES, t), F32),
                   jax.ShapeDtypeStruct((n_tiles, SUBLANES, N_EXPERTS), F32)],
        compiler_params=_CP(dimension_semantics=("arbitrary",)),
        name="router",
    )(x_new, mod, norm_g.reshape(1, d), rwt, rb)


BF16_ROWS = 2 * SUBLANES
SEG_ROWS = 2 * BF16_ROWS
TILE_SLOTS = -(-(ROW_TILE * TOP_K + N_EXPERTS * (BF16_ROWS - 1)) // ROW_TILE) * ROW_TILE
_COPY_CLASSES = ((SEG_ROWS, TILE_SLOTS // SEG_ROWS), (BF16_ROWS, N_EXPERTS + 1))
_COPY_COUNTS_AT = 2 * sum(cap for _, cap in _COPY_CLASSES)
_COPY_LIST_LEN = -(-(_COPY_COUNTS_AT + len(_COPY_CLASSES)) // LANES) * LANES


def _copy_lists(cnt, lstart, goff):
    n_tiles, n_seg = cnt.shape
    segments = jnp.arange(n_seg, dtype=jnp.int32)

    def expand(per_e, cap):
        cum = jnp.cumsum(per_e, axis=1)
        j = jnp.arange(cap, dtype=jnp.int32)
        e_of_j = jnp.minimum(jnp.sum((cum[:, None, :] <= j[None, :, None]).astype(jnp.int32), axis=2), n_seg - 1)
        onehot = (e_of_j[:, :, None] == segments[None, None, :]).astype(jnp.int32)
        take = lambda v: jnp.sum(onehot * v[:, None, :], axis=2)
        return take, j[None, :] - take(cum - per_e), cum[:, -1]

    parts, counts = [], []
    for size, cap in _COPY_CLASSES:
        if size == SEG_ROWS:
            take, k, total = expand(cnt // SEG_ROWS, cap)
            off = k * SEG_ROWS
        else:
            take, _, total = expand((jnp.bitwise_and(cnt, size) > 0).astype(jnp.int32), cap)
            n = take(cnt)
            off = n - n % (2 * size)
        parts += [take(lstart) + off, take(goff) + off]
        counts.append(total)
    row = jnp.concatenate(parts + [jnp.stack(counts, axis=1)], axis=1)
    row = jnp.pad(row, ((0, 0), (0, _COPY_LIST_LEN - row.shape[1])))
    return row.astype(jnp.int32).reshape(n_tiles, 1, _COPY_LIST_LEN)


def _segment_copies(list_ref, make_copy):
    def rows(first, n_rows):
        return pl.ds(pl.multiple_of(first, BF16_ROWS), n_rows)

    base = 0
    for k, (size, cap) in enumerate(_COPY_CLASSES):
        def issue(j, carry, base=base, size=size, cap=cap):
            make_copy(rows(list_ref[0, 0, base + j], size), rows(list_ref[0, 0, base + cap + j], size)).start()
            return carry

        lax.fori_loop(0, list_ref[0, 0, _COPY_COUNTS_AT + k], issue, 0)
        base += 2 * cap


def _dispatch_kernel(h2_ref, rloc_ref, list_ref, xs_hbm, buf, sem):
    i = pl.program_id(0)
    n_tiles = pl.num_programs(0)
    slot = i % 2
    tm, d = h2_ref.shape

    def all_copies(s):
        return pltpu.make_async_copy(buf.at[s], xs_hbm.at[pl.ds(0, TILE_SLOTS)], sem.at[s])

    @pl.when(i >= 2)
    def _():
        all_copies(slot).wait()

    rid = _iota2((TILE_SLOTS, tm), 0)
    perm = jnp.zeros((TILE_SLOTS, tm), F32)
    for k in range(TOP_K):
        perm = jnp.where(rid == rloc_ref[k:k + 1, :], 1.0, perm)
    perm = perm.astype(BF16)
    for c in range(0, d, 2 * LANES):
        buf[slot, :, c:c + 2 * LANES] = _dot(perm, h2_ref[:, c:c + 2 * LANES]).astype(BF16)

    _segment_copies(list_ref, lambda loc, glob: pltpu.make_async_copy(buf.at[slot, loc], xs_hbm.at[glob], sem.at[slot]))

    @pl.when(i == n_tiles - 1)
    def _():
        all_copies(slot).wait()

        @pl.when(n_tiles >= 2)
        def _():
            all_copies(1 - slot).wait()


def _dispatch(h2, rloc, lists):
    t, d = h2.shape
    n_tiles = t // ROW_TILE
    return pl.pallas_call(
        _dispatch_kernel,
        grid=(n_tiles,),
        in_specs=[pl.BlockSpec((ROW_TILE, d), lambda i: (i, 0)),
                  pl.BlockSpec((SUBLANES, ROW_TILE), lambda i: (0, i)),
                  pl.BlockSpec((1, 1, _COPY_LIST_LEN), lambda i: (i, 0, 0), memory_space=pltpu.SMEM)],
        out_specs=pl.BlockSpec(memory_space=pl.ANY),
        out_shape=jax.ShapeDtypeStruct((n_tiles * TILE_SLOTS, d), BF16),
        scratch_shapes=[pltpu.VMEM((2, TILE_SLOTS, d), BF16),
                        pltpu.SemaphoreType.DMA((2,))],
        compiler_params=_CP(dimension_semantics=("arbitrary",)),
        name="moe_dispatch",
    )(h2, rloc, lists)


def _expert_kernel(blk_ref, exp_ref, lo_ref, hi_ref, n_ref, x_ref, wg_ref, wu_ref, wdn_ref, y_ref,
                   wgu_s, wd_s):
    i = pl.program_id(0)
    prev = jnp.maximum(i - 1, 0)
    valid = i < n_ref[0]
    is_expert = exp_ref[i] < N_EXPERTS
    new_expert = jnp.logical_or(i == 0, exp_ref[i] != exp_ref[prev])
    first_of_block = jnp.logical_or(i == 0, blk_ref[i] != blk_ref[prev])
    rows = _iota2(y_ref.shape, 0)
    mine = jnp.logical_and(rows >= lo_ref[i], rows < hi_ref[i])

    def put(y):
        @pl.when(first_of_block)
        def _():
            y_ref[...] = jnp.where(mine, y, 0.0).astype(y_ref.dtype)

        @pl.when(jnp.logical_not(first_of_block))
        def _():
            y_ref[...] = jnp.where(mine, y, y_ref[...].astype(F32)).astype(y_ref.dtype)

    @pl.when(jnp.logical_and(jnp.logical_and(valid, is_expert), new_expert))
    def _():
        wgu_s[:, :D_EXPERT] = wg_ref[0, 0].astype(BF16)
        wgu_s[:, D_EXPERT:] = wu_ref[0, 0].astype(BF16)
        wd_s[...] = wdn_ref[0, 0].astype(BF16)

    @pl.when(jnp.logical_and(valid, is_expert))
    def _():
        gu = _dot(x_ref[...], wgu_s[...])
        act = _silu(gu[:, :D_EXPERT]) * gu[:, D_EXPERT:]
        put(_dot(act.astype(BF16), wd_s[...]))

    @pl.when(jnp.logical_and(valid, jnp.logical_not(is_expert)))
    def _():
        put(jnp.zeros(y_ref.shape, F32))


def _experts(xs, items, layer, wg, wu, wdn):
    blk, exp, lo, hi, n_items = items
    d = xs.shape[1]
    by_block = lambda i, blk_r, exp_r, lo_r, hi_r, n_r: (blk_r[i], 0)
    by_block_in = lambda i, blk_r, exp_r, lo_r, hi_r, n_r: (jnp.where(exp_r[i] < N_EXPERTS, blk_r[i], n_r[1]), 0)
    by_expert = lambda i, blk_r, exp_r, lo_r, hi_r, n_r: (layer, jnp.minimum(exp_r[i], N_EXPERTS - 1), 0, 0)
    grid_spec = pltpu.PrefetchScalarGridSpec(
        num_scalar_prefetch=5,
        grid=(blk.shape[0],),
        in_specs=[pl.BlockSpec((MOE_ROWS, d), by_block_in),
                  pl.BlockSpec((1, 1) + wg.shape[2:], by_expert),
                  pl.BlockSpec((1, 1) + wu.shape[2:], by_expert),
                  pl.BlockSpec((1, 1) + wdn.shape[2:], by_expert)],
        out_specs=pl.BlockSpec((MOE_ROWS, d), by_block),
        scratch_shapes=[pltpu.VMEM((wg.shape[2], 2 * D_EXPERT), BF16),
                        pltpu.VMEM(wdn.shape[2:], BF16)])
    return pl.pallas_call(
        _expert_kernel,
        grid_spec=grid_spec,
        out_shape=jax.ShapeDtypeStruct(xs.shape, BF16),
        compiler_params=_CP(dimension_semantics=("arbitrary",)),
        name="routed_experts",
    )(blk, exp, lo, hi, n_items, xs, wg, wu, wdn)


def _moe_plan(cnt_tiles, t):
    p = t // ROW_TILE * TILE_SLOTS
    assert p % MOE_ROWS == 0
    n_seg = N_EXPERTS + 1
    cnt = cnt_tiles[:, 0, :].astype(jnp.int32)
    cnt = (cnt + BF16_ROWS - 1) // BF16_ROWS * BF16_ROWS
    cnt = jnp.concatenate([cnt, TILE_SLOTS - jnp.sum(cnt, axis=1, keepdims=True)], axis=1)
    totals = jnp.sum(cnt, axis=0)
    ends = jnp.cumsum(totals)
    starts = ends - totals
    goff = starts[None, :] + jnp.cumsum(cnt, axis=0) - cnt
    lstart = jnp.cumsum(cnt, axis=1) - cnt
    n_blk = jnp.where(totals > 0, (ends - 1) // MOE_ROWS - starts // MOE_ROWS + 1, 0)
    item_end = jnp.cumsum(n_blk)
    n_items = item_end[-1]
    max_items = p // MOE_ROWS + n_seg
    it = jnp.minimum(jnp.arange(max_items, dtype=jnp.int32), jnp.maximum(n_items - 1, 0))
    exp = jnp.minimum(jnp.sum((item_end[None, :] <= it[:, None]).astype(jnp.int32), axis=1), n_seg - 1)
    sel = (exp[:, None] == jnp.arange(n_seg, dtype=jnp.int32)[None, :]).astype(jnp.int32)
    pick = lambda v: jnp.sum(sel * v[None, :], axis=1)
    blk = jnp.clip(pick(starts) // MOE_ROWS + it - pick(item_end - n_blk), 0, p // MOE_ROWS - 1)
    lo = jnp.maximum(pick(starts), blk * MOE_ROWS) - blk * MOE_ROWS
    hi = jnp.minimum(pick(ends), (blk + 1) * MOE_ROWS) - blk * MOE_ROWS
    keep_blk = blk[jnp.clip(item_end[N_EXPERTS - 1] - 1, 0, max_items - 1)]
    items = tuple(a.astype(jnp.int32) for a in (blk, exp, lo, hi, jnp.stack([n_items, keep_blk])))
    return _copy_lists(cnt, lstart, goff), items


def _combine_kernel(final, n_ctx_tiles, x_ref, h2_ref, w_ref, rloc_ref, mod_ref, fn_ref, wsgu_ref, wsd_ref,
                    list_ref, list_next_ref, y_hbm, oc_ref, ol_ref, ybuf, sem):
    i = pl.program_id(0)
    n_tiles = pl.num_programs(0)
    slot = i % 2
    tm = x_ref.shape[0]

    def fetch(lists, s):
        _segment_copies(lists, lambda loc, glob: pltpu.make_async_copy(y_hbm.at[glob], ybuf.at[s, loc], sem.at[s]))

    @pl.when(i == 0)
    def _():
        fetch(list_ref, 0)

    @pl.when(i + 1 < n_tiles)
    def _():
        fetch(list_next_ref, 1 - slot)

    gu = _dot(h2_ref[...], wsgu_ref[...])
    shared = _dot((_silu(gu[:, :D_SHARED]) * gu[:, D_SHARED:]).astype(BF16), wsd_ref[...])

    rid = _iota2((tm, TILE_SLOTS), 1)
    w = w_ref[...]
    unsort = jnp.zeros((tm, TILE_SLOTS), F32)
    for k in range(TOP_K):
        unsort = jnp.where(rid == rloc_ref[:, k:k + 1], w[:, k:k + 1], unsort)
    unsort = unsort.astype(BF16)

    pltpu.make_async_copy(y_hbm.at[pl.ds(0, TILE_SLOTS)], ybuf.at[slot], sem.at[slot]).wait()
    routed = _dot(unsort, ybuf[slot])
    xo = x_ref[...] + mod_ref[0, 5:6, :] * (routed + shared)
    if final:
        xo = xo * lax.rsqrt(jnp.mean(xo * xo, axis=-1, keepdims=True) + EPS) * fn_ref[...]

    @pl.when(i < n_ctx_tiles)
    def _():
        oc_ref[...] = xo

    @pl.when(i >= n_ctx_tiles)
    def _():
        ol_ref[...] = xo


def _combine(final, n_ctx_tiles, x_new, h2, y, w_tk, rloc_tk, lists, mod, final_norm, wsgu, wsd, mod_index):
    t, d = x_new.shape
    n_tiles = t // ROW_TILE
    row = lambda r: (r, 0)
    cur = lambda r: (r, 0, 0)
    nxt = lambda r: (jnp.minimum(r + 1, n_tiles - 1), 0, 0)
    smem_tile = functools.partial(pl.BlockSpec, (1, 1, _COPY_LIST_LEN), memory_space=pltpu.SMEM)
    t_ctx = n_ctx_tiles * ROW_TILE
    return pl.pallas_call(
        functools.partial(_combine_kernel, final, n_ctx_tiles),
        grid=(n_tiles,),
        in_specs=[pl.BlockSpec((ROW_TILE, d), row),
                  pl.BlockSpec((ROW_TILE, d), row),
                  pl.BlockSpec((ROW_TILE, SUBLANES), row),
                  pl.BlockSpec((ROW_TILE, SUBLANES), row),
                  pl.BlockSpec((1, 6, d), mod_index),
                  pl.BlockSpec((1, d), lambda r: (0, 0)),
                  pl.BlockSpec(wsgu.shape, lambda r: (0, 0)),
                  pl.BlockSpec(wsd.shape, lambda r: (0, 0)),
                  smem_tile(cur), smem_tile(nxt),
                  pl.BlockSpec(memory_space=pl.ANY)],
        out_specs=_group_specs(d, n_ctx_tiles),
        out_shape=[jax.ShapeDtypeStruct((t_ctx, d), F32), jax.ShapeDtypeStruct((t - t_ctx, d), F32)],
        scratch_shapes=[pltpu.VMEM((2, TILE_SLOTS, d), BF16),
                        pltpu.SemaphoreType.DMA((2,))],
        compiler_params=_CP(dimension_semantics=("arbitrary",)),
        name="combine_shared",
    )(x_new, h2, w_tk, rloc_tk, mod, final_norm.reshape(1, d), wsgu, wsd, lists, lists, y)


def _split_rotary_halves(w):
    rows = w.shape[0]
    return w.reshape(rows, H_R, 2, DK_R // 2).transpose(0, 2, 1, 3).reshape(rows, RET_QK_W)


def _ret_state_rows():
    half = DK_R // 2
    rows = np.zeros((H_R, DK_R), np.int32)
    for h in range(H_R):
        for j in range(DK_R):
            rows[h, j] = (j // half) * LANES + h * half + j % half
    return rows


def _repeat_bc_groups(a, axis):
    take = lambda lo, hi: lax.slice_in_dim(a, lo, hi, axis=axis)
    parts = [take(0, BR_W)]
    for base in (BR_W, BR_W + G_M * N_M):
        for g in range(G_M):
            grp = take(base + g * N_M, base + (g + 1) * N_M)
            parts += [grp, grp]
    return jnp.concatenate(parts, axis=axis)


def _pad_lanes(v):
    return jnp.zeros((1, LANES), F32).at[0, :v.shape[0]].set(v.astype(F32))


def _ret_state_pack(s):
    rows = _ret_state_rows()
    src = np.full((H_R, 2 * LANES), DK_R, np.int32)
    for h in range(H_R):
        src[h, rows[h]] = np.arange(DK_R)
    sz = jnp.concatenate([s, jnp.zeros(s.shape[:3] + (1, DV_R), s.dtype)], axis=3)
    return sz[:, :, np.arange(H_R)[:, None], src, :]


def _ssm_state_pack(s):
    b = s.shape[0]
    sr = s.reshape(b, 2, H_M // 2, 2, N_M, 1, P_M)
    eye = jnp.eye(2, dtype=s.dtype).reshape(1, 1, 1, 2, 1, 2, 1)
    return (sr * eye).reshape(b, 2, H_M // 2, 2 * N_M, 2 * P_M)


def _grid_rope(l):
    rows = l // GRID_W
    row = jnp.repeat(jnp.arange(rows), GRID_W).astype(F32)
    col = (jnp.arange(rows * GRID_W) % GRID_W).astype(F32)
    n_freq = DK_R // 4
    freqs = ROPE_BASE ** (-jnp.arange(n_freq, dtype=F32) / n_freq)
    ang = jnp.concatenate([row[:, None] * freqs, col[:, None] * freqs], axis=-1)
    return jnp.tile(jnp.cos(ang), (1, H_R)), jnp.tile(jnp.sin(ang), (1, H_R))


def _layer_weights(i, w_in, ssm_conv_w, ssm_conv_b):
    cuts = np.cumsum(IN_SPLITS)[:-1]
    rq, rk, rv, rg, sz, sxbc, sdt, hq, hf, hi, hg, gl = jnp.split(w_in[i], cuts, axis=1)
    w_ret = jnp.concatenate([_split_rotary_halves(rq), _split_rotary_halves(rk) * (DK_R ** -0.5), rv, rg],
                            axis=1).astype(BF16)
    w_ssm = jnp.concatenate([sz, _repeat_bc_groups(sxbc, 1)], axis=1).astype(BF16)
    w_hg = jnp.concatenate([hq, hi, hg], axis=1).astype(BF16)
    w_prec = jnp.concatenate([hf, sdt, jnp.zeros((w_in.shape[1], LANES - SSM_DT_W), F32)], axis=1).astype(BF16)
    conv_w = _repeat_bc_groups(ssm_conv_w[i], 0).T
    conv_b = _repeat_bc_groups(ssm_conv_b[i], 0)[None, :]
    return (w_ret, w_ssm, w_hg, gl.astype(BF16), w_prec), conv_w, conv_b


def kernel(x_prompt, x_sample, state_ret, state_ssm, state_hgrn, c, c_ctx, ada_w, ada_b, norm1, norm2, w_in,
           ret_decay_logit, ret_gn, ssm_conv_w, ssm_conv_b, ssm_a_log, ssm_dt_bias, ssm_d, ssm_norm,
           hgrn_lb_logits, hgrn_norm, w_branch, w_out, router_w, router_b, exp_w_gate, exp_w_up,
           exp_w_down, sh_w_gate, sh_w_up, sh_w_down, final_norm):
    bc, lc, d = x_prompt.shape
    bl, ll, _ = x_sample.shape
    tc, tl = bc * lc, bl * ll
    t = tc + tl
    assert tc % PROJ_TILE == 0 and ll % PROJ_TILE == 0 and PROJ_TILE % ROW_TILE == 0
    assert lc % CHUNK == 0 and ll % CHUNK == 0 and tl % lc == 0 and tc % ll == 0 and 1 + bl <= SUBLANES
    mod_index = _mod_index(tc // ROW_TILE, ll // ROW_TILE)
    proj_mod_index = _mod_index(tc // PROJ_TILE, ll // PROJ_TILE)

    n_ctx_tiles = tc // ROW_TILE
    x_pair = (x_prompt.reshape(tc, d), x_sample.reshape(tl, d))
    c_all = jnp.concatenate([c_ctx[None, :], c, jnp.zeros((SUBLANES - 1 - bl, d), F32)], axis=0)
    mods = _ada(c_all, 1 + bl, ada_w, ada_b)

    lb_all = jnp.cumsum(jax.nn.softmax(hgrn_lb_logits.astype(F32), axis=1), axis=1)
    lb_all = lb_all - lb_all[:, :1]
    cos4, sin4 = _grid_rope(ll)
    lane_head = (np.arange(2 * LANES) % LANES) // (DK_R // 2)

    new_ret, new_ssm, new_hg = [], [], []
    for i in range(DEPTH):
        mod = mods[i].reshape(SUBLANES, 6, d)
        weights, conv_w, conv_b = _layer_weights(i, w_in, ssm_conv_w, ssm_conv_b)
        ret_in, ssm_in, hg_in, gl, prec = _inproj(x_pair, mod, norm1[i], weights, (BF16, BF16, BF16, BF16, F32),
                                                  proj_mod_index)

        lg = jax.nn.log_sigmoid(ret_decay_logit[i].astype(F32))
        lgl = lg[:, lane_head]
        lgv = jnp.broadcast_to(lg[:, :, None], (2, H_R, LANES))
        gn = ret_gn[i][None, :]
        o_ret_c, s_ret = _retention(ret_in, 0, bc, lc, None, None, lgl, lgv, gn, None)
        o_ret_l, = _retention(ret_in, tc, bl, ll, cos4, sin4, lgl, lgv, gn, _ret_state_pack(state_ret[:, i]))
        new_ret.append(s_ret)

        dtb = _pad_lanes(ssm_dt_bias[i].reshape(-1))
        na = _pad_lanes(-jnp.exp(ssm_a_log[i].astype(F32)).reshape(-1))
        dsk = jnp.repeat(ssm_d[i], P_M)[None, :]
        ng = ssm_norm[i][None, :]
        o_ssm_c, s_ssm = _ssd(ssm_in, prec, 0, bc, lc, conv_w, conv_b, dtb, na, dsk, ng, None)
        o_ssm_l, = _ssd(ssm_in, prec, tc, bl, ll, conv_w, conv_b, dtb, na, dsk, ng,
                        _ssm_state_pack(state_ssm[:, i]))
        new_ssm.append(s_ssm)

        lb = lb_all[:, i]
        llb = jnp.log(lb).reshape(1, 2 * BR_W)
        oml = (1.0 - lb).reshape(1, 2 * BR_W)
        l1m = jnp.log1p(-lb).reshape(1, 2 * BR_W)
        hn = hgrn_norm[i][None, :]
        o_hg_c, s_hg = _hgrn(hg_in, prec, 0, bc, lc, llb, oml, l1m, hn, None)
        o_hg_l, = _hgrn(hg_in, prec, tc, bl, ll, llb, oml, l1m, hn, state_hgrn[:, i])
        new_hg.append(s_hg)

        x_new, h2 = _merge(x_pair, mod, norm2[i], ((o_ret_c, o_ret_l), (o_ssm_c, o_ssm_l), (o_hg_c, o_hg_l)), gl,
                           w_branch[i].astype(BF16), w_out[i].astype(BF16), proj_mod_index)

        rloc, w8, cnt_tiles = _router(x_new, mod, norm2[i], router_w[i].T, router_b[i][:, None], mod_index)
        lists, items = _moe_plan(cnt_tiles, t)
        xs = _dispatch(h2, rloc, lists)
        y = _experts(xs, items, i, exp_w_gate, exp_w_up, exp_w_down)
        wsgu = jnp.concatenate([sh_w_gate[i], sh_w_up[i]], axis=-1).astype(BF16)
        x_pair = _combine(i == DEPTH - 1, n_ctx_tiles, x_new, h2, y, w8.T, rloc.T, lists, mod,
                          final_norm, wsgu, sh_w_down[i].astype(BF16), mod_index)

    y_prompt = x_pair[0].reshape(bc, lc, d)
    y_sample = x_pair[1].reshape(bl, ll, d)
    return (y_prompt, y_sample, jnp.stack(new_ret, axis=1), jnp.stack(new_ssm, axis=1),
            jnp.stack(new_hg, axis=1))
```
